```python
import math
import jax
import jax.numpy as jnp
from jax import lax
import numpy as np

D_MODEL = 1024
BATCH = 8
SEQ = 2048
DEPTH = 1
DEC_BATCH = 128
DEC_SEQ = 1
PAST_LEN = 2048
PAGE_SIZE = 128

D_MIX = 2 * D_MODEL
D_M = D_MIX // 2
H_M = 4
DH_M = D_M // H_M
D_D = D_MIX // 4
H_D = 4
DV_D = D_D // H_D
DQK_D = DV_D // 2
D_C = D_MIX // 4
H_C = 4
DH_C = D_C // H_C
N_MEM = 256
D_IN = 4 * D_M + 2 * H_M + 3 * D_D + D_C + D_MIX
CHUNK = 128
Q_BLOCK = 128
ROPE_THETA = 10000.0
EPS = 1e-5
ALPHA = (2 * DEPTH) ** 0.25
BETA = (8 * DEPTH) ** -0.25
POOL_NUM = 5
POOL_DEN = 4

kernel_name = "hybrid_mlstm_diffattn_memxattn_step"


def split_in(z):
    sizes = (D_M, D_M, D_M, D_M, H_M, H_M, D_D, D_D, D_D, D_C, D_MIX)
    points = []
    acc = 0
    for s in sizes[:-1]:
        acc += s
        points.append(acc)
    return jnp.split(z, points, axis=-1)


def rope(t, pos):
    half = t.shape[-1] // 2
    inv_freq = ROPE_THETA ** (-jnp.arange(half, dtype=jnp.float32) / half)
    ang = pos.astype(jnp.float32)[:, None] * inv_freq[None, :]
    cos = jnp.cos(ang)[:, None, :]
    sin = jnp.sin(ang)[:, None, :]
    t32 = t.astype(jnp.float32)
    t1, t2 = t32[..., :half], t32[..., half:]
    return jnp.concatenate([t1 * cos - t2 * sin, t2 * cos + t1 * sin], axis=-1).astype(t.dtype)


def layer_norm(y, g, b):
    y32 = y.astype(jnp.float32)
    mu = jnp.mean(y32, axis=-1, keepdims=True)
    var = jnp.mean(jnp.square(y32 - mu), axis=-1, keepdims=True)
    out = (y32 - mu) * lax.rsqrt(var + EPS) * g.astype(jnp.float32) + b.astype(jnp.float32)
    return out.astype(y.dtype)


def branch_inputs(x, pos, w_in_l, b_if_l):
    b, l, _ = x.shape
    mq, mk, mv, mo, mi, mf, dq, dk, dv, cq, gate = split_in(x @ w_in_l)

    def to_bhld(t):
        return jnp.swapaxes(t.reshape(b, l, H_M, DH_M), 1, 2).astype(jnp.float32)

    q_m = to_bhld(mq)
    k_m = to_bhld(mk) * (DH_M ** -0.5)
    v_m = to_bhld(mv)
    ig = jnp.swapaxes((mi + b_if_l[:H_M]).astype(jnp.float32), 1, 2)
    lf = jax.nn.log_sigmoid(jnp.swapaxes((mf + b_if_l[H_M:]).astype(jnp.float32), 1, 2))
    dq = rope(dq.reshape(b, l, 2 * H_D, DQK_D), pos).reshape(b, l, H_D, 2, DQK_D)
    dk = rope(dk.reshape(b, l, 2 * H_D, DQK_D), pos).reshape(b, l, H_D, 2, DQK_D)
    dv = dv.reshape(b, l, H_D, DV_D)
    cq = cq.reshape(b, l, H_C, DH_C)
    return q_m, k_m, v_m, ig, lf, mo, dq, dk, dv, cq, gate


def mlstm_chunkwise(q, k, v, ig, lf, c0, n0, m0):
    b, h, l, _ = q.shape
    dv = v.shape[-1]
    c = CHUNK if l % CHUNK == 0 else l
    nc = l // c

    def blocks(t):
        return jnp.moveaxis(t.reshape((b, h, nc, c) + t.shape[3:]), 2, 0)

    tri = jnp.tril(jnp.ones((c, c), dtype=bool))

    def step(carry, xs):
        cm, nv, m = carry
        qc, kc, vc, ic, fc = xs
        bcum = jnp.cumsum(fc, axis=-1)
        dlog = jnp.where(tri, bcum[..., :, None] - bcum[..., None, :] + ic[..., None, :], -jnp.inf)
        inter = bcum + m[..., None]
        mt = jnp.maximum(jnp.max(dlog, axis=-1), inter)
        w_intra = jnp.exp(dlog - mt[..., None])
        w_inter = jnp.exp(inter - mt)
        s = jnp.einsum('bhtd,bhsd->bhts', qc, kc) * w_intra
        num = jnp.einsum('bhts,bhsv->bhtv', s, vc) + w_inter[..., None] * jnp.einsum('bhtd,bhdv->bhtv', qc, cm)
        den = jnp.sum(s, axis=-1) + w_inter * jnp.einsum('bhtd,bhd->bht', qc, nv)
        hc = num / jnp.maximum(jnp.abs(den), jnp.exp(-mt))[..., None]
        m_new = mt[..., -1]
        w_end = jnp.exp(bcum[..., -1:] - bcum + ic - m_new[..., None])
        decay = jnp.exp(bcum[..., -1] + m - m_new)
        c_new = decay[..., None, None] * cm + jnp.einsum('bhs,bhsd,bhsv->bhdv', w_end, kc, vc)
        n_new = decay[..., None] * nv + jnp.einsum('bhs,bhsd->bhd', w_end, kc)
        return (c_new, n_new, m_new), hc

    (c_f, n_f, m_f), hs = lax.scan(step, (c0, n0, m0), (blocks(q), blocks(k), blocks(v), blocks(ig), blocks(lf)))
    hs = jnp.moveaxis(hs, 0, 2).reshape(b, h, l, dv)
    return hs, c_f, n_f, m_f


def mlstm_readout(hm, o_pre, g):
    b, _, l, _ = hm.shape
    mu = jnp.mean(hm, axis=-1, keepdims=True)
    var = jnp.mean(jnp.square(hm - mu), axis=-1, keepdims=True)
    hn = (hm - mu) * lax.rsqrt(var + EPS) * g.astype(jnp.float32).reshape(H_M, 1, DH_M)
    hn = jnp.swapaxes(hn, 1, 2).reshape(b, l, D_M)
    return hn * jax.nn.sigmoid(o_pre.astype(jnp.float32))


def diff_attend(q, k, v, lam, q_pos, k_pos):
    s = jnp.einsum('bqhcd,bkhcd->bhcqk', q, k).astype(jnp.float32) * (DQK_D ** -0.5)
    mask = k_pos[None, :] <= q_pos[:, None]
    p = jax.nn.softmax(jnp.where(mask, s, -jnp.inf), axis=-1)
    a = p[:, :, 0] - lam * p[:, :, 1]
    return jnp.einsum('bhqk,bkhd->bqhd', a.astype(v.dtype), v)


def diff_attention_prompt(q, k, v, lam):
    b, l = q.shape[:2]
    qb = Q_BLOCK if l % Q_BLOCK == 0 else l
    nb = l // qb
    q_blocks = jnp.moveaxis(q.reshape(b, nb, qb, H_D, 2, DQK_D), 1, 0)
    pos_blocks = jnp.arange(l).reshape(nb, qb)
    k_pos = jnp.arange(l)
    ob = lax.map(lambda a: diff_attend(a[0], k, v, lam, a[1], k_pos), (q_blocks, pos_blocks))
    return jnp.moveaxis(ob, 0, 1).reshape(b, l, H_D, DV_D)


def diff_post(o, g, lam_init):
    o32 = o.astype(jnp.float32)
    o32 = o32 * lax.rsqrt(jnp.mean(o32 * o32, axis=-1, keepdims=True) + EPS) * g.astype(jnp.float32)
    return (o32 * (1.0 - lam_init)).reshape(o.shape[0], o.shape[1], D_D)


def memory_kv(mem, w):
    b, n, _ = mem.shape
    mk, mv = jnp.split(mem @ w, 2, axis=-1)
    return mk.reshape(b, n, H_C, DH_C), mv.reshape(b, n, H_C, DH_C)


def memory_attend(q, mk, mv):
    b, l = q.shape[:2]
    s = jnp.einsum('blhd,bnhd->bhln', q, mk).astype(jnp.float32) * (DH_C ** -0.5)
    p = jax.nn.softmax(s, axis=-1)
    return jnp.einsum('bhln,bnhd->blhd', p.astype(mv.dtype), mv).reshape(b, l, D_C)


def merge_output(x, h_m, h_d, h_c, gate, w_out_l, ln_g_l, ln_b_l):
    h = jnp.concatenate([h_m, h_d, h_c], axis=-1).astype(x.dtype) * jax.nn.silu(gate)
    return layer_norm(ALPHA * x + h @ w_out_l, ln_g_l, ln_b_l)


def setup_inputs(seed: int = 0) -> dict:
    key = jax.random.key(seed)
    ks = jax.random.split(key, 24)
    f32 = jnp.float32
    n_pages = PAST_LEN // PAGE_SIZE
    n_pool = (DEC_BATCH * n_pages * POOL_NUM) // POOL_DEN

    def nrm(k, shape, s=1.0):
        return s * jax.random.normal(k, shape, f32)

    x_prompt = nrm(ks[0], (BATCH, SEQ, D_MODEL))
    x_sample = nrm(ks[1], (DEC_BATCH, DEC_SEQ, D_MODEL))
    state_mlstm_c = nrm(ks[2], (DEPTH, DEC_BATCH, H_M, DH_M, DH_M), 0.1)
    state_mlstm_n = nrm(ks[3], (DEPTH, DEC_BATCH, H_M, DH_M), 0.1)
    state_mlstm_m = nrm(ks[4], (DEPTH, DEC_BATCH, H_M), 0.5)
    cache_diff_k = nrm(ks[5], (DEPTH, n_pool, PAGE_SIZE, H_D, 2 * DQK_D))
    cache_diff_v = nrm(ks[6], (DEPTH, n_pool, PAGE_SIZE, H_D, DV_D))
    cache_mem_k = nrm(ks[7], (DEPTH, DEC_BATCH, N_MEM, H_C, DH_C))
    cache_mem_v = nrm(ks[8], (DEPTH, DEC_BATCH, N_MEM, H_C, DH_C))
    page_table = jax.random.permutation(ks[9], n_pool)[: DEC_BATCH * n_pages].reshape(DEC_BATCH, n_pages).astype(jnp.int32)
    mem_prompt = nrm(ks[10], (BATCH, N_MEM, D_MODEL))
    w_in = nrm(ks[11], (DEPTH, D_MODEL, D_IN), D_MODEL ** -0.5)
    b_i = nrm(ks[12], (DEPTH, H_M), 0.1)
    b_f = jnp.linspace(3.0, 6.0, H_M, dtype=f32)[None, :] + nrm(ks[13], (DEPTH, H_M), 0.1)
    b_if = jnp.concatenate([b_i, b_f], axis=-1)
    w_mlstm_norm = 1.0 + nrm(ks[14], (DEPTH, D_M), 0.02)
    lambda_q1 = nrm(ks[15], (DEPTH, DQK_D), 0.1)
    lambda_k1 = nrm(ks[16], (DEPTH, DQK_D), 0.1)
    lambda_q2 = nrm(ks[17], (DEPTH, DQK_D), 0.1)
    lambda_k2 = nrm(ks[18], (DEPTH, DQK_D), 0.1)
    w_diff_norm = 1.0 + nrm(ks[19], (DEPTH, DV_D), 0.02)
    w_mem_kv = nrm(ks[20], (DEPTH, D_MODEL, 2 * D_C), D_MODEL ** -0.5)
    w_out = nrm(ks[21], (DEPTH, D_MIX, D_MODEL), BETA * D_MIX ** -0.5)
    ln_g = 1.0 + nrm(ks[22], (DEPTH, D_MODEL), 0.02)
    ln_b = nrm(ks[23], (DEPTH, D_MODEL), 0.02)
    return {
        "x_prompt": x_prompt, "x_sample": x_sample,
        "state_mlstm_c": state_mlstm_c, "state_mlstm_n": state_mlstm_n, "state_mlstm_m": state_mlstm_m,
        "cache_diff_k": cache_diff_k, "cache_diff_v": cache_diff_v,
        "cache_mem_k": cache_mem_k, "cache_mem_v": cache_mem_v,
        "page_table": page_table, "mem_prompt": mem_prompt,
        "w_in": w_in, "b_if": b_if, "w_mlstm_norm": w_mlstm_norm,
        "lambda_q1": lambda_q1, "lambda_k1": lambda_k1, "lambda_q2": lambda_q2, "lambda_k2": lambda_k2,
        "w_diff_norm": w_diff_norm, "w_mem_kv": w_mem_kv, "w_out": w_out,
        "ln_g": ln_g, "ln_b": ln_b,
    }


def reference(x_prompt, x_sample, state_mlstm_c, state_mlstm_n, state_mlstm_m, cache_diff_k, cache_diff_v,
              cache_mem_k, cache_mem_v, page_table, mem_prompt, w_in, b_if, w_mlstm_norm,
              lambda_q1, lambda_k1, lambda_q2, lambda_k2, w_diff_norm, w_mem_kv, w_out, ln_g, ln_b):
    f32 = jnp.float32
    bp, lp = x_prompt.shape[:2]
    bd, ls = x_sample.shape[:2]
    past_len = page_table.shape[1] * cache_diff_k.shape[2]
    pos_p = jnp.arange(lp)
    pos_s = past_len + jnp.arange(ls)
    k_pos_s = jnp.arange(past_len + ls)
    hp, hs = x_prompt, x_sample
    p_c, p_n, p_m, p_k, p_v, p_mk, p_mv = [], [], [], [], [], [], []
    s_c, s_n, s_m, s_k, s_v = [], [], [], [], []
    for l in range(DEPTH):
        lam_init = 0.8 - 0.6 * math.exp(-0.3 * l)
        lam = (jnp.exp(jnp.sum(lambda_q1[l].astype(f32) * lambda_k1[l].astype(f32)))
               - jnp.exp(jnp.sum(lambda_q2[l].astype(f32) * lambda_k2[l].astype(f32))) + lam_init)

        q_m, k_m, v_m, ig, lf, mo, dq, dk, dv, cq, gate = branch_inputs(hp, pos_p, w_in[l], b_if[l])
        c0 = jnp.zeros((bp, H_M, DH_M, DH_M), f32)
        n0 = jnp.zeros((bp, H_M, DH_M), f32)
        m0 = jnp.zeros((bp, H_M), f32)
        h_m, c_f, n_f, m_f = mlstm_chunkwise(q_m, k_m, v_m, ig, lf, c0, n0, m0)
        h_m = mlstm_readout(h_m, mo, w_mlstm_norm[l])
        h_d = diff_post(diff_attention_prompt(dq, dk, dv, lam), w_diff_norm[l], lam_init)
        mk, mv = memory_kv(mem_prompt, w_mem_kv[l])
        h_c = memory_attend(cq, mk, mv)
        hp_next = merge_output(hp, h_m, h_d, h_c, gate, w_out[l], ln_g[l], ln_b[l])
        p_c.append(c_f)
        p_n.append(n_f)
        p_m.append(m_f)
        p_k.append(dk.reshape(bp, lp, H_D, 2 * DQK_D))
        p_v.append(dv)
        p_mk.append(mk)
        p_mv.append(mv)

        q_m, k_m, v_m, ig, lf, mo, dq, dk, dv, cq, gate = branch_inputs(hs, pos_s, w_in[l], b_if[l])
        h_m, c_f, n_f, m_f = mlstm_chunkwise(q_m, k_m, v_m, ig, lf, state_mlstm_c[l].astype(f32),
                                             state_mlstm_n[l].astype(f32), state_mlstm_m[l].astype(f32))
        h_m = mlstm_readout(h_m, mo, w_mlstm_norm[l])
        past_k = cache_diff_k[l][page_table].reshape(bd, past_len, H_D, 2, DQK_D)
        past_v = cache_diff_v[l][page_table].reshape(bd, past_len, H_D, DV_D)
        k_all = jnp.concatenate([past_k, dk.astype(past_k.dtype)], axis=1)
        v_all = jnp.concatenate([past_v, dv.astype(past_v.dtype)], axis=1)
        h_d = diff_post(diff_attend(dq.astype(k_all.dtype), k_all, v_all, lam, pos_s, k_pos_s), w_diff_norm[l], lam_init)
        h_c = memory_attend(cq, cache_mem_k[l].astype(cq.dtype), cache_mem_v[l].astype(cq.dtype))
        hs_next = merge_output(hs, h_m, h_d, h_c, gate, w_out[l], ln_g[l], ln_b[l])
        s_c.append(c_f)
        s_n.append(n_f)
        s_m.append(m_f)
        s_k.append(dk.reshape(bd, ls, H_D, 2 * DQK_D))
        s_v.append(dv)

        hp, hs = hp_next, hs_next

    return (hp, hs,
            jnp.stack(p_c), jnp.stack(p_n), jnp.stack(p_m), jnp.stack(p_k), jnp.stack(p_v),
            jnp.stack(p_mk), jnp.stack(p_mv),
            jnp.stack(s_c), jnp.stack(s_n), jnp.stack(s_m), jnp.stack(s_k), jnp.stack(s_v))
```

```python
import functools
import math

import jax
import jax.numpy as jnp
from jax import lax
from jax.experimental import pallas as pl
from jax.experimental.pallas import tpu as pltpu

F32 = jnp.float32
BF16 = jnp.bfloat16

H_M = 4
H_D = 4
H_C = 4
N_GATE = 2 * H_M
CHUNK = 128
ROPE_THETA = 10000.0
EPS = 1e-5
LANES = 128
SUBLANES = 8
VMEM_LIMIT = 56 * 1024 * 1024

NT_DIMS = (((1,), (1,)), ((), ()))
TN_DIMS = (((0,), (0,)), ((), ()))


def _params(semantics):
    return pltpu.CompilerParams(dimension_semantics=semantics, vmem_limit_bytes=VMEM_LIMIT)


def _log_sigmoid(x):
    return jnp.minimum(x, 0.0) - jnp.log1p(jnp.exp(-jnp.abs(x)))


def _silu(x):
    return x * jax.nn.sigmoid(x)


def _lambda(lq1_ref, lk1_ref, lq2_ref, lk2_ref, lam_init):
    a = jnp.sum(lq1_ref[...] * lk1_ref[...], axis=1, keepdims=True)
    b = jnp.sum(lq2_ref[...] * lk2_ref[...], axis=1, keepdims=True)
    return jnp.exp(a) - jnp.exp(b) + lam_init


def _inproj_kernel(x_ref, wm_ref, wg_ref, wgt_ref, wd_ref, wgate_ref, bcol_ref, brow_ref,
                   cos_ref, sin_ref,
                   q_ref, k_ref, v_ref, o_ref, gcol_ref, grow_ref,
                   dq_ref, dkf_ref, dkb_ref, dvf_ref, dvb_ref, cq_ref, gate_ref,
                   *, d_m, d_d, dh_m, dqk):
    x = x_ref[...]

    def mm(w):
        return jnp.dot(x, w, preferred_element_type=F32)

    q_ref[...] = mm(wm_ref[:, 0:d_m]).astype(q_ref.dtype)
    k_ref[...] = (mm(wm_ref[:, d_m:2 * d_m]) * (dh_m ** -0.5)).astype(k_ref.dtype)
    v_ref[...] = mm(wm_ref[:, 2 * d_m:3 * d_m]).astype(v_ref.dtype)
    o_ref[...] = mm(wm_ref[:, 3 * d_m:4 * d_m])
    gcol_ref[...] = mm(wg_ref[...]) + bcol_ref[...]
    grow_ref[...] = (lax.dot_general(wgt_ref[...], x, NT_DIMS, preferred_element_type=F32)
                     + brow_ref[:, 0:1])

    cos = cos_ref[...]
    sin = sin_ref[...]
    lane = lax.broadcasted_iota(jnp.int32, cos.shape, 1)
    first_half = (lane % dqk) < (dqk // 2)

    def rope(t):
        swapped = jnp.where(first_half, pltpu.roll(t, LANES - dqk // 2, 1),
                            pltpu.roll(t, dqk // 2, 1))
        return t * cos + swapped * sin

    dq = mm(wd_ref[:, 0:d_d])
    dk = mm(wd_ref[:, d_d:2 * d_d])
    for h in range(d_d // LANES):
        sl = slice(h * LANES, (h + 1) * LANES)
        dq_ref[:, sl] = (rope(dq[:, sl]) * (dqk ** -0.5)).astype(dq_ref.dtype)
        dk_h = rope(dk[:, sl])
        dkf_ref[:, sl] = dk_h
        dkb_ref[:, sl] = dk_h.astype(dkb_ref.dtype)
    dv = mm(wd_ref[:, 2 * d_d:3 * d_d])
    dvf_ref[...] = dv
    dvb_ref[...] = dv.astype(dvb_ref.dtype)
    cq_ref[...] = mm(wd_ref[:, 3 * d_d:4 * d_d]).astype(cq_ref.dtype)
    gate_ref[...] = mm(wgate_ref[...])


def _input_projection(x2d, wts, cos_tab, sin_tab, tm, qkv_dtype):
    m, d_model = x2d.shape
    wm, wg, wgt, wd, wgate, bcol, brow = wts
    d_m = wm.shape[1] // 4
    d_d = wd.shape[1] // 4
    d_mix = wgate.shape[1]
    dh_m = d_m // H_M
    dqk = d_d // H_D // 2
    grid = (m // tm,)

    def row_spec(n):
        return pl.BlockSpec((tm, n), lambda i: (i, 0))

    def whole(a):
        return pl.BlockSpec(a.shape, lambda i: (0,) * a.ndim, pipeline_mode=pl.Buffered(1))

    out_shape = (
        jax.ShapeDtypeStruct((m, d_m), qkv_dtype),
        jax.ShapeDtypeStruct((m, d_m), qkv_dtype),
        jax.ShapeDtypeStruct((m, d_m), qkv_dtype),
        jax.ShapeDtypeStruct((m, d_m), F32),
        jax.ShapeDtypeStruct((m, LANES), F32),
        jax.ShapeDtypeStruct((N_GATE, m), F32),
        jax.ShapeDtypeStruct((m, d_d), BF16),
        jax.ShapeDtypeStruct((m, d_d), F32),
        jax.ShapeDtypeStruct((m, d_d), BF16),
        jax.ShapeDtypeStruct((m, d_d), F32),
        jax.ShapeDtypeStruct((m, d_d), BF16),
        jax.ShapeDtypeStruct((m, d_d), BF16),
        jax.ShapeDtypeStruct((m, d_mix), F32),
    )
    out_specs = (
        row_spec(d_m), row_spec(d_m), row_spec(d_m), row_spec(d_m), row_spec(LANES),
        pl.BlockSpec((N_GATE, tm), lambda i: (0, i)),
        row_spec(d_d), row_spec(d_d), row_spec(d_d), row_spec(d_d), row_spec(d_d), row_spec(d_d),
        row_spec(d_mix),
    )
    n_tab = cos_tab.shape[0] // tm
    tab_spec = pl.BlockSpec((tm, LANES), lambda i: (i % n_tab, 0))
    in_specs = [row_spec(d_model), whole(wm), whole(wg), whole(wgt), whole(wd), whole(wgate),
                whole(bcol), whole(brow), tab_spec, tab_spec]
    return pl.pallas_call(
        functools.partial(_inproj_kernel, d_m=d_m, d_d=d_d, dh_m=dh_m, dqk=dqk),
        grid=grid, in_specs=in_specs, out_specs=out_specs, out_shape=out_shape,
        compiler_params=_params(("parallel",)), name="input_projection",
    )(x2d, wm, wg, wgt, wd, wgate, bcol, brow, cos_tab, sin_tab)


def _mlstm_prompt_kernel(q_ref, k_ref, v_ref, o_ref, gate_ref, grow_ref, gcol_ref, g_ref,
                         h_ref, c_ref, n_ref, m_ref, *, dh):
    @pl.when(pl.program_id(1) == 0)
    def _():
        c_ref[...] = jnp.zeros_like(c_ref)
        n_ref[...] = jnp.zeros_like(n_ref)
        m_ref[...] = jnp.zeros_like(m_ref)

    c = q_ref.shape[0]
    t_idx = lax.broadcasted_iota(jnp.int32, (c, c), 0)
    s_idx = lax.broadcasted_iota(jnp.int32, (c, c), 1)
    causal = s_idx <= t_idx
    causal_t = t_idx <= s_idx

    for h in range(H_M):
        sl = slice(h * dh, (h + 1) * dh)
        q = q_ref[:, sl]
        k = k_ref[:, sl]
        v = v_ref[:, sl]
        ig_row = grow_ref[h:h + 1, :]
        lf_row = _log_sigmoid(grow_ref[H_M + h:H_M + h + 1, :])
        ig_col = gcol_ref[:, h:h + 1]
        lf_col = _log_sigmoid(gcol_ref[:, H_M + h:H_M + h + 1])
        bcum_col = jnp.sum(jnp.where(causal, lf_row, 0.0), axis=1, keepdims=True)
        bcum_row = jnp.sum(jnp.where(causal_t, lf_col, 0.0), axis=0, keepdims=True)
        m_prev = m_ref[:, h:h + 1]

        dlog = jnp.where(causal, bcum_col - bcum_row + ig_row, -jnp.inf)
        inter = bcum_col + m_prev
        mt = jnp.maximum(jnp.max(dlog, axis=1, keepdims=True), inter)
        w_intra = jnp.exp(dlog - mt)
        w_inter = jnp.exp(inter - mt)
        s = lax.dot_general(q, k, NT_DIMS, preferred_element_type=F32) * w_intra
        c_old = c_ref[h]
        n_old = n_ref[h:h + 1, :]
        num = (jnp.dot(s.astype(BF16), v, preferred_element_type=F32)
               + w_inter * jnp.dot(q, c_old.astype(BF16), preferred_element_type=F32))
        qn = jnp.sum(q.astype(F32) * n_old, axis=1, keepdims=True)
        den = jnp.sum(s, axis=1, keepdims=True) + w_inter * qn
        hc = num / jnp.maximum(jnp.abs(den), jnp.exp(-mt))

        m_new = mt[c - 1:c, :]
        b_last = bcum_col[c - 1:c, :]
        w_end = jnp.exp(b_last - bcum_col + ig_col - m_new)
        decay = jnp.exp(b_last + m_prev - m_new)
        kw = k.astype(F32) * w_end
        c_ref[h] = decay * c_old + lax.dot_general(kw.astype(BF16), v, TN_DIMS,
                                                   preferred_element_type=F32)
        n_ref[h:h + 1, :] = decay * n_old + jnp.sum(kw, axis=0, keepdims=True)
        m_ref[:, h:h + 1] = m_new

        mu = jnp.mean(hc, axis=1, keepdims=True)
        var = jnp.mean(jnp.square(hc - mu), axis=1, keepdims=True)
        hn = (hc - mu) * lax.rsqrt(var + EPS) * g_ref[:, sl]
        hn = hn * jax.nn.sigmoid(o_ref[:, sl])
        h_ref[:, sl] = (hn * _silu(gate_ref[:, sl])).astype(h_ref.dtype)


def _mlstm_prompt(q, k, v, o_pre, gate, grow, gcol, g_norm, b, l):
    d_m = q.shape[1]
    dh = d_m // H_M
    nc = l // CHUNK
    tok = pl.BlockSpec((CHUNK, d_m), lambda i, j: (i * nc + j, 0))
    in_specs = [tok, tok, tok, tok, tok,
                pl.BlockSpec((N_GATE, CHUNK), lambda i, j: (0, i * nc + j)),
                pl.BlockSpec((CHUNK, LANES), lambda i, j: (i * nc + j, 0)),
                pl.BlockSpec((1, d_m), lambda i, j: (0, 0))]
    out_shape = (jax.ShapeDtypeStruct((b * l, d_m), BF16),
                 jax.ShapeDtypeStruct((b, H_M, dh, dh), F32),
                 jax.ShapeDtypeStruct((b, H_M, dh), F32),
                 jax.ShapeDtypeStruct((b, 1, LANES), F32))
    out_specs = (tok,
                 pl.BlockSpec((None, H_M, dh, dh), lambda i, j: (i, 0, 0, 0)),
                 pl.BlockSpec((None, H_M, dh), lambda i, j: (i, 0, 0)),
                 pl.BlockSpec((None, 1, LANES), lambda i, j: (i, 0, 0)))
    return pl.pallas_call(
        functools.partial(_mlstm_prompt_kernel, dh=dh),
        grid=(b, nc), in_specs=in_specs, out_specs=out_specs, out_shape=out_shape,
        compiler_params=_params(("parallel", "arbitrary")), name="mlstm_prompt",
    )(q, k, v, o_pre, gate, grow, gcol, g_norm)


def _diff_prompt_kernel(lq1_ref, lk1_ref, lq2_ref, lk2_ref, q_ref, k_ref, v_ref, gate_ref, g_ref,
                        o_ref, *, lam_init, dqk):
    tq = q_ref.shape[0]
    lk = k_ref.shape[0]
    lam = _lambda(lq1_ref, lk1_ref, lq2_ref, lk2_ref, lam_init)
    q = q_ref[...]
    k = k_ref[...]
    lane = lax.broadcasted_iota(jnp.int32, q.shape, 1)
    zero = jnp.zeros_like(q)
    q_pos = pl.program_id(2) * tq + lax.broadcasted_iota(jnp.int32, (tq, lk), 0)
    k_pos = lax.broadcasted_iota(jnp.int32, (tq, lk), 1)
    visible = k_pos <= q_pos

    def softmax_map(qc):
        s = lax.dot_general(qc, k, NT_DIMS, preferred_element_type=F32)
        s = jnp.where(visible, s, -jnp.inf)
        p = jnp.exp(s - jnp.max(s, axis=1, keepdims=True))
        return p / jnp.sum(p, axis=1, keepdims=True)

    p1 = softmax_map(jnp.where(lane < dqk, q, zero))
    p2 = softmax_map(jnp.where(lane >= dqk, q, zero))
    a = (p1 - lam * p2).astype(BF16)
    o = jnp.dot(a, v_ref[...], preferred_element_type=F32)
    o = o * lax.rsqrt(jnp.mean(o * o, axis=1, keepdims=True) + EPS) * g_ref[...]
    o_ref[...] = (o * (1.0 - lam_init) * _silu(gate_ref[...])).astype(o_ref.dtype)


def _diff_prompt(lams, dq, dk, dv, gate, g_norm, b, l, tq, lam_init, gate_col0):
    d_d = dq.shape[1]
    dv_d = d_d // H_D
    nq = l // tq
    lam_spec = pl.BlockSpec(lams[0].shape, lambda i, h, j: (0, 0))
    in_specs = [lam_spec] * 4 + [
        pl.BlockSpec((tq, dv_d), lambda i, h, j: (i * nq + j, h)),
        pl.BlockSpec((l, dv_d), lambda i, h, j: (i, h)),
        pl.BlockSpec((l, dv_d), lambda i, h, j: (i, h)),
        pl.BlockSpec((tq, dv_d), lambda i, h, j: (i * nq + j, gate_col0 // dv_d + h)),
        pl.BlockSpec((1, dv_d), lambda i, h, j: (0, 0))]
    return pl.pallas_call(
        functools.partial(_diff_prompt_kernel, lam_init=lam_init, dqk=dv_d // 2),
        grid=(b, H_D, nq), in_specs=in_specs,
        out_specs=pl.BlockSpec((tq, dv_d), lambda i, h, j: (i * nq + j, h)),
        out_shape=jax.ShapeDtypeStruct((b * l, d_d), BF16),
        compiler_params=_params(("parallel", "parallel", "parallel")), name="diff_attn_prompt",
    )(*lams, dq, dk, dv, gate, g_norm)


def _mem_kv_kernel(x_ref, w_ref, kf_ref, kb_ref, vf_ref, vb_ref):
    d_c = kf_ref.shape[1]
    x = x_ref[...]
    mk = jnp.dot(x, w_ref[:, 0:d_c], preferred_element_type=F32)
    mv = jnp.dot(x, w_ref[:, d_c:2 * d_c], preferred_element_type=F32)
    kf_ref[...] = mk
    kb_ref[...] = mk.astype(kb_ref.dtype)
    vf_ref[...] = mv
    vb_ref[...] = mv.astype(vb_ref.dtype)


def _mem_kv(mem2d, w, tm):
    m, d_model = mem2d.shape
    d_c = w.shape[1] // 2
    row = pl.BlockSpec((tm, d_c), lambda i: (i, 0))
    return pl.pallas_call(
        _mem_kv_kernel, grid=(m // tm,),
        in_specs=[pl.BlockSpec((tm, d_model), lambda i: (i, 0)),
                  pl.BlockSpec(w.shape, lambda i: (0, 0))],
        out_specs=(row, row, row, row),
        out_shape=(jax.ShapeDtypeStruct((m, d_c), F32), jax.ShapeDtypeStruct((m, d_c), BF16),
                   jax.ShapeDtypeStruct((m, d_c), F32), jax.ShapeDtypeStruct((m, d_c), BF16)),
        compiler_params=_params(("parallel",)), name="memory_kv",
    )(mem2d, w)


def _mem_prompt_kernel(q_ref, k_ref, v_ref, gate_ref, o_ref, *, dh):
    for h in range(H_C):
        sl = slice(h * dh, (h + 1) * dh)
        s = lax.dot_general(q_ref[:, sl], k_ref[:, sl], NT_DIMS,
                            preferred_element_type=F32) * (dh ** -0.5)
        p = jnp.exp(s - jnp.max(s, axis=1, keepdims=True))
        p = p / jnp.sum(p, axis=1, keepdims=True)
        o = jnp.dot(p.astype(BF16), v_ref[:, sl], preferred_element_type=F32)
        o_ref[:, sl] = (o * _silu(gate_ref[:, sl])).astype(o_ref.dtype)


def _mem_prompt(cq, mk, mv, gate, b, l, n_mem, tl, gate_col0):
    d_c = cq.shape[1]
    nl = l // tl
    tok = pl.BlockSpec((tl, d_c), lambda i, j: (i * nl + j, 0))
    mem = pl.BlockSpec((n_mem, d_c), lambda i, j: (i, 0))
    return pl.pallas_call(
        functools.partial(_mem_prompt_kernel, dh=d_c // H_C),
        grid=(b, nl),
        in_specs=[tok, mem, mem,
                  pl.BlockSpec((tl, d_c), lambda i, j: (i * nl + j, gate_col0 // d_c))],
        out_specs=tok, out_shape=jax.ShapeDtypeStruct((b * l, d_c), BF16),
        compiler_params=_params(("parallel", "parallel")), name="memory_attn_prompt",
    )(cq, mk, mv, gate)


def _merge_kernel(x_ref, hm_ref, hd_ref, hc_ref, w_ref, g_ref, b_ref, y_ref, *, alpha):
    d_m = hm_ref.shape[1]
    d_d = hd_ref.shape[1]
    acc = jnp.dot(hm_ref[...], w_ref[0:d_m, :], preferred_element_type=F32)
    acc += jnp.dot(hd_ref[...], w_ref[d_m:d_m + d_d, :], preferred_element_type=F32)
    acc += jnp.dot(hc_ref[...], w_ref[d_m + d_d:, :], preferred_element_type=F32)
    y = alpha * x_ref[...] + acc
    mu = jnp.mean(y, axis=1, keepdims=True)
    var = jnp.mean(jnp.square(y - mu), axis=1, keepdims=True)
    y_ref[...] = (y - mu) * lax.rsqrt(var + EPS) * g_ref[...] + b_ref[...]


def _merge(x2d, h_m, h_d, h_c, w_out, ln_g, ln_b, tm, alpha):
    m, d_model = x2d.shape

    def row(n):
        return pl.BlockSpec((tm, n), lambda i: (i, 0))

    def whole(a):
        return pl.BlockSpec(a.shape, lambda i: (0, 0))

    return pl.pallas_call(
        functools.partial(_merge_kernel, alpha=alpha), grid=(m // tm,),
        in_specs=[row(d_model), row(h_m.shape[1]), row(h_d.shape[1]), row(h_c.shape[1]),
                  whole(w_out), whole(ln_g), whole(ln_b)],
        out_specs=row(d_model), out_shape=jax.ShapeDtypeStruct((m, d_model), F32),
        compiler_params=_params(("parallel",)), name="output_projection",
    )(x2d, h_m, h_d, h_c, w_out, ln_g, ln_b)


def _mlstm_step_kernel(q_ref, k_ref, v_ref, o_ref, gate_ref, ig_ref, fg_ref, m_ref, n_ref, c_ref,
                       g_ref, h_ref, cn_ref, nn_ref, mn_ref, *, dh):
    tb = q_ref.shape[0]
    r_idx = lax.broadcasted_iota(jnp.int32, (dh, dh), 0)
    c_idx = lax.broadcasted_iota(jnp.int32, (dh, dh), 1)
    eye = r_idx == c_idx

    def column(row):
        return jnp.sum(jnp.where(eye, row, 0.0), axis=1, keepdims=True)

    for b in range(tb):
        for h in range(H_M):
            sl = slice(h * dh, (h + 1) * dh)
            q = q_ref[b, :, sl]
            k = k_ref[b, :, sl]
            v = v_ref[b, :, sl]
            ig = ig_ref[b, h]
            lf = _log_sigmoid(fg_ref[b, h])
            m_prev = m_ref[b, h]
            n_old = n_ref[b, h]
            c_old = c_ref[b, h]

            inter = lf + m_prev
            mt = jnp.maximum(ig, inter)
            w_in = jnp.exp(ig - mt)
            w_st = jnp.exp(inter - mt)
            s = jnp.sum(q * k, axis=1, keepdims=True) * w_in
            qc = jnp.sum(column(q) * c_old, axis=0, keepdims=True)
            num = s * v + w_st * qc
            den = s + w_st * jnp.sum(q * n_old, axis=1, keepdims=True)
            hc = num / jnp.maximum(jnp.abs(den), jnp.exp(-mt))

            cn_ref[b, h] = w_st * c_old + (w_in * column(k)) * v
            nn_ref[b, h] = w_st * n_old + w_in * k
            mn_ref[b, h] = mt

            mu = jnp.mean(hc, axis=1, keepdims=True)
            var = jnp.mean(jnp.square(hc - mu), axis=1, keepdims=True)
            hn = (hc - mu) * lax.rsqrt(var + EPS) * g_ref[:, sl]
            hn = hn * jax.nn.sigmoid(o_ref[b, :, sl])
            h_ref[b, :, sl] = (hn * _silu(gate_ref[b, :, sl])).astype(h_ref.dtype)


def _mlstm_step(q, k, v, o_pre, gate, ig, fg, m0, n0, c0, g_norm, tb):
    bd, _, d_m = q.shape
    dh = d_m // H_M

    def blk(shape):
        nd = len(shape)
        return pl.BlockSpec((tb,) + tuple(shape), lambda i: (i,) + (0,) * nd)

    tok = blk((1, d_m))
    one = blk((H_M, 1, 1))
    in_specs = [tok, tok, tok, tok, tok, one, one, one, blk((H_M, 1, dh)), blk((H_M, dh, dh)),
                pl.BlockSpec((1, d_m), lambda i: (0, 0))]
    out_shape = (jax.ShapeDtypeStruct((bd, 1, d_m), BF16),
                 jax.ShapeDtypeStruct((bd, H_M, dh, dh), F32),
                 jax.ShapeDtypeStruct((bd, H_M, 1, dh), F32),
                 jax.ShapeDtypeStruct((bd, H_M, 1, 1), F32))
    out_specs = (tok, blk((H_M, dh, dh)), blk((H_M, 1, dh)), one)
    return pl.pallas_call(
        functools.partial(_mlstm_step_kernel, dh=dh), grid=(bd // tb,),
        in_specs=in_specs, out_specs=out_specs, out_shape=out_shape,
        compiler_params=_params(("parallel",)), name="mlstm_step",
    )(q, k, v, o_pre, gate, ig, fg, m0, n0, c0, g_norm)


def _diff_decode_kernel(pt_ref, lq1_ref, lk1_ref, lq2_ref, lk2_ref, q_ref, kn_ref, vn_ref,
                        gate_ref, g_ref, *rest, n_pages, lam_init, dv_d):
    del pt_ref
    k_refs = rest[:n_pages]
    v_refs = rest[n_pages:2 * n_pages]
    o_ref = rest[2 * n_pages]
    d_d = q_ref.shape[2]
    dqk = dv_d // 2
    lam = _lambda(lq1_ref, lk1_ref, lq2_ref, lk2_ref, lam_init)

    rows = lax.broadcasted_iota(jnp.int32, (SUBLANES, d_d), 0)
    cols = lax.broadcasted_iota(jnp.int32, (SUBLANES, d_d), 1)
    own = (cols // dqk) == ((rows % H_D) * 2 + rows // H_D)
    q8 = jnp.where(own, q_ref[0].astype(F32), 0.0)
    q8b = q8.astype(BF16)

    scores = [lax.dot_general(q8b, k_refs[j][...].astype(BF16), NT_DIMS,
                              preferred_element_type=F32) for j in range(n_pages)]
    s_new = jnp.sum(q8 * kn_ref[0], axis=1, keepdims=True)
    mx = s_new
    for s in scores:
        mx = jnp.maximum(mx, jnp.max(s, axis=1, keepdims=True))
    probs = [jnp.exp(s - mx) for s in scores]
    p_new = jnp.exp(s_new - mx)
    total = p_new
    for p in probs:
        total = total + jnp.sum(p, axis=1, keepdims=True)
    inv = 1.0 / total

    def combine(p):
        pn = p * inv
        return pn - lam * pltpu.roll(pn, SUBLANES - H_D, 0)

    acc = combine(jnp.broadcast_to(p_new, (SUBLANES, LANES)))[:, 0:1] * vn_ref[0]
    for j in range(n_pages):
        acc = acc + jnp.dot(combine(probs[j]).astype(BF16), v_refs[j][...].astype(BF16),
                            preferred_element_type=F32)
    for h in range(H_D):
        sl = slice(h * dv_d, (h + 1) * dv_d)
        o = acc[h:h + 1, sl]
        o = o * lax.rsqrt(jnp.mean(o * o, axis=1, keepdims=True) + EPS) * g_ref[...]
        o_ref[0, :, sl] = (o * (1.0 - lam_init) * _silu(gate_ref[0, :, sl])).astype(o_ref.dtype)


def _diff_decode(page_table, lams, dq, dk_new, dv_new, gate, g_norm, cache_k, cache_v,
                 lam_init, gate_col0):
    bd, _, d_d = dq.shape
    n_pages = page_table.shape[1]
    page = cache_k.shape[1]
    dv_d = d_d // H_D
    lam_spec = pl.BlockSpec(lams[0].shape, lambda i, pt: (0, 0))
    tok = pl.BlockSpec((1, 1, d_d), lambda i, pt: (i, 0, 0))
    page_specs = [pl.BlockSpec((None, page, d_d), lambda i, pt, j=j: (pt[i, j], 0, 0))
                  for j in range(n_pages)]
    in_specs = ([lam_spec] * 4 + [tok, tok, tok,
                                  pl.BlockSpec((1, 1, d_d), lambda i, pt: (i, 0, gate_col0 // d_d)),
                                  pl.BlockSpec((1, dv_d), lambda i, pt: (0, 0))]
                + page_specs + page_specs)
    grid_spec = pltpu.PrefetchScalarGridSpec(
        num_scalar_prefetch=1, grid=(bd,), in_specs=in_specs, out_specs=tok)
    return pl.pallas_call(
        functools.partial(_diff_decode_kernel, n_pages=n_pages, lam_init=lam_init, dv_d=dv_d),
        grid_spec=grid_spec, out_shape=jax.ShapeDtypeStruct((bd, 1, d_d), BF16),
        compiler_params=_params(("parallel",)), name="diff_attn_decode",
    )(page_table, *lams, dq, dk_new, dv_new, gate, g_norm,
      *([cache_k] * n_pages), *([cache_v] * n_pages))


def _mem_decode_kernel(q_ref, k_ref, v_ref, gate_ref, o_ref, *, dh):
    d_c = q_ref.shape[2]
    rows = lax.broadcasted_iota(jnp.int32, (SUBLANES, d_c), 0)
    cols = lax.broadcasted_iota(jnp.int32, (SUBLANES, d_c), 1)
    own = (cols // dh) == (rows % H_C)
    q8 = jnp.where(own, q_ref[0].astype(F32), 0.0).astype(BF16)
    s = lax.dot_general(q8, k_ref[0].astype(BF16), NT_DIMS,
                        preferred_element_type=F32) * (dh ** -0.5)
    p = jnp.exp(s - jnp.max(s, axis=1, keepdims=True))
    p = p / jnp.sum(p, axis=1, keepdims=True)
    acc = jnp.dot(p.astype(BF16), v_ref[0].astype(BF16), preferred_element_type=F32)
    for h in range(H_C):
        sl = slice(h * dh, (h + 1) * dh)
        o_ref[0, :, sl] = (acc[h:h + 1, sl] * _silu(gate_ref[0, :, sl])).astype(o_ref.dtype)


def _mem_decode(cq, mem_k, mem_v, gate, gate_col0):
    bd, _, d_c = cq.shape
    n_mem = mem_k.shape[1]
    tok = pl.BlockSpec((1, 1, d_c), lambda i: (i, 0, 0))
    mem = pl.BlockSpec((1, n_mem, d_c), lambda i: (i, 0, 0))
    return pl.pallas_call(
        functools.partial(_mem_decode_kernel, dh=d_c // H_C), grid=(bd,),
        in_specs=[tok, mem, mem, pl.BlockSpec((1, 1, d_c), lambda i: (i, 0, gate_col0 // d_c))],
        out_specs=tok, out_shape=jax.ShapeDtypeStruct((bd, 1, d_c), BF16),
        compiler_params=_params(("parallel",)), name="memory_attn_decode",
    )(cq, mem_k, mem_v, gate)


def _rope_tables(pos, n_rows, dqk):
    half = dqk // 2
    inv_freq = ROPE_THETA ** (-jnp.arange(half, dtype=F32) / half)
    ang = pos.astype(F32)[:, None] * inv_freq[None, :]
    cos = jnp.cos(ang)
    sin = jnp.sin(ang)
    reps = LANES // dqk
    cos_t = jnp.tile(jnp.concatenate([cos, cos], axis=1), (1, reps))
    sin_t = jnp.tile(jnp.concatenate([-sin, sin], axis=1), (1, reps))
    return (jnp.broadcast_to(cos_t, (n_rows, LANES)), jnp.broadcast_to(sin_t, (n_rows, LANES)))


def _split_weights(w_in_l, b_if_l, d_m, d_d, d_mix):
    g0 = 4 * d_m
    d0 = g0 + N_GATE
    wm = w_in_l[:, :g0].astype(BF16)
    w_gates = w_in_l[:, g0:d0]
    wg = jnp.pad(w_gates, ((0, 0), (0, LANES - N_GATE))).astype(BF16)
    wgt = w_gates.T.astype(BF16)
    wd = w_in_l[:, d0:d0 + 4 * d_d].astype(BF16)
    wgate = w_in_l[:, d0 + 4 * d_d:].astype(BF16)
    bcol = jnp.pad(b_if_l, (0, LANES - N_GATE)).reshape(1, LANES).astype(F32)
    brow = jnp.broadcast_to(b_if_l.astype(F32)[:, None], (N_GATE, LANES))
    return wm, wg, wgt, wd, wgate, bcol, brow


def kernel(x_prompt, x_sample, state_mlstm_c, state_mlstm_n, state_mlstm_m, cache_diff_k, cache_diff_v,
           cache_mem_k, cache_mem_v, page_table, mem_prompt, w_in, b_if, w_mlstm_norm,
           lambda_q1, lambda_k1, lambda_q2, lambda_k2, w_diff_norm, w_mem_kv, w_out, ln_g, ln_b):
    depth = w_in.shape[0]
    assert depth == 1, "single-layer trunk"
    bp, lp, d_model = x_prompt.shape
    bd, ls, _ = x_sample.shape
    assert ls == 1
    d_mix = w_out.shape[1]
    d_m = d_mix // 2
    d_d = d_mix // 4
    d_c = d_mix // 4
    dh_m = d_m // H_M
    dv_d = d_d // H_D
    dqk = dv_d // 2
    n_mem = mem_prompt.shape[1]
    n_pool, page = cache_diff_k.shape[1], cache_diff_k.shape[2]
    past_len = page_table.shape[1] * page
    alpha = (2 * depth) ** 0.25
    layer = 0
    lam_init = 0.8 - 0.6 * math.exp(-0.3 * layer)

    wts = _split_weights(w_in[layer], b_if[layer], d_m, d_d, d_mix)
    lams = tuple(a[layer].reshape(1, dqk).astype(F32)
                 for a in (lambda_q1, lambda_k1, lambda_q2, lambda_k2))
    g_m = w_mlstm_norm[layer].reshape(1, d_m).astype(F32)
    g_d = w_diff_norm[layer].reshape(1, dv_d).astype(F32)
    w_out_b = w_out[layer].astype(BF16)
    g_ln = ln_g[layer].reshape(1, d_model).astype(F32)
    b_ln = ln_b[layer].reshape(1, d_model).astype(F32)
    gate_d0 = d_m
    gate_c0 = d_m + d_d

    mp = bp * lp
    xp2d = x_prompt.reshape(mp, d_model)
    cos_p, sin_p = _rope_tables(jnp.arange(lp), lp, dqk)
    (q_m, k_m, v_m, o_pre, gcol, grow, dq, dk_f, dk_b, dv_f, dv_b, cq, gate) = _input_projection(
        xp2d.astype(BF16), wts, cos_p, sin_p, tm=256, qkv_dtype=BF16)
    h_m, c_p, n_p, m_p = _mlstm_prompt(q_m, k_m, v_m, o_pre, gate, grow, gcol, g_m, bp, lp)
    h_d = _diff_prompt(lams, dq, dk_b, dv_b, gate, g_d, bp, lp, 256, lam_init, gate_d0)
    mk_f, mk_b, mv_f, mv_b = _mem_kv(mem_prompt.reshape(bp * n_mem, d_model).astype(BF16),
                                     w_mem_kv[layer].astype(BF16), tm=512)
    h_c = _mem_prompt(cq, mk_b, mv_b, gate, bp, lp, n_mem, 512, gate_c0)
    y_p = _merge(xp2d, h_m, h_d, h_c, w_out_b, g_ln, b_ln, 512, alpha)

    xs2d = x_sample.reshape(bd, d_model)
    cos_s, sin_s = _rope_tables(jnp.full((1,), past_len), bd, dqk)
    (qs, ks, vs, os_pre, gcol_s, _, dq_s, dk_s, _, dv_s, _, cq_s, gate_s) = _input_projection(
        xs2d.astype(BF16), wts, cos_s, sin_s, tm=bd, qkv_dtype=F32)

    def tok3(a):
        return a.reshape(bd, 1, a.shape[1])

    ig_s = gcol_s[:, 0:H_M].reshape(bd, H_M, 1, 1)
    fg_s = gcol_s[:, H_M:N_GATE].reshape(bd, H_M, 1, 1)
    hs_m, c_s, n_s, m_s = _mlstm_step(
        tok3(qs), tok3(ks), tok3(vs), tok3(os_pre), tok3(gate_s), ig_s, fg_s,
        state_mlstm_m[layer].astype(F32).reshape(bd, H_M, 1, 1),
        state_mlstm_n[layer].astype(F32).reshape(bd, H_M, 1, dh_m),
        state_mlstm_c[layer].astype(F32), g_m, tb=4)
    hs_d = _diff_decode(page_table, lams, tok3(dq_s), tok3(dk_s), tok3(dv_s), tok3(gate_s), g_d,
                        cache_diff_k[layer].reshape(n_pool, page, d_d),
                        cache_diff_v[layer].reshape(n_pool, page, d_d), lam_init, gate_d0)
    hs_c = _mem_decode(tok3(cq_s), cache_mem_k[layer].reshape(bd, n_mem, d_c),
                       cache_mem_v[layer].reshape(bd, n_mem, d_c), tok3(gate_s), gate_c0)
    y_s = _merge(xs2d, hs_m.reshape(bd, d_m), hs_d.reshape(bd, d_d), hs_c.reshape(bd, d_c),
                 w_out_b, g_ln, b_ln, bd, alpha)

    return (y_p.reshape(bp, lp, d_model), y_s.reshape(bd, ls, d_model),
            c_p[None], n_p[None], m_p[:, 0, 0:H_M][None],
            dk_f.reshape(1, bp, lp, H_D, dv_d), dv_f.reshape(1, bp, lp, H_D, dv_d),
            mk_f.reshape(1, bp, n_mem, H_C, d_c // H_C), mv_f.reshape(1, bp, n_mem, H_C, d_c // H_C),
            c_s[None], n_s.reshape(1, bd, H_M, dh_m), m_s.reshape(1, bd, H_M),
            dk_s.reshape(1, bd, ls, H_D, dv_d), dv_s.reshape(1, bd, ls, H_D, dv_d))
```

```python
import functools
import math

import jax
import jax.numpy as jnp
from jax import lax
from jax.experimental import pallas as pl
from jax.experimental.pallas import tpu as pltpu

F32 = jnp.float32
BF16 = jnp.bfloat16

H_M = 4
H_D = 4
H_C = 4
N_GATE = 2 * H_M
CHUNK = 128
ROPE_THETA = 10000.0
EPS = 1e-5
LANES = 128
SUBLANES = 8
VMEM_LIMIT = 56 * 1024 * 1024

NT_DIMS = (((1,), (1,)), ((), ()))
TN_DIMS = (((0,), (0,)), ((), ()))


def _params(semantics):
    return pltpu.CompilerParams(dimension_semantics=semantics, vmem_limit_bytes=VMEM_LIMIT)


def _log_sigmoid(x):
    return jnp.minimum(x, 0.0) - jnp.log1p(jnp.exp(-jnp.abs(x)))


def _silu(x):
    return x * jax.nn.sigmoid(x)


def _lambda(lq1_ref, lk1_ref, lq2_ref, lk2_ref, lam_init):
    a = jnp.sum(lq1_ref[...] * lk1_ref[...], axis=1, keepdims=True)
    b = jnp.sum(lq2_ref[...] * lk2_ref[...], axis=1, keepdims=True)
    return jnp.exp(a) - jnp.exp(b) + lam_init


def _inproj_kernel(x_ref, wm_ref, wg_ref, wgt_ref, wd_ref, wgate_ref, bcol_ref, brow_ref,
                   cos_ref, sin_ref,
                   q_ref, k_ref, v_ref, o_ref, gcol_ref, grow_ref,
                   dq_ref, dkf_ref, dkb_ref, dvf_ref, dvt_ref, cq_ref, gate_ref,
                   *, d_m, d_d, dh_m, dqk):
    x = x_ref[...].astype(BF16)

    def mm(w):
        return jnp.dot(x, w, preferred_element_type=F32)

    q_ref[...] = mm(wm_ref[:, 0:d_m]).astype(q_ref.dtype)
    k_ref[...] = (mm(wm_ref[:, d_m:2 * d_m]) * (dh_m ** -0.5)).astype(k_ref.dtype)
    v_ref[...] = mm(wm_ref[:, 2 * d_m:3 * d_m]).astype(v_ref.dtype)
    o_ref[...] = mm(wm_ref[:, 3 * d_m:4 * d_m])
    gcol_ref[...] = mm(wg_ref[...]) + bcol_ref[...]
    grow_ref[...] = (lax.dot_general(wgt_ref[...], x, NT_DIMS, preferred_element_type=F32)
                     + brow_ref[:, 0:1])

    cos = cos_ref[...]
    sin = sin_ref[...]
    lane = lax.broadcasted_iota(jnp.int32, cos.shape, 1)
    first_half = (lane % dqk) < (dqk // 2)

    def rope(t):
        swapped = jnp.where(first_half, pltpu.roll(t, LANES - dqk // 2, 1),
                            pltpu.roll(t, dqk // 2, 1))
        return t * cos + swapped * sin

    tm = x.shape[0]
    dq = mm(wd_ref[:, 0:d_d])
    dk = mm(wd_ref[:, d_d:2 * d_d])
    dv = mm(wd_ref[:, 2 * d_d:3 * d_d])
    dvt_ref[0] = dv.T.astype(dvt_ref.dtype)
    for h in range(H_D):
        sl = slice(h * LANES, (h + 1) * LANES)
        dq_ref[:, sl] = (rope(dq[:, sl]) * (dqk ** -0.5)).astype(dq_ref.dtype)
        dk_h = rope(dk[:, sl])
        dkb_ref[:, sl] = dk_h.astype(dkb_ref.dtype)
        dkf_ref[pl.ds(h, tm, stride=H_D), :] = dk_h
        dvf_ref[pl.ds(h, tm, stride=H_D), :] = dv[:, sl]
    cq_ref[...] = mm(wd_ref[:, 3 * d_d:4 * d_d]).astype(cq_ref.dtype)
    gate_ref[...] = mm(wgate_ref[...])


def _input_projection(x2d, wts, cos_tab, sin_tab, tm, qkv_dtype):
    m, d_model = x2d.shape
    wm, wg, wgt, wd, wgate, bcol, brow = wts
    d_m = wm.shape[1] // 4
    d_d = wd.shape[1] // 4
    d_mix = wgate.shape[1]
    dh_m = d_m // H_M
    dqk = d_d // H_D // 2
    grid = (m // tm,)

    def row_spec(n):
        return pl.BlockSpec((tm, n), lambda i: (i, 0))

    def whole(a):
        return pl.BlockSpec(a.shape, lambda i: (0,) * a.ndim, pipeline_mode=pl.Buffered(1))

    out_shape = (
        jax.ShapeDtypeStruct((m, d_m), qkv_dtype),
        jax.ShapeDtypeStruct((m, d_m), qkv_dtype),
        jax.ShapeDtypeStruct((m, d_m), qkv_dtype),
        jax.ShapeDtypeStruct((m, d_m), F32),
        jax.ShapeDtypeStruct((m, LANES), F32),
        jax.ShapeDtypeStruct((N_GATE, m), F32),
        jax.ShapeDtypeStruct((m, d_d), BF16),
        jax.ShapeDtypeStruct((m * H_D, LANES), F32),
        jax.ShapeDtypeStruct((m, d_d), BF16),
        jax.ShapeDtypeStruct((m * H_D, LANES), F32),
        jax.ShapeDtypeStruct((m // tm, d_d, tm), BF16),
        jax.ShapeDtypeStruct((m, d_d), BF16),
        jax.ShapeDtypeStruct((m, d_mix), F32),
    )
    head_rows = pl.BlockSpec((tm * H_D, LANES), lambda i: (i, 0))
    out_specs = (
        row_spec(d_m), row_spec(d_m), row_spec(d_m), row_spec(d_m), row_spec(LANES),
        pl.BlockSpec((N_GATE, tm), lambda i: (0, i)),
        row_spec(d_d), head_rows, row_spec(d_d), head_rows,
        pl.BlockSpec((1, d_d, tm), lambda i: (i, 0, 0)), row_spec(d_d),
        row_spec(d_mix),
    )
    n_tab = cos_tab.shape[0] // tm
    tab_spec = pl.BlockSpec((tm, LANES), lambda i: (i % n_tab, 0))
    in_specs = [row_spec(d_model), whole(wm), whole(wg), whole(wgt), whole(wd), whole(wgate),
                whole(bcol), whole(brow), tab_spec, tab_spec]
    return pl.pallas_call(
        functools.partial(_inproj_kernel, d_m=d_m, d_d=d_d, dh_m=dh_m, dqk=dqk),
        grid=grid, in_specs=in_specs, out_specs=out_specs, out_shape=out_shape,
        compiler_params=_params(("parallel",)), name="input_projection",
    )(x2d, wm, wg, wgt, wd, wgate, bcol, brow, cos_tab, sin_tab)


def _mlstm_prompt_kernel(q_ref, k_ref, v_ref, o_ref, gate_ref, grow_ref, gcol_ref, g_ref,
                         h_ref, c_ref, n_ref, m_ref, *, dh):
    @pl.when(pl.program_id(1) == 0)
    def _():
        c_ref[...] = jnp.zeros_like(c_ref)
        n_ref[...] = jnp.zeros_like(n_ref)
        m_ref[...] = jnp.zeros_like(m_ref)

    c = q_ref.shape[0]
    t_idx = lax.broadcasted_iota(jnp.int32, (c, c), 0)
    s_idx = lax.broadcasted_iota(jnp.int32, (c, c), 1)
    causal = s_idx <= t_idx
    causal_t = t_idx <= s_idx

    for h in range(H_M):
        sl = slice(h * dh, (h + 1) * dh)
        q = q_ref[:, sl]
        k = k_ref[:, sl]
        v = v_ref[:, sl]
        ig_row = grow_ref[h:h + 1, :]
        lf_row = _log_sigmoid(grow_ref[H_M + h:H_M + h + 1, :])
        ig_col = gcol_ref[:, h:h + 1]
        lf_col = _log_sigmoid(gcol_ref[:, H_M + h:H_M + h + 1])
        bcum_col = jnp.sum(jnp.where(causal, lf_row, 0.0), axis=1, keepdims=True)
        bcum_row = jnp.sum(jnp.where(causal_t, lf_col, 0.0), axis=0, keepdims=True)
        m_prev = m_ref[:, h:h + 1]

        dlog = jnp.where(causal, bcum_col - bcum_row + ig_row, -jnp.inf)
        inter = bcum_col + m_prev
        mt = jnp.maximum(jnp.max(dlog, axis=1, keepdims=True), inter)
        w_intra = jnp.exp(dlog - mt)
        w_inter = jnp.exp(inter - mt)
        s = lax.dot_general(q, k, NT_DIMS, preferred_element_type=F32) * w_intra
        c_old = c_ref[h]
        n_old = n_ref[h:h + 1, :]
        num = (jnp.dot(s.astype(BF16), v, preferred_element_type=F32)
               + w_inter * jnp.dot(q, c_old.astype(BF16), preferred_element_type=F32))
        qn = jnp.sum(q.astype(F32) * n_old, axis=1, keepdims=True)
        den = jnp.sum(s, axis=1, keepdims=True) + w_inter * qn
        hc = num / jnp.maximum(jnp.abs(den), jnp.exp(-mt))

        m_new = mt[c - 1:c, :]
        b_last = bcum_col[c - 1:c, :]
        w_end = jnp.exp(b_last - bcum_col + ig_col - m_new)
        decay = jnp.exp(b_last + m_prev - m_new)
        kw = k.astype(F32) * w_end
        c_ref[h] = decay * c_old + lax.dot_general(kw.astype(BF16), v, TN_DIMS,
                                                   preferred_element_type=F32)
        n_ref[h:h + 1, :] = decay * n_old + jnp.sum(kw, axis=0, keepdims=True)
        m_ref[:, h:h + 1] = m_new

        mu = jnp.mean(hc, axis=1, keepdims=True)
        var = jnp.mean(jnp.square(hc - mu), axis=1, keepdims=True)
        hn = (hc - mu) * lax.rsqrt(var + EPS) * g_ref[:, sl]
        hn = hn * jax.nn.sigmoid(o_ref[:, sl])
        h_ref[:, sl] = (hn * _silu(gate_ref[:, sl])).astype(h_ref.dtype)


def _mlstm_prompt(q, k, v, o_pre, gate, grow, gcol, g_norm, b, l):
    d_m = q.shape[1]
    dh = d_m // H_M
    nc = l // CHUNK
    tok = pl.BlockSpec((CHUNK, d_m), lambda i, j: (i * nc + j, 0))
    in_specs = [tok, tok, tok, tok, tok,
                pl.BlockSpec((N_GATE, CHUNK), lambda i, j: (0, i * nc + j)),
                pl.BlockSpec((CHUNK, LANES), lambda i, j: (i * nc + j, 0)),
                pl.BlockSpec((1, d_m), lambda i, j: (0, 0))]
    out_shape = (jax.ShapeDtypeStruct((b * l, d_m), BF16),
                 jax.ShapeDtypeStruct((b, H_M, dh, dh), F32),
                 jax.ShapeDtypeStruct((b, H_M, dh), F32),
                 jax.ShapeDtypeStruct((b, 1, LANES), F32))
    out_specs = (tok,
                 pl.BlockSpec((None, H_M, dh, dh), lambda i, j: (i, 0, 0, 0)),
                 pl.BlockSpec((None, H_M, dh), lambda i, j: (i, 0, 0)),
                 pl.BlockSpec((None, 1, LANES), lambda i, j: (i, 0, 0)))
    return pl.pallas_call(
        functools.partial(_mlstm_prompt_kernel, dh=dh),
        grid=(b, nc), in_specs=in_specs, out_specs=out_specs, out_shape=out_shape,
        compiler_params=_params(("parallel", "arbitrary")), name="mlstm_prompt",
    )(q, k, v, o_pre, gate, grow, gcol, g_norm)


def _diff_prompt_kernel(lq1_ref, lk1_ref, lq2_ref, lk2_ref, q_ref, k_ref, vt_ref, gate_ref, g_ref,
                        o_ref, m_sc, l_sc, acc_sc, *, lam_init, dqk):
    tq = q_ref.shape[0]
    dv = 2 * dqk
    qi = pl.program_id(1)
    lam = _lambda(lq1_ref, lk1_ref, lq2_ref, lk2_ref, lam_init)
    lane = lax.broadcasted_iota(jnp.int32, (tq, dv), 1)
    key_i = lax.broadcasted_iota(jnp.int32, (tq, 2 * tq), 0)
    qry_i = lax.broadcasted_iota(jnp.int32, (tq, 2 * tq), 1)
    visible = key_i <= jnp.where(qry_i >= tq, qry_i - tq, qry_i)
    q_both = []
    for h in range(H_D):
        q = q_ref[:, h * dv:(h + 1) * dv]
        zero = jnp.zeros_like(q)
        q_both.append(jnp.concatenate([jnp.where(lane < dqk, q, zero),
                                       jnp.where(lane >= dqk, q, zero)], axis=0))
    m_sc[...] = jnp.full(m_sc.shape, -jnp.inf, F32)
    l_sc[...] = jnp.zeros_like(l_sc)
    acc_sc[...] = jnp.zeros_like(acc_sc)

    def key_span(first_tile, n_tiles, on_diagonal):
        start = pl.multiple_of(first_tile * tq, tq)
        for h in range(H_D):
            hs = slice(h * dv, (h + 1) * dv)
            k = k_ref[pl.ds(start, n_tiles * tq), hs]
            st = lax.dot_general(k, q_both[h], NT_DIMS, preferred_element_type=F32)
            if on_diagonal:
                st = jnp.where(visible, st, -jnp.inf)
            m_old = m_sc[h]
            m_new = jnp.maximum(m_old, jnp.max(st, axis=0, keepdims=True))
            alpha = jnp.exp(m_old - m_new)
            pt = jnp.exp(st - m_new)
            l_sc[h] = alpha * l_sc[h] + jnp.sum(pt, axis=0, keepdims=True)
            pt = pt.astype(BF16)
            pv = jnp.dot(vt_ref[first_tile, hs, :], pt[0:tq], preferred_element_type=F32)
            for t in range(1, n_tiles):
                pv += jnp.dot(vt_ref[first_tile + t, hs, :], pt[t * tq:(t + 1) * tq],
                              preferred_element_type=F32)
            acc_sc[h] = alpha * acc_sc[h] + pv
            m_sc[h] = m_new

    def tile_pair(i, carry):
        key_span(2 * i, 2, False)
        return carry

    lax.fori_loop(0, qi // 2, tile_pair, 0)

    @pl.when(qi % 2 == 1)
    def _():
        key_span(qi - 1, 1, False)

    key_span(qi, 1, True)

    for h in range(H_D):
        hs = slice(h * dv, (h + 1) * dv)
        on = acc_sc[h] / l_sc[h]
        o = (on[:, 0:tq] - lam * on[:, tq:2 * tq]).T
        o = o * lax.rsqrt(jnp.mean(o * o, axis=1, keepdims=True) + EPS) * g_ref[...]
        o_ref[:, hs] = (o * (1.0 - lam_init) * _silu(gate_ref[:, hs])).astype(o_ref.dtype)


def _diff_prompt(lams, dq, dk, dv_t, gate, g_norm, b, l, lam_init, gate_col0):
    d_d = dq.shape[1]
    dv_d = d_d // H_D
    tq = dv_t.shape[2]
    nq = l // tq
    lam_spec = pl.BlockSpec(lams[0].shape, lambda i, j: (0, 0))
    tok = pl.BlockSpec((tq, d_d), lambda i, j: (i * nq + j, 0))
    in_specs = [lam_spec] * 4 + [
        tok,
        pl.BlockSpec((l, d_d), lambda i, j: (i, 0)),
        pl.BlockSpec((nq, d_d, tq), lambda i, j: (i, 0, 0)),
        pl.BlockSpec((tq, d_d), lambda i, j: (i * nq + j, gate_col0 // d_d)),
        pl.BlockSpec((1, dv_d), lambda i, j: (0, 0))]
    return pl.pallas_call(
        functools.partial(_diff_prompt_kernel, lam_init=lam_init, dqk=dv_d // 2),
        grid=(b, nq), in_specs=in_specs, out_specs=tok,
        out_shape=jax.ShapeDtypeStruct((b * l, d_d), BF16),
        scratch_shapes=[pltpu.VMEM((H_D, 1, 2 * tq), F32), pltpu.VMEM((H_D, 1, 2 * tq), F32),
                        pltpu.VMEM((H_D, dv_d, 2 * tq), F32)],
        compiler_params=_params(("parallel", "parallel")), name="diff_attn_prompt",
    )(*lams, dq, dk, dv_t, gate, g_norm)


def _mem_kv_kernel(x_ref, w_ref, kf_ref, kb_ref, vf_ref, vb_ref):
    tm, d_c = kb_ref.shape
    dh = d_c // H_C
    x = x_ref[...]
    mk = jnp.dot(x, w_ref[:, 0:d_c], preferred_element_type=F32)
    mv = jnp.dot(x, w_ref[:, d_c:2 * d_c], preferred_element_type=F32)
    kb_ref[...] = mk.astype(kb_ref.dtype)
    vb_ref[...] = mv.astype(vb_ref.dtype)
    for h in range(H_C):
        sl = slice(h * dh, (h + 1) * dh)
        kf_ref[pl.ds(h, tm, stride=H_C), :] = mk[:, sl]
        vf_ref[pl.ds(h, tm, stride=H_C), :] = mv[:, sl]


def _mem_kv(mem2d, w, tm):
    m, d_model = mem2d.shape
    d_c = w.shape[1] // 2
    dh = d_c // H_C
    row = pl.BlockSpec((tm, d_c), lambda i: (i, 0))
    head_rows = pl.BlockSpec((tm * H_C, dh), lambda i: (i, 0))
    return pl.pallas_call(
        _mem_kv_kernel, grid=(m // tm,),
        in_specs=[pl.BlockSpec((tm, d_model), lambda i: (i, 0)),
                  pl.BlockSpec(w.shape, lambda i: (0, 0))],
        out_specs=(head_rows, row, head_rows, row),
        out_shape=(jax.ShapeDtypeStruct((m * H_C, dh), F32), jax.ShapeDtypeStruct((m, d_c), BF16),
                   jax.ShapeDtypeStruct((m * H_C, dh), F32), jax.ShapeDtypeStruct((m, d_c), BF16)),
        compiler_params=_params(("parallel",)), name="memory_kv",
    )(mem2d, w)


def _mem_prompt_kernel(q_ref, k_ref, v_ref, gate_ref, o_ref, *, dh):
    for h in range(H_C):
        sl = slice(h * dh, (h + 1) * dh)
        s = lax.dot_general(q_ref[:, sl], k_ref[:, sl], NT_DIMS,
                            preferred_element_type=F32) * (dh ** -0.5)
        p = jnp.exp(s - jnp.max(s, axis=1, keepdims=True))
        p = p / jnp.sum(p, axis=1, keepdims=True)
        o = jnp.dot(p.astype(BF16), v_ref[:, sl], preferred_element_type=F32)
        o_ref[:, sl] = (o * _silu(gate_ref[:, sl])).astype(o_ref.dtype)


def _mem_prompt(cq, mk, mv, gate, b, l, n_mem, tl, gate_col0):
    d_c = cq.shape[1]
    nl = l // tl
    tok = pl.BlockSpec((tl, d_c), lambda i, j: (i * nl + j, 0))
    mem = pl.BlockSpec((n_mem, d_c), lambda i, j: (i, 0))
    return pl.pallas_call(
        functools.partial(_mem_prompt_kernel, dh=d_c // H_C),
        grid=(b, nl),
        in_specs=[tok, mem, mem,
                  pl.BlockSpec((tl, d_c), lambda i, j: (i * nl + j, gate_col0 // d_c))],
        out_specs=tok, out_shape=jax.ShapeDtypeStruct((b * l, d_c), BF16),
        compiler_params=_params(("parallel", "parallel")), name="memory_attn_prompt",
    )(cq, mk, mv, gate)


def _merge_kernel(x_ref, hm_ref, hd_ref, hc_ref, w_ref, g_ref, b_ref, y_ref, *, alpha):
    d_m = hm_ref.shape[1]
    d_d = hd_ref.shape[1]
    acc = jnp.dot(hm_ref[...], w_ref[0:d_m, :], preferred_element_type=F32)
    acc += jnp.dot(hd_ref[...], w_ref[d_m:d_m + d_d, :], preferred_element_type=F32)
    acc += jnp.dot(hc_ref[...], w_ref[d_m + d_d:, :], preferred_element_type=F32)
    y = alpha * x_ref[...] + acc
    mu = jnp.mean(y, axis=1, keepdims=True)
    var = jnp.mean(jnp.square(y - mu), axis=1, keepdims=True)
    y_ref[...] = (y - mu) * lax.rsqrt(var + EPS) * g_ref[...] + b_ref[...]


def _merge(x2d, h_m, h_d, h_c, w_out, ln_g, ln_b, tm, alpha):
    m, d_model = x2d.shape

    def row(n):
        return pl.BlockSpec((tm, n), lambda i: (i, 0))

    def whole(a):
        return pl.BlockSpec(a.shape, lambda i: (0, 0))

    return pl.pallas_call(
        functools.partial(_merge_kernel, alpha=alpha), grid=(m // tm,),
        in_specs=[row(d_model), row(h_m.shape[1]), row(h_d.shape[1]), row(h_c.shape[1]),
                  whole(w_out), whole(ln_g), whole(ln_b)],
        out_specs=row(d_model), out_shape=jax.ShapeDtypeStruct((m, d_model), F32),
        compiler_params=_params(("parallel",)), name="output_projection",
    )(x2d, h_m, h_d, h_c, w_out, ln_g, ln_b)


def _mlstm_step_kernel(q_ref, k_ref, v_ref, o_ref, gate_ref, ig_ref, fg_ref, m_ref, n_ref, c_ref,
                       g_ref, h_ref, cn_ref, nn_ref, mn_ref, *, dh):
    tb = q_ref.shape[0]
    r_idx = lax.broadcasted_iota(jnp.int32, (dh, dh), 0)
    c_idx = lax.broadcasted_iota(jnp.int32, (dh, dh), 1)
    eye = r_idx == c_idx

    def column(row):
        return jnp.sum(jnp.where(eye, row, 0.0), axis=1, keepdims=True)

    for b in range(tb):
        for h in range(H_M):
            sl = slice(h * dh, (h + 1) * dh)
            q = q_ref[b, :, sl]
            k = k_ref[b, :, sl]
            v = v_ref[b, :, sl]
            ig = ig_ref[b, h]
            lf = _log_sigmoid(fg_ref[b, h])
            m_prev = m_ref[b, h]
            n_old = n_ref[b, h]
            c_old = c_ref[b, h]

            inter = lf + m_prev
            mt = jnp.maximum(ig, inter)
            w_in = jnp.exp(ig - mt)
            w_st = jnp.exp(inter - mt)
            s = jnp.sum(q * k, axis=1, keepdims=True) * w_in
            qc = jnp.sum(column(q) * c_old, axis=0, keepdims=True)
            num = s * v + w_st * qc
            den = s + w_st * jnp.sum(q * n_old, axis=1, keepdims=True)
            hc = num / jnp.maximum(jnp.abs(den), jnp.exp(-mt))

            cn_ref[b, h] = w_st * c_old + (w_in * column(k)) * v
            nn_ref[b, h] = w_st * n_old + w_in * k
            mn_ref[b, h] = mt

            mu = jnp.mean(hc, axis=1, keepdims=True)
            var = jnp.mean(jnp.square(hc - mu), axis=1, keepdims=True)
            hn = (hc - mu) * lax.rsqrt(var + EPS) * g_ref[:, sl]
            hn = hn * jax.nn.sigmoid(o_ref[b, :, sl])
            h_ref[b, :, sl] = (hn * _silu(gate_ref[b, :, sl])).astype(h_ref.dtype)


def _mlstm_step(q, k, v, o_pre, gate, ig, fg, m0, n0, c0, g_norm, tb):
    bd, _, d_m = q.shape
    dh = d_m // H_M

    def blk(shape):
        nd = len(shape)
        return pl.BlockSpec((tb,) + tuple(shape), lambda i: (i,) + (0,) * nd)

    tok = blk((1, d_m))
    one = blk((H_M, 1, 1))
    in_specs = [tok, tok, tok, tok, tok, one, one, one, blk((H_M, 1, dh)), blk((H_M, dh, dh)),
                pl.BlockSpec((1, d_m), lambda i: (0, 0))]
    out_shape = (jax.ShapeDtypeStruct((bd, 1, d_m), BF16),
                 jax.ShapeDtypeStruct((bd, H_M, dh, dh), F32),
                 jax.ShapeDtypeStruct((bd, H_M, 1, dh), F32),
                 jax.ShapeDtypeStruct((bd, H_M, 1, 1), F32))
    out_specs = (tok, blk((H_M, dh, dh)), blk((H_M, 1, dh)), one)
    return pl.pallas_call(
        functools.partial(_mlstm_step_kernel, dh=dh), grid=(bd // tb,),
        in_specs=in_specs, out_specs=out_specs, out_shape=out_shape,
        compiler_params=_params(("parallel",)), name="mlstm_step",
    )(q, k, v, o_pre, gate, ig, fg, m0, n0, c0, g_norm)


def _head_rows(row, n_heads, reps):
    dh = row.shape[1] // n_heads
    heads = [row[:, h * dh:(h + 1) * dh] for h in range(n_heads)]
    return jnp.concatenate(heads * reps, axis=0)


def _own_head(n_rows, n_cols, n_heads):
    r = lax.broadcasted_iota(jnp.int32, (n_rows, n_cols), 0)
    c = lax.broadcasted_iota(jnp.int32, (n_rows, n_cols), 1)
    return (c % n_heads) == (r % n_heads)


def _diff_decode_kernel(pt_ref, lq1_ref, lk1_ref, lq2_ref, lk2_ref, q_ref, kn_ref, vn_ref,
                        gate_ref, g_ref, *rest, n_pages, lam_init):
    del pt_ref
    k_refs = rest[:n_pages]
    v_refs = rest[n_pages:2 * n_pages]
    o_ref = rest[2 * n_pages]
    n_rows, dv_d = k_refs[0].shape
    dqk = dv_d // 2
    lam = _lambda(lq1_ref, lk1_ref, lq2_ref, lk2_ref, lam_init)

    q8 = _head_rows(q_ref[0].astype(F32), H_D, 2)
    r_idx = lax.broadcasted_iota(jnp.int32, q8.shape, 0)
    l_idx = lax.broadcasted_iota(jnp.int32, q8.shape, 1)
    q8 = jnp.where((l_idx // dqk) == (r_idx // H_D), q8, 0.0)
    q8b = q8.astype(BF16)
    own = _own_head(2 * H_D, n_rows, H_D)

    scores = [jnp.where(own, lax.dot_general(q8b, k_refs[j][...].astype(BF16), NT_DIMS,
                                             preferred_element_type=F32), -jnp.inf)
              for j in range(n_pages)]
    kn4 = kn_ref[0]
    vn4 = vn_ref[0]
    s_new = jnp.sum(q8 * jnp.concatenate([kn4, kn4], axis=0), axis=1, keepdims=True)
    mx = s_new
    for s in scores:
        mx = jnp.maximum(mx, jnp.max(s, axis=1, keepdims=True))
    probs = [jnp.exp(s - mx) for s in scores]
    p_new = jnp.exp(s_new - mx)
    total = p_new
    for p in probs:
        total = total + jnp.sum(p, axis=1, keepdims=True)
    inv = 1.0 / total

    def combine(p):
        pn = p * inv
        return pn - lam * pltpu.roll(pn, SUBLANES - H_D, 0)

    acc = (combine(jnp.broadcast_to(p_new, (SUBLANES, LANES)))[:, 0:1]
           * jnp.concatenate([vn4, vn4], axis=0))
    for j in range(n_pages):
        acc = acc + jnp.dot(combine(probs[j]).astype(BF16), v_refs[j][...].astype(BF16),
                            preferred_element_type=F32)
    for h in range(H_D):
        sl = slice(h * dv_d, (h + 1) * dv_d)
        o = acc[h:h + 1, :]
        o = o * lax.rsqrt(jnp.mean(o * o, axis=1, keepdims=True) + EPS) * g_ref[...]
        o_ref[0, :, sl] = (o * (1.0 - lam_init) * _silu(gate_ref[0, :, sl])).astype(o_ref.dtype)


def _diff_decode(page_table, lams, dq, dk_new, dv_new, gate, g_norm, cache_k, cache_v,
                 lam_init, gate_col0):
    bd, _, d_d = dq.shape
    n_pages = page_table.shape[1]
    _, n_rows, dv_d = cache_k.shape
    lam_spec = pl.BlockSpec(lams[0].shape, lambda i, pt: (0, 0))
    tok = pl.BlockSpec((1, 1, d_d), lambda i, pt: (i, 0, 0))
    page_specs = [pl.BlockSpec((None, n_rows, dv_d), lambda i, pt, j=j: (pt[i, j], 0, 0))
                  for j in range(n_pages)]
    new_tok = pl.BlockSpec((1, H_D, dv_d), lambda i, pt: (i, 0, 0))
    in_specs = ([lam_spec] * 4 + [tok, new_tok, new_tok,
                                  pl.BlockSpec((1, 1, d_d), lambda i, pt: (i, 0, gate_col0 // d_d)),
                                  pl.BlockSpec((1, dv_d), lambda i, pt: (0, 0))]
                + page_specs + page_specs)
    grid_spec = pltpu.PrefetchScalarGridSpec(
        num_scalar_prefetch=1, grid=(bd,), in_specs=in_specs, out_specs=tok)
    return pl.pallas_call(
        functools.partial(_diff_decode_kernel, n_pages=n_pages, lam_init=lam_init),
        grid_spec=grid_spec, out_shape=jax.ShapeDtypeStruct((bd, 1, d_d), BF16),
        compiler_params=_params(("parallel",)), name="diff_attn_decode",
    )(page_table, *lams, dq, dk_new, dv_new, gate, g_norm,
      *([cache_k] * n_pages), *([cache_v] * n_pages))


def _mem_decode_kernel(q_ref, k_ref, v_ref, gate_ref, o_ref):
    tb, n_rows, dh = k_ref.shape
    own = _own_head(SUBLANES, n_rows, H_C)
    for b in range(tb):
        q8 = _head_rows(q_ref[b].astype(F32), H_C, SUBLANES // H_C).astype(BF16)
        s = lax.dot_general(q8, k_ref[b].astype(BF16), NT_DIMS,
                            preferred_element_type=F32) * (dh ** -0.5)
        s = jnp.where(own, s, -jnp.inf)
        p = jnp.exp(s - jnp.max(s, axis=1, keepdims=True))
        p = p / jnp.sum(p, axis=1, keepdims=True)
        acc = jnp.dot(p.astype(BF16), v_ref[b].astype(BF16), preferred_element_type=F32)
        for h in range(H_C):
            sl = slice(h * dh, (h + 1) * dh)
            o_ref[b, :, sl] = (acc[h:h + 1, :] * _silu(gate_ref[b, :, sl])).astype(o_ref.dtype)


def _mem_decode(cq, mem_k, mem_v, gate, gate_col0, tb):
    bd, _, d_c = cq.shape
    _, n_rows, dh = mem_k.shape
    tok = pl.BlockSpec((tb, 1, d_c), lambda i: (i, 0, 0))
    mem = pl.BlockSpec((tb, n_rows, dh), lambda i: (i, 0, 0))
    return pl.pallas_call(
        _mem_decode_kernel, grid=(bd // tb,),
        in_specs=[tok, mem, mem, pl.BlockSpec((tb, 1, d_c), lambda i: (i, 0, gate_col0 // d_c))],
        out_specs=tok, out_shape=jax.ShapeDtypeStruct((bd, 1, d_c), BF16),
        compiler_params=_params(("parallel",)), name="memory_attn_decode",
    )(cq, mem_k, mem_v, gate)


def _rope_tables(pos, n_rows, dqk):
    half = dqk // 2
    inv_freq = ROPE_THETA ** (-jnp.arange(half, dtype=F32) / half)
    ang = pos.astype(F32)[:, None] * inv_freq[None, :]
    cos = jnp.cos(ang)
    sin = jnp.sin(ang)
    reps = LANES // dqk
    cos_t = jnp.tile(jnp.concatenate([cos, cos], axis=1), (1, reps))
    sin_t = jnp.tile(jnp.concatenate([-sin, sin], axis=1), (1, reps))
    return (jnp.broadcast_to(cos_t, (n_rows, LANES)), jnp.broadcast_to(sin_t, (n_rows, LANES)))


def _split_weights(w_in_l, b_if_l, d_m, d_d, d_mix):
    g0 = 4 * d_m
    d0 = g0 + N_GATE
    wm = w_in_l[:, :g0].astype(BF16)
    w_gates = w_in_l[:, g0:d0]
    wg = jnp.pad(w_gates, ((0, 0), (0, LANES - N_GATE))).astype(BF16)
    wgt = w_gates.T.astype(BF16)
    wd = w_in_l[:, d0:d0 + 4 * d_d].astype(BF16)
    wgate = w_in_l[:, d0 + 4 * d_d:].astype(BF16)
    bcol = jnp.pad(b_if_l, (0, LANES - N_GATE)).reshape(1, LANES).astype(F32)
    brow = jnp.broadcast_to(b_if_l.astype(F32)[:, None], (N_GATE, LANES))
    return wm, wg, wgt, wd, wgate, bcol, brow


def kernel(x_prompt, x_sample, state_mlstm_c, state_mlstm_n, state_mlstm_m, cache_diff_k, cache_diff_v,
           cache_mem_k, cache_mem_v, page_table, mem_prompt, w_in, b_if, w_mlstm_norm,
           lambda_q1, lambda_k1, lambda_q2, lambda_k2, w_diff_norm, w_mem_kv, w_out, ln_g, ln_b):
    depth = w_in.shape[0]
    assert depth == 1, "single-layer trunk"
    bp, lp, d_model = x_prompt.shape
    bd, ls, _ = x_sample.shape
    assert ls == 1
    d_mix = w_out.shape[1]
    d_m = d_mix // 2
    d_d = d_mix // 4
    d_c = d_mix // 4
    dh_m = d_m // H_M
    dv_d = d_d // H_D
    dqk = dv_d // 2
    assert dv_d == LANES and d_c // H_C == LANES, "one attention head per lane tile"
    n_mem = mem_prompt.shape[1]
    n_pool, page = cache_diff_k.shape[1], cache_diff_k.shape[2]
    past_len = page_table.shape[1] * page
    alpha = (2 * depth) ** 0.25
    layer = 0
    lam_init = 0.8 - 0.6 * math.exp(-0.3 * layer)

    wts = _split_weights(w_in[layer], b_if[layer], d_m, d_d, d_mix)
    lams = tuple(a[layer].reshape(1, dqk).astype(F32)
                 for a in (lambda_q1, lambda_k1, lambda_q2, lambda_k2))
    g_m = w_mlstm_norm[layer].reshape(1, d_m).astype(F32)
    g_d = w_diff_norm[layer].reshape(1, dv_d).astype(F32)
    w_out_b = w_out[layer].astype(BF16)
    g_ln = ln_g[layer].reshape(1, d_model).astype(F32)
    b_ln = ln_b[layer].reshape(1, d_model).astype(F32)
    gate_d0 = d_m
    gate_c0 = d_m + d_d

    mp = bp * lp
    xp2d = x_prompt.reshape(mp, d_model)
    cos_p, sin_p = _rope_tables(jnp.arange(lp), lp, dqk)
    (q_m, k_m, v_m, o_pre, gcol, grow, dq, dk_f, dk_b, dv_f, dv_t, cq, gate) = _input_projection(
        xp2d, wts, cos_p, sin_p, tm=256, qkv_dtype=BF16)
    h_m, c_p, n_p, m_p = _mlstm_prompt(q_m, k_m, v_m, o_pre, gate, grow, gcol, g_m, bp, lp)
    h_d = _diff_prompt(lams, dq, dk_b, dv_t, gate, g_d, bp, lp, lam_init, gate_d0)
    mk_f, mk_b, mv_f, mv_b = _mem_kv(mem_prompt.reshape(bp * n_mem, d_model).astype(BF16),
                                     w_mem_kv[layer].astype(BF16), tm=512)
    h_c = _mem_prompt(cq, mk_b, mv_b, gate, bp, lp, n_mem, 512, gate_c0)
    y_p = _merge(xp2d, h_m, h_d, h_c, w_out_b, g_ln, b_ln, 512, alpha)

    xs2d = x_sample.reshape(bd, d_model)
    cos_s, sin_s = _rope_tables(jnp.full((1,), past_len), bd, dqk)
    (qs, ks, vs, os_pre, gcol_s, _, dq_s, dk_s, _, dv_s, _, cq_s, gate_s) = _input_projection(
        xs2d, wts, cos_s, sin_s, tm=bd, qkv_dtype=F32)

    def tok3(a):
        return a.reshape(bd, 1, a.shape[1])

    ig_s = gcol_s[:, 0:H_M].reshape(bd, H_M, 1, 1)
    fg_s = gcol_s[:, H_M:N_GATE].reshape(bd, H_M, 1, 1)
    hs_m, c_s, n_s, m_s = _mlstm_step(
        tok3(qs), tok3(ks), tok3(vs), tok3(os_pre), tok3(gate_s), ig_s, fg_s,
        state_mlstm_m[layer].astype(F32).reshape(bd, H_M, 1, 1),
        state_mlstm_n[layer].astype(F32).reshape(bd, H_M, 1, dh_m),
        state_mlstm_c[layer].astype(F32), g_m, tb=4)
    hs_d = _diff_decode(page_table, lams, tok3(dq_s), dk_s.reshape(bd, H_D, dv_d),
                        dv_s.reshape(bd, H_D, dv_d), tok3(gate_s), g_d,
                        cache_diff_k[layer].reshape(n_pool, page * H_D, dv_d),
                        cache_diff_v[layer].reshape(n_pool, page * H_D, dv_d), lam_init, gate_d0)
    hs_c = _mem_decode(tok3(cq_s), cache_mem_k[layer].reshape(bd, n_mem * H_C, d_c // H_C),
                       cache_mem_v[layer].reshape(bd, n_mem * H_C, d_c // H_C), tok3(gate_s),
                       gate_c0, tb=4)
    y_s = _merge(xs2d, hs_m.reshape(bd, d_m), hs_d.reshape(bd, d_d), hs_c.reshape(bd, d_c),
                 w_out_b, g_ln, b_ln, bd, alpha)

    return (y_p.reshape(bp, lp, d_model), y_s.reshape(bd, ls, d_model),
            c_p[None], n_p[None], m_p[:, 0, 0:H_M][None],
            dk_f.reshape(1, bp, lp, H_D, dv_d), dv_f.reshape(1, bp, lp, H_D, dv_d),
            mk_f.reshape(1, bp, n_mem, H_C, d_c // H_C), mv_f.reshape(1, bp, n_mem, H_C, d_c // H_C),
            c_s[None], n_s.reshape(1, bd, H_M, dh_m), m_s.reshape(1, bd, H_M),
            dk_s.reshape(1, bd, ls, H_D, dv_d), dv_s.reshape(1, bd, ls, H_D, dv_d))
```

```python
import functools
import math

import jax
import jax.numpy as jnp
from jax import lax
from jax.experimental import pallas as pl
from jax.experimental.pallas import tpu as pltpu

F32 = jnp.float32
BF16 = jnp.bfloat16

H_M = 4
H_D = 4
H_C = 4
N_GATE = 2 * H_M
CHUNK = 128
ROPE_THETA = 10000.0
EPS = 1e-5
LANES = 128
SUBLANES = 8
VMEM_LIMIT = 56 * 1024 * 1024

NT_DIMS = (((1,), (1,)), ((), ()))
TN_DIMS = (((0,), (0,)), ((), ()))


def _params(semantics):
    return pltpu.CompilerParams(dimension_semantics=semantics, vmem_limit_bytes=VMEM_LIMIT)


def _log_sigmoid(x):
    return jnp.minimum(x, 0.0) - jnp.log1p(jnp.exp(-jnp.abs(x)))


def _silu(x):
    return x * jax.nn.sigmoid(x)


def _lambda(lq1_ref, lk1_ref, lq2_ref, lk2_ref, lam_init):
    a = jnp.sum(lq1_ref[...] * lk1_ref[...], axis=1, keepdims=True)
    b = jnp.sum(lq2_ref[...] * lk2_ref[...], axis=1, keepdims=True)
    return jnp.exp(a) - jnp.exp(b) + lam_init


def _inproj_kernel(x_ref, wm_ref, wg_ref, wgt_ref, wd_ref, wgate_ref, bcol_ref, brow_ref,
                   cos_ref, sin_ref,
                   q_ref, k_ref, v_ref, o_ref, gcol_ref, grow_ref,
                   dq_ref, dkf_ref, dkb_ref, dvf_ref, dvt_ref, cq_ref, gate_ref,
                   *, d_m, d_d, dh_m, dqk):
    x = x_ref[...].astype(BF16)

    def mm(w):
        return jnp.dot(x, w, preferred_element_type=F32)

    q_ref[...] = mm(wm_ref[:, 0:d_m]).astype(q_ref.dtype)
    k_ref[...] = (mm(wm_ref[:, d_m:2 * d_m]) * (dh_m ** -0.5)).astype(k_ref.dtype)
    v_ref[...] = mm(wm_ref[:, 2 * d_m:3 * d_m]).astype(v_ref.dtype)
    o_ref[...] = mm(wm_ref[:, 3 * d_m:4 * d_m])
    gcol_ref[...] = mm(wg_ref[...]) + bcol_ref[...]
    grow_ref[...] = (lax.dot_general(wgt_ref[...], x, NT_DIMS, preferred_element_type=F32)
                     + brow_ref[:, 0:1])

    cos = cos_ref[...]
    sin = sin_ref[...]
    lane = lax.broadcasted_iota(jnp.int32, cos.shape, 1)
    first_half = (lane % dqk) < (dqk // 2)

    def rope(t):
        swapped = jnp.where(first_half, pltpu.roll(t, LANES - dqk // 2, 1),
                            pltpu.roll(t, dqk // 2, 1))
        return t * cos + swapped * sin

    tm = x.shape[0]
    dq = mm(wd_ref[:, 0:d_d])
    dk = mm(wd_ref[:, d_d:2 * d_d])
    dv = mm(wd_ref[:, 2 * d_d:3 * d_d])
    dvt_ref[0] = dv.T.astype(dvt_ref.dtype)
    for h in range(H_D):
        sl = slice(h * LANES, (h + 1) * LANES)
        dq_ref[:, sl] = (rope(dq[:, sl]) * (dqk ** -0.5)).astype(dq_ref.dtype)
        dk_h = rope(dk[:, sl])
        dkb_ref[:, sl] = dk_h.astype(dkb_ref.dtype)
        dkf_ref[pl.ds(h, tm, stride=H_D), :] = dk_h
        dvf_ref[pl.ds(h, tm, stride=H_D), :] = dv[:, sl]
    cq_ref[...] = mm(wd_ref[:, 3 * d_d:4 * d_d]).astype(cq_ref.dtype)
    gate_ref[...] = mm(wgate_ref[...])


def _input_projection(x2d, wts, cos_tab, sin_tab, tm, qkv_dtype):
    m, d_model = x2d.shape
    wm, wg, wgt, wd, wgate, bcol, brow = wts
    d_m = wm.shape[1] // 4
    d_d = wd.shape[1] // 4
    d_mix = wgate.shape[1]
    dh_m = d_m // H_M
    dqk = d_d // H_D // 2
    grid = (m // tm,)

    def row_spec(n):
        return pl.BlockSpec((tm, n), lambda i: (i, 0))

    def whole(a):
        return pl.BlockSpec(a.shape, lambda i: (0,) * a.ndim, pipeline_mode=pl.Buffered(1))

    out_shape = (
        jax.ShapeDtypeStruct((m, d_m), qkv_dtype),
        jax.ShapeDtypeStruct((m, d_m), qkv_dtype),
        jax.ShapeDtypeStruct((m, d_m), qkv_dtype),
        jax.ShapeDtypeStruct((m, d_m), F32),
        jax.ShapeDtypeStruct((m, LANES), F32),
        jax.ShapeDtypeStruct((N_GATE, m), F32),
        jax.ShapeDtypeStruct((m, d_d), BF16),
        jax.ShapeDtypeStruct((m * H_D, LANES), F32),
        jax.ShapeDtypeStruct((m, d_d), BF16),
        jax.ShapeDtypeStruct((m * H_D, LANES), F32),
        jax.ShapeDtypeStruct((m // tm, d_d, tm), BF16),
        jax.ShapeDtypeStruct((m, d_d), BF16),
        jax.ShapeDtypeStruct((m, d_mix), F32),
    )
    head_rows = pl.BlockSpec((tm * H_D, LANES), lambda i: (i, 0))
    out_specs = (
        row_spec(d_m), row_spec(d_m), row_spec(d_m), row_spec(d_m), row_spec(LANES),
        pl.BlockSpec((N_GATE, tm), lambda i: (0, i)),
        row_spec(d_d), head_rows, row_spec(d_d), head_rows,
        pl.BlockSpec((1, d_d, tm), lambda i: (i, 0, 0)), row_spec(d_d),
        row_spec(d_mix),
    )
    n_tab = cos_tab.shape[0] // tm
    tab_spec = pl.BlockSpec((tm, LANES), lambda i: (i % n_tab, 0))
    in_specs = [row_spec(d_model), whole(wm), whole(wg), whole(wgt), whole(wd), whole(wgate),
                whole(bcol), whole(brow), tab_spec, tab_spec]
    return pl.pallas_call(
        functools.partial(_inproj_kernel, d_m=d_m, d_d=d_d, dh_m=dh_m, dqk=dqk),
        grid=grid, in_specs=in_specs, out_specs=out_specs, out_shape=out_shape,
        compiler_params=_params(("parallel",)), name="input_projection",
    )(x2d, wm, wg, wgt, wd, wgate, bcol, brow, cos_tab, sin_tab)


def _mlstm_prompt_kernel(q_ref, k_ref, v_ref, o_ref, gate_ref, grow_ref, gcol_ref, g_ref,
                         h_ref, c_ref, n_ref, m_ref, *, dh):
    @pl.when(pl.program_id(1) == 0)
    def _():
        c_ref[...] = jnp.zeros_like(c_ref)
        n_ref[...] = jnp.zeros_like(n_ref)
        m_ref[...] = jnp.zeros_like(m_ref)

    c = q_ref.shape[0]
    t_idx = lax.broadcasted_iota(jnp.int32, (c, c), 0)
    s_idx = lax.broadcasted_iota(jnp.int32, (c, c), 1)
    causal = s_idx <= t_idx
    causal_t = t_idx <= s_idx

    for h in range(H_M):
        sl = slice(h * dh, (h + 1) * dh)
        q = q_ref[:, sl]
        k = k_ref[:, sl]
        v = v_ref[:, sl]
        ig_row = grow_ref[h:h + 1, :]
        lf_row = _log_sigmoid(grow_ref[H_M + h:H_M + h + 1, :])
        ig_col = gcol_ref[:, h:h + 1]
        lf_col = _log_sigmoid(gcol_ref[:, H_M + h:H_M + h + 1])
        bcum_col = jnp.sum(jnp.where(causal, lf_row, 0.0), axis=1, keepdims=True)
        bcum_row = jnp.sum(jnp.where(causal_t, lf_col, 0.0), axis=0, keepdims=True)
        m_prev = m_ref[:, h:h + 1]

        dlog = jnp.where(causal, bcum_col - bcum_row + ig_row, -jnp.inf)
        inter = bcum_col + m_prev
        mt = jnp.maximum(jnp.max(dlog, axis=1, keepdims=True), inter)
        w_intra = jnp.exp(dlog - mt)
        w_inter = jnp.exp(inter - mt)
        s = lax.dot_general(q, k, NT_DIMS, preferred_element_type=F32) * w_intra
        c_old = c_ref[h]
        n_old = n_ref[h:h + 1, :]
        num = (jnp.dot(s.astype(BF16), v, preferred_element_type=F32)
               + w_inter * jnp.dot(q, c_old.astype(BF16), preferred_element_type=F32))
        qn = jnp.sum(q.astype(F32) * n_old, axis=1, keepdims=True)
        den = jnp.sum(s, axis=1, keepdims=True) + w_inter * qn
        hc = num / jnp.maximum(jnp.abs(den), jnp.exp(-mt))

        m_new = mt[c - 1:c, :]
        b_last = bcum_col[c - 1:c, :]
        w_end = jnp.exp(b_last - bcum_col + ig_col - m_new)
        decay = jnp.exp(b_last + m_prev - m_new)
        kw = k.astype(F32) * w_end
        c_ref[h] = decay * c_old + lax.dot_general(kw.astype(BF16), v, TN_DIMS,
                                                   preferred_element_type=F32)
        n_ref[h:h + 1, :] = decay * n_old + jnp.sum(kw, axis=0, keepdims=True)
        m_ref[:, h:h + 1] = m_new

        mu = jnp.mean(hc, axis=1, keepdims=True)
        var = jnp.mean(jnp.square(hc - mu), axis=1, keepdims=True)
        hn = (hc - mu) * lax.rsqrt(var + EPS) * g_ref[:, sl]
        hn = hn * jax.nn.sigmoid(o_ref[:, sl])
        h_ref[:, sl] = (hn * _silu(gate_ref[:, sl])).astype(h_ref.dtype)


def _mlstm_prompt(q, k, v, o_pre, gate, grow, gcol, g_norm, b, l):
    d_m = q.shape[1]
    dh = d_m // H_M
    nc = l // CHUNK
    tok = pl.BlockSpec((CHUNK, d_m), lambda i, j: (i * nc + j, 0))
    in_specs = [tok, tok, tok, tok, tok,
                pl.BlockSpec((N_GATE, CHUNK), lambda i, j: (0, i * nc + j)),
                pl.BlockSpec((CHUNK, LANES), lambda i, j: (i * nc + j, 0)),
                pl.BlockSpec((1, d_m), lambda i, j: (0, 0))]
    out_shape = (jax.ShapeDtypeStruct((b * l, d_m), BF16),
                 jax.ShapeDtypeStruct((b, H_M, dh, dh), F32),
                 jax.ShapeDtypeStruct((b, H_M, dh), F32),
                 jax.ShapeDtypeStruct((b, 1, LANES), F32))
    out_specs = (tok,
                 pl.BlockSpec((None, H_M, dh, dh), lambda i, j: (i, 0, 0, 0)),
                 pl.BlockSpec((None, H_M, dh), lambda i, j: (i, 0, 0)),
                 pl.BlockSpec((None, 1, LANES), lambda i, j: (i, 0, 0)))
    return pl.pallas_call(
        functools.partial(_mlstm_prompt_kernel, dh=dh),
        grid=(b, nc), in_specs=in_specs, out_specs=out_specs, out_shape=out_shape,
        compiler_params=_params(("parallel", "arbitrary")), name="mlstm_prompt",
    )(q, k, v, o_pre, gate, grow, gcol, g_norm)


def _diff_prompt_kernel(lq1_ref, lk1_ref, lq2_ref, lk2_ref, q_ref, k_ref, vt_ref, gate_ref, g_ref,
                        o_ref, m_sc, l_sc, acc_sc, *, lam_init, dqk):
    tq = q_ref.shape[0]
    dv = 2 * dqk
    qi = pl.program_id(1)
    lam = _lambda(lq1_ref, lk1_ref, lq2_ref, lk2_ref, lam_init)
    lane = lax.broadcasted_iota(jnp.int32, (tq, dv), 1)
    key_i = lax.broadcasted_iota(jnp.int32, (tq, 2 * tq), 0)
    qry_i = lax.broadcasted_iota(jnp.int32, (tq, 2 * tq), 1)
    visible = key_i <= jnp.where(qry_i >= tq, qry_i - tq, qry_i)
    q_both = []
    for h in range(H_D):
        q = q_ref[:, h * dv:(h + 1) * dv]
        zero = jnp.zeros_like(q)
        q_both.append(jnp.concatenate([jnp.where(lane < dqk, q, zero),
                                       jnp.where(lane >= dqk, q, zero)], axis=0))
    m_sc[...] = jnp.full(m_sc.shape, -jnp.inf, F32)
    l_sc[...] = jnp.zeros_like(l_sc)
    acc_sc[...] = jnp.zeros_like(acc_sc)

    def key_span(first_tile, n_tiles, on_diagonal):
        start = pl.multiple_of(first_tile * tq, tq)
        scores = []
        for h in range(H_D):
            hs = slice(h * dv, (h + 1) * dv)
            k = k_ref[pl.ds(start, n_tiles * tq), hs]
            scores.append(lax.dot_general(k, q_both[h], NT_DIMS,
                                          preferred_element_type=F32))
        for h in range(H_D):
            hs = slice(h * dv, (h + 1) * dv)
            st = scores[h]
            if on_diagonal:
                st = jnp.where(visible, st, -jnp.inf)
            m_old = m_sc[h]
            m_new = jnp.maximum(m_old, jnp.max(st, axis=0, keepdims=True))
            alpha = jnp.exp(m_old - m_new)
            pt = jnp.exp(st - m_new)
            l_sc[h] = alpha * l_sc[h] + jnp.sum(pt, axis=0, keepdims=True)
            pt = pt.astype(BF16)
            pv = jnp.dot(vt_ref[first_tile, hs, :], pt[0:tq], preferred_element_type=F32)
            for t in range(1, n_tiles):
                pv += jnp.dot(vt_ref[first_tile + t, hs, :], pt[t * tq:(t + 1) * tq],
                              preferred_element_type=F32)
            acc_sc[h] = alpha * acc_sc[h] + pv
            m_sc[h] = m_new

    def tile_pair(i, carry):
        key_span(2 * i, 2, False)
        return carry

    lax.fori_loop(0, qi // 2, tile_pair, 0)

    @pl.when(qi % 2 == 1)
    def _():
        key_span(qi - 1, 1, False)

    key_span(qi, 1, True)

    for h in range(H_D):
        hs = slice(h * dv, (h + 1) * dv)
        on = acc_sc[h] / l_sc[h]
        o = (on[:, 0:tq] - lam * on[:, tq:2 * tq]).T
        o = o * lax.rsqrt(jnp.mean(o * o, axis=1, keepdims=True) + EPS) * g_ref[...]
        o_ref[:, hs] = (o * (1.0 - lam_init) * _silu(gate_ref[:, hs])).astype(o_ref.dtype)


def _diff_prompt(lams, dq, dk, dv_t, gate, g_norm, b, l, lam_init, gate_col0):
    d_d = dq.shape[1]
    dv_d = d_d // H_D
    tq = dv_t.shape[2]
    nq = l // tq
    lam_spec = pl.BlockSpec(lams[0].shape, lambda i, j: (0, 0))
    tok = pl.BlockSpec((tq, d_d), lambda i, j: (i * nq + j, 0))
    in_specs = [lam_spec] * 4 + [
        tok,
        pl.BlockSpec((l, d_d), lambda i, j: (i, 0)),
        pl.BlockSpec((nq, d_d, tq), lambda i, j: (i, 0, 0)),
        pl.BlockSpec((tq, d_d), lambda i, j: (i * nq + j, gate_col0 // d_d)),
        pl.BlockSpec((1, dv_d), lambda i, j: (0, 0))]
    return pl.pallas_call(
        functools.partial(_diff_prompt_kernel, lam_init=lam_init, dqk=dv_d // 2),
        grid=(b, nq), in_specs=in_specs, out_specs=tok,
        out_shape=jax.ShapeDtypeStruct((b * l, d_d), BF16),
        scratch_shapes=[pltpu.VMEM((H_D, 1, 2 * tq), F32), pltpu.VMEM((H_D, 1, 2 * tq), F32),
                        pltpu.VMEM((H_D, dv_d, 2 * tq), F32)],
        compiler_params=_params(("parallel", "parallel")), name="diff_attn_prompt",
    )(*lams, dq, dk, dv_t, gate, g_norm)


def _mem_kv_kernel(x_ref, w_ref, kf_ref, kb_ref, vf_ref, vb_ref):
    tm, d_c = kb_ref.shape
    dh = d_c // H_C
    x = x_ref[...]
    mk = jnp.dot(x, w_ref[:, 0:d_c], preferred_element_type=F32)
    mv = jnp.dot(x, w_ref[:, d_c:2 * d_c], preferred_element_type=F32)
    kb_ref[...] = mk.astype(kb_ref.dtype)
    vb_ref[...] = mv.astype(vb_ref.dtype)
    for h in range(H_C):
        sl = slice(h * dh, (h + 1) * dh)
        kf_ref[pl.ds(h, tm, stride=H_C), :] = mk[:, sl]
        vf_ref[pl.ds(h, tm, stride=H_C), :] = mv[:, sl]


def _mem_kv(mem2d, w, tm):
    m, d_model = mem2d.shape
    d_c = w.shape[1] // 2
    dh = d_c // H_C
    row = pl.BlockSpec((tm, d_c), lambda i: (i, 0))
    head_rows = pl.BlockSpec((tm * H_C, dh), lambda i: (i, 0))
    return pl.pallas_call(
        _mem_kv_kernel, grid=(m // tm,),
        in_specs=[pl.BlockSpec((tm, d_model), lambda i: (i, 0)),
                  pl.BlockSpec(w.shape, lambda i: (0, 0))],
        out_specs=(head_rows, row, head_rows, row),
        out_shape=(jax.ShapeDtypeStruct((m * H_C, dh), F32), jax.ShapeDtypeStruct((m, d_c), BF16),
                   jax.ShapeDtypeStruct((m * H_C, dh), F32), jax.ShapeDtypeStruct((m, d_c), BF16)),
        compiler_params=_params(("parallel",)), name="memory_kv",
    )(mem2d, w)


def _mem_prompt_kernel(q_ref, k_ref, v_ref, gate_ref, o_ref, *, dh):
    for h in range(H_C):
        sl = slice(h * dh, (h + 1) * dh)
        s = lax.dot_general(q_ref[:, sl], k_ref[:, sl], NT_DIMS,
                            preferred_element_type=F32) * (dh ** -0.5)
        p = jnp.exp(s - jnp.max(s, axis=1, keepdims=True))
        p = p / jnp.sum(p, axis=1, keepdims=True)
        o = jnp.dot(p.astype(BF16), v_ref[:, sl], preferred_element_type=F32)
        o_ref[:, sl] = (o * _silu(gate_ref[:, sl])).astype(o_ref.dtype)


def _mem_prompt(cq, mk, mv, gate, b, l, n_mem, tl, gate_col0):
    d_c = cq.shape[1]
    nl = l // tl
    tok = pl.BlockSpec((tl, d_c), lambda i, j: (i * nl + j, 0))
    mem = pl.BlockSpec((n_mem, d_c), lambda i, j: (i, 0))
    return pl.pallas_call(
        functools.partial(_mem_prompt_kernel, dh=d_c // H_C),
        grid=(b, nl),
        in_specs=[tok, mem, mem,
                  pl.BlockSpec((tl, d_c), lambda i, j: (i * nl + j, gate_col0 // d_c))],
        out_specs=tok, out_shape=jax.ShapeDtypeStruct((b * l, d_c), BF16),
        compiler_params=_params(("parallel", "parallel")), name="memory_attn_prompt",
    )(cq, mk, mv, gate)


def _merge_kernel(x_ref, hm_ref, hd_ref, hc_ref, w_ref, g_ref, b_ref, y_ref, *, alpha):
    d_m = hm_ref.shape[1]
    d_d = hd_ref.shape[1]
    acc = jnp.dot(hm_ref[...], w_ref[0:d_m, :], preferred_element_type=F32)
    acc += jnp.dot(hd_ref[...], w_ref[d_m:d_m + d_d, :], preferred_element_type=F32)
    acc += jnp.dot(hc_ref[...], w_ref[d_m + d_d:, :], preferred_element_type=F32)
    y = alpha * x_ref[...] + acc
    mu = jnp.mean(y, axis=1, keepdims=True)
    var = jnp.mean(jnp.square(y - mu), axis=1, keepdims=True)
    y_ref[...] = (y - mu) * lax.rsqrt(var + EPS) * g_ref[...] + b_ref[...]


def _merge(x2d, h_m, h_d, h_c, w_out, ln_g, ln_b, tm, alpha):
    m, d_model = x2d.shape

    def row(n):
        return pl.BlockSpec((tm, n), lambda i: (i, 0))

    def whole(a):
        return pl.BlockSpec(a.shape, lambda i: (0, 0))

    return pl.pallas_call(
        functools.partial(_merge_kernel, alpha=alpha), grid=(m // tm,),
        in_specs=[row(d_model), row(h_m.shape[1]), row(h_d.shape[1]), row(h_c.shape[1]),
                  whole(w_out), whole(ln_g), whole(ln_b)],
        out_specs=row(d_model), out_shape=jax.ShapeDtypeStruct((m, d_model), F32),
        compiler_params=_params(("parallel",)), name="output_projection",
    )(x2d, h_m, h_d, h_c, w_out, ln_g, ln_b)


def _mlstm_step_kernel(q_ref, k_ref, v_ref, o_ref, gate_ref, ig_ref, fg_ref, m_ref, n_ref, c_ref,
                       g_ref, h_ref, cn_ref, nn_ref, mn_ref, *, dh):
    tb = q_ref.shape[0]
    n_rows = 2 * SUBLANES
    r_idx = lax.broadcasted_iota(jnp.int32, (n_rows, dh), 0)

    def split_rows(row, order):
        hi = row.astype(BF16).astype(F32)
        parts = {"hi": hi, "lo": row - hi}
        out = jnp.zeros((n_rows, dh), F32)
        for i, name in enumerate(order):
            out = jnp.where(r_idx == i, parts[name], out)
        return out.astype(BF16)

    for b in range(tb):
        for h in range(H_M):
            sl = slice(h * dh, (h + 1) * dh)
            q = q_ref[b, :, sl]
            k = k_ref[b, :, sl]
            v = v_ref[b, :, sl]
            ig = ig_ref[b, h]
            lf = _log_sigmoid(fg_ref[b, h])
            m_prev = m_ref[b, h]
            n_old = n_ref[b, h]
            c_old = c_ref[b, h]

            inter = lf + m_prev
            mt = jnp.maximum(ig, inter)
            w_in = jnp.exp(ig - mt)
            w_st = jnp.exp(inter - mt)
            s = jnp.sum(q * k, axis=1, keepdims=True) * w_in
            qc2 = jnp.dot(split_rows(q, ("hi", "lo")), c_old.astype(BF16),
                          preferred_element_type=F32)
            qc = qc2[0:1, :] + qc2[1:2, :]
            outer = lax.dot_general(split_rows(k, ("hi", "hi", "lo")),
                                    split_rows(v, ("hi", "lo", "hi")), TN_DIMS,
                                    preferred_element_type=F32)
            num = s * v + w_st * qc
            den = s + w_st * jnp.sum(q * n_old, axis=1, keepdims=True)
            hc = num / jnp.maximum(jnp.abs(den), jnp.exp(-mt))

            cn_ref[b, h] = w_st * c_old + w_in * outer
            nn_ref[b, h] = w_st * n_old + w_in * k
            mn_ref[b, h] = mt

            mu = jnp.mean(hc, axis=1, keepdims=True)
            var = jnp.mean(jnp.square(hc - mu), axis=1, keepdims=True)
            hn = (hc - mu) * lax.rsqrt(var + EPS) * g_ref[:, sl]
            hn = hn * jax.nn.sigmoid(o_ref[b, :, sl])
            h_ref[b, :, sl] = (hn * _silu(gate_ref[b, :, sl])).astype(h_ref.dtype)


def _mlstm_step(q, k, v, o_pre, gate, ig, fg, m0, n0, c0, g_norm, tb):
    bd, _, d_m = q.shape
    dh = d_m // H_M

    def blk(shape):
        nd = len(shape)
        return pl.BlockSpec((tb,) + tuple(shape), lambda i: (i,) + (0,) * nd)

    tok = blk((1, d_m))
    one = blk((H_M, 1, 1))
    in_specs = [tok, tok, tok, tok, tok, one, one, one, blk((H_M, 1, dh)), blk((H_M, dh, dh)),
                pl.BlockSpec((1, d_m), lambda i: (0, 0))]
    out_shape = (jax.ShapeDtypeStruct((bd, 1, d_m), BF16),
                 jax.ShapeDtypeStruct((bd, H_M, dh, dh), F32),
                 jax.ShapeDtypeStruct((bd, H_M, 1, dh), F32),
                 jax.ShapeDtypeStruct((bd, H_M, 1, 1), F32))
    out_specs = (tok, blk((H_M, dh, dh)), blk((H_M, 1, dh)), one)
    return pl.pallas_call(
        functools.partial(_mlstm_step_kernel, dh=dh), grid=(bd // tb,),
        in_specs=in_specs, out_specs=out_specs, out_shape=out_shape,
        compiler_params=_params(("parallel",)), name="mlstm_step",
    )(q, k, v, o_pre, gate, ig, fg, m0, n0, c0, g_norm)


def _head_rows(row, n_heads, reps):
    dh = row.shape[1] // n_heads
    heads = [row[:, h * dh:(h + 1) * dh] for h in range(n_heads)]
    return jnp.concatenate(heads * reps, axis=0)


def _own_head(n_rows, n_cols, n_heads):
    r = lax.broadcasted_iota(jnp.int32, (n_rows, n_cols), 0)
    c = lax.broadcasted_iota(jnp.int32, (n_rows, n_cols), 1)
    return (c % n_heads) == (r % n_heads)


def _diff_decode_kernel(pt_ref, lq1_ref, lk1_ref, lq2_ref, lk2_ref, q_ref, kn_ref, vn_ref,
                        gate_ref, g_ref, *rest, n_pages, lam_init):
    del pt_ref
    k_refs = rest[:n_pages]
    v_refs = rest[n_pages:2 * n_pages]
    o_ref = rest[2 * n_pages]
    n_rows, dv_d = k_refs[0].shape
    dqk = dv_d // 2
    lam = _lambda(lq1_ref, lk1_ref, lq2_ref, lk2_ref, lam_init)

    q8 = _head_rows(q_ref[0].astype(F32), H_D, 2)
    r_idx = lax.broadcasted_iota(jnp.int32, q8.shape, 0)
    l_idx = lax.broadcasted_iota(jnp.int32, q8.shape, 1)
    q8 = jnp.where((l_idx // dqk) == (r_idx // H_D), q8, 0.0)
    q8b = q8.astype(BF16)
    own = _own_head(2 * H_D, n_rows, H_D)

    scores = [jnp.where(own, lax.dot_general(q8b, k_refs[j][...].astype(BF16), NT_DIMS,
                                             preferred_element_type=F32), -jnp.inf)
              for j in range(n_pages)]
    kn4 = kn_ref[0]
    vn4 = vn_ref[0]
    s_new = jnp.sum(q8 * jnp.concatenate([kn4, kn4], axis=0), axis=1, keepdims=True)
    mx = s_new
    for s in scores:
        mx = jnp.maximum(mx, jnp.max(s, axis=1, keepdims=True))
    probs = [jnp.exp(s - mx) for s in scores]
    p_new = jnp.exp(s_new - mx)
    total = p_new
    for p in probs:
        total = total + jnp.sum(p, axis=1, keepdims=True)
    inv = 1.0 / total

    def combine(p):
        pn = p * inv
        return pn - lam * pltpu.roll(pn, SUBLANES - H_D, 0)

    acc = (combine(jnp.broadcast_to(p_new, (SUBLANES, LANES)))[:, 0:1]
           * jnp.concatenate([vn4, vn4], axis=0))
    for j in range(n_pages):
        acc = acc + jnp.dot(combine(probs[j]).astype(BF16), v_refs[j][...].astype(BF16),
                            preferred_element_type=F32)
    for h in range(H_D):
        sl = slice(h * dv_d, (h + 1) * dv_d)
        o = acc[h:h + 1, :]
        o = o * lax.rsqrt(jnp.mean(o * o, axis=1, keepdims=True) + EPS) * g_ref[...]
        o_ref[0, :, sl] = (o * (1.0 - lam_init) * _silu(gate_ref[0, :, sl])).astype(o_ref.dtype)


def _diff_decode(page_table, lams, dq, dk_new, dv_new, gate, g_norm, cache_k, cache_v,
                 lam_init, gate_col0):
    bd, _, d_d = dq.shape
    n_pages = page_table.shape[1]
    _, n_rows, dv_d = cache_k.shape
    lam_spec = pl.BlockSpec(lams[0].shape, lambda i, pt: (0, 0))
    tok = pl.BlockSpec((1, 1, d_d), lambda i, pt: (i, 0, 0))
    page_specs = [pl.BlockSpec((None, n_rows, dv_d), lambda i, pt, j=j: (pt[i, j], 0, 0))
                  for j in range(n_pages)]
    new_tok = pl.BlockSpec((1, H_D, dv_d), lambda i, pt: (i, 0, 0))
    in_specs = ([lam_spec] * 4 + [tok, new_tok, new_tok,
                                  pl.BlockSpec((1, 1, d_d), lambda i, pt: (i, 0, gate_col0 // d_d)),
                                  pl.BlockSpec((1, dv_d), lambda i, pt: (0, 0))]
                + page_specs + page_specs)
    grid_spec = pltpu.PrefetchScalarGridSpec(
        num_scalar_prefetch=1, grid=(bd,), in_specs=in_specs, out_specs=tok)
    return pl.pallas_call(
        functools.partial(_diff_decode_kernel, n_pages=n_pages, lam_init=lam_init),
        grid_spec=grid_spec, out_shape=jax.ShapeDtypeStruct((bd, 1, d_d), BF16),
        compiler_params=_params(("parallel",)), name="diff_attn_decode",
    )(page_table, *lams, dq, dk_new, dv_new, gate, g_norm,
      *([cache_k] * n_pages), *([cache_v] * n_pages))


def _mem_decode_kernel(q_ref, k_ref, v_ref, gate_ref, o_ref):
    tb, n_rows, dh = k_ref.shape
    own = _own_head(SUBLANES, n_rows, H_C)
    for b in range(tb):
        q8 = _head_rows(q_ref[b].astype(F32), H_C, SUBLANES // H_C).astype(BF16)
        s = lax.dot_general(q8, k_ref[b].astype(BF16), NT_DIMS,
                            preferred_element_type=F32) * (dh ** -0.5)
        s = jnp.where(own, s, -jnp.inf)
        p = jnp.exp(s - jnp.max(s, axis=1, keepdims=True))
        p = p / jnp.sum(p, axis=1, keepdims=True)
        acc = jnp.dot(p.astype(BF16), v_ref[b].astype(BF16), preferred_element_type=F32)
        for h in range(H_C):
            sl = slice(h * dh, (h + 1) * dh)
            o_ref[b, :, sl] = (acc[h:h + 1, :] * _silu(gate_ref[b, :, sl])).astype(o_ref.dtype)


def _mem_decode(cq, mem_k, mem_v, gate, gate_col0, tb):
    bd, _, d_c = cq.shape
    _, n_rows, dh = mem_k.shape
    tok = pl.BlockSpec((tb, 1, d_c), lambda i: (i, 0, 0))
    mem = pl.BlockSpec((tb, n_rows, dh), lambda i: (i, 0, 0))
    return pl.pallas_call(
        _mem_decode_kernel, grid=(bd // tb,),
        in_specs=[tok, mem, mem, pl.BlockSpec((tb, 1, d_c), lambda i: (i, 0, gate_col0 // d_c))],
        out_specs=tok, out_shape=jax.ShapeDtypeStruct((bd, 1, d_c), BF16),
        compiler_params=_params(("parallel",)), name="memory_attn_decode",
    )(cq, mem_k, mem_v, gate)


def _rope_tables(pos, n_rows, dqk):
    half = dqk // 2
    inv_freq = ROPE_THETA ** (-jnp.arange(half, dtype=F32) / half)
    ang = pos.astype(F32)[:, None] * inv_freq[None, :]
    cos = jnp.cos(ang)
    sin = jnp.sin(ang)
    reps = LANES // dqk
    cos_t = jnp.tile(jnp.concatenate([cos, cos], axis=1), (1, reps))
    sin_t = jnp.tile(jnp.concatenate([-sin, sin], axis=1), (1, reps))
    return (jnp.broadcast_to(cos_t, (n_rows, LANES)), jnp.broadcast_to(sin_t, (n_rows, LANES)))


def _split_weights(w_in_l, b_if_l, d_m, d_d, d_mix):
    g0 = 4 * d_m
    d0 = g0 + N_GATE
    wm = w_in_l[:, :g0].astype(BF16)
    w_gates = w_in_l[:, g0:d0]
    wg = jnp.pad(w_gates, ((0, 0), (0, LANES - N_GATE))).astype(BF16)
    wgt = w_gates.T.astype(BF16)
    wd = w_in_l[:, d0:d0 + 4 * d_d].astype(BF16)
    wgate = w_in_l[:, d0 + 4 * d_d:].astype(BF16)
    bcol = jnp.pad(b_if_l, (0, LANES - N_GATE)).reshape(1, LANES).astype(F32)
    brow = jnp.broadcast_to(b_if_l.astype(F32)[:, None], (N_GATE, LANES))
    return wm, wg, wgt, wd, wgate, bcol, brow


def kernel(x_prompt, x_sample, state_mlstm_c, state_mlstm_n, state_mlstm_m, cache_diff_k, cache_diff_v,
           cache_mem_k, cache_mem_v, page_table, mem_prompt, w_in, b_if, w_mlstm_norm,
           lambda_q1, lambda_k1, lambda_q2, lambda_k2, w_diff_norm, w_mem_kv, w_out, ln_g, ln_b):
    depth = w_in.shape[0]
    assert depth == 1, "single-layer trunk"
    bp, lp, d_model = x_prompt.shape
    bd, ls, _ = x_sample.shape
    assert ls == 1
    d_mix = w_out.shape[1]
    d_m = d_mix // 2
    d_d = d_mix // 4
    d_c = d_mix // 4
    dh_m = d_m // H_M
    dv_d = d_d // H_D
    dqk = dv_d // 2
    assert dv_d == LANES and d_c // H_C == LANES, "one attention head per lane tile"
    n_mem = mem_prompt.shape[1]
    n_pool, page = cache_diff_k.shape[1], cache_diff_k.shape[2]
    past_len = page_table.shape[1] * page
    alpha = (2 * depth) ** 0.25
    layer = 0
    lam_init = 0.8 - 0.6 * math.exp(-0.3 * layer)

    wts = _split_weights(w_in[layer], b_if[layer], d_m, d_d, d_mix)
    lams = tuple(a[layer].reshape(1, dqk).astype(F32)
                 for a in (lambda_q1, lambda_k1, lambda_q2, lambda_k2))
    g_m = w_mlstm_norm[layer].reshape(1, d_m).astype(F32)
    g_d = w_diff_norm[layer].reshape(1, dv_d).astype(F32)
    w_out_b = w_out[layer].astype(BF16)
    g_ln = ln_g[layer].reshape(1, d_model).astype(F32)
    b_ln = ln_b[layer].reshape(1, d_model).astype(F32)
    gate_d0 = d_m
    gate_c0 = d_m + d_d

    mp = bp * lp
    xp2d = x_prompt.reshape(mp, d_model)
    cos_p, sin_p = _rope_tables(jnp.arange(lp), lp, dqk)
    (q_m, k_m, v_m, o_pre, gcol, grow, dq, dk_f, dk_b, dv_f, dv_t, cq, gate) = _input_projection(
        xp2d, wts, cos_p, sin_p, tm=256, qkv_dtype=BF16)
    h_m, c_p, n_p, m_p = _mlstm_prompt(q_m, k_m, v_m, o_pre, gate, grow, gcol, g_m, bp, lp)
    h_d = _diff_prompt(lams, dq, dk_b, dv_t, gate, g_d, bp, lp, lam_init, gate_d0)
    mk_f, mk_b, mv_f, mv_b = _mem_kv(mem_prompt.reshape(bp * n_mem, d_model).astype(BF16),
                                     w_mem_kv[layer].astype(BF16), tm=512)
    h_c = _mem_prompt(cq, mk_b, mv_b, gate, bp, lp, n_mem, 512, gate_c0)
    y_p = _merge(xp2d, h_m, h_d, h_c, w_out_b, g_ln, b_ln, 512, alpha)

    xs2d = x_sample.reshape(bd, d_model)
    cos_s, sin_s = _rope_tables(jnp.full((1,), past_len), bd, dqk)
    (qs, ks, vs, os_pre, gcol_s, _, dq_s, dk_s, _, dv_s, _, cq_s, gate_s) = _input_projection(
        xs2d, wts, cos_s, sin_s, tm=bd, qkv_dtype=F32)

    def tok3(a):
        return a.reshape(bd, 1, a.shape[1])

    ig_s = gcol_s[:, 0:H_M].reshape(bd, H_M, 1, 1)
    fg_s = gcol_s[:, H_M:N_GATE].reshape(bd, H_M, 1, 1)
    hs_m, c_s, n_s, m_s = _mlstm_step(
        tok3(qs), tok3(ks), tok3(vs), tok3(os_pre), tok3(gate_s), ig_s, fg_s,
        state_mlstm_m[layer].astype(F32).reshape(bd, H_M, 1, 1),
        state_mlstm_n[layer].astype(F32).reshape(bd, H_M, 1, dh_m),
        state_mlstm_c[layer].astype(F32), g_m, tb=4)
    hs_d = _diff_decode(page_table, lams, tok3(dq_s), dk_s.reshape(bd, H_D, dv_d),
                        dv_s.reshape(bd, H_D, dv_d), tok3(gate_s), g_d,
                        cache_diff_k[layer].reshape(n_pool, page * H_D, dv_d),
                        cache_diff_v[layer].reshape(n_pool, page * H_D, dv_d), lam_init, gate_d0)
    hs_c = _mem_decode(tok3(cq_s), cache_mem_k[layer].reshape(bd, n_mem * H_C, d_c // H_C),
                       cache_mem_v[layer].reshape(bd, n_mem * H_C, d_c // H_C), tok3(gate_s),
                       gate_c0, tb=4)
    y_s = _merge(xs2d, hs_m.reshape(bd, d_m), hs_d.reshape(bd, d_d), hs_c.reshape(bd, d_c),
                 w_out_b, g_ln, b_ln, bd, alpha)

    return (y_p.reshape(bp, lp, d_model), y_s.reshape(bd, ls, d_model),
            c_p[None], n_p[None], m_p[:, 0, 0:H_M][None],
            dk_f.reshape(1, bp, lp, H_D, dv_d), dv_f.reshape(1, bp, lp, H_D, dv_d),
            mk_f.reshape(1, bp, n_mem, H_C, d_c // H_C), mv_f.reshape(1, bp, n_mem, H_C, d_c // H_C),
            c_s[None], n_s.reshape(1, bd, H_M, dh_m), m_s.reshape(1, bd, H_M),
            dk_s.reshape(1, bd, ls, H_D, dv_d), dv_s.reshape(1, bd, ls, H_D, dv_d))
```

```python
import functools
import math

import jax
import jax.numpy as jnp
from jax import lax
from jax.experimental import pallas as pl
from jax.experimental.pallas import tpu as pltpu

F32 = jnp.float32
BF16 = jnp.bfloat16

H_M = 4
H_D = 4
H_C = 4
N_GATE = 2 * H_M
CHUNK = 128
ROPE_THETA = 10000.0
EPS = 1e-5
LANES = 128
SUBLANES = 8
VMEM_LIMIT = 56 * 1024 * 1024

NT_DIMS = (((1,), (1,)), ((), ()))
TN_DIMS = (((0,), (0,)), ((), ()))


def _params(semantics):
    return pltpu.CompilerParams(dimension_semantics=semantics, vmem_limit_bytes=VMEM_LIMIT)


def _log_sigmoid(x):
    return jnp.minimum(x, 0.0) - jnp.log1p(jnp.exp(-jnp.abs(x)))


def _silu(x):
    return x * jax.nn.sigmoid(x)


def _split_rows(row, order):
    n_rows = 2 * SUBLANES
    r_idx = lax.broadcasted_iota(jnp.int32, (n_rows, row.shape[1]), 0)
    hi = row.astype(BF16).astype(F32)
    parts = {"hi": hi, "lo": row - hi}
    out = jnp.zeros((n_rows, row.shape[1]), F32)
    for i, name in enumerate(order):
        out = jnp.where(r_idx == i, parts[name], out)
    return out.astype(BF16)


def _lambda(lq1_ref, lk1_ref, lq2_ref, lk2_ref, lam_init):
    a = jnp.sum(lq1_ref[...] * lk1_ref[...], axis=1, keepdims=True)
    b = jnp.sum(lq2_ref[...] * lk2_ref[...], axis=1, keepdims=True)
    return jnp.exp(a) - jnp.exp(b) + lam_init


def _inproj_kernel(x_ref, wqt_ref, wvt_ref, wm_ref, wg_ref, wgt_ref, wd_ref, wgate_ref,
                   bcol_ref, brow_ref,
                   cos_ref, sin_ref,
                   q_ref, k_ref, v_ref, o_ref, gcol_ref, grow_ref,
                   dq_ref, dkf_ref, dkb_ref, dvf_ref, dvt_ref, cq_ref, gate_ref,
                   *, d_m, d_d, dh_m, dqk, transposed_qv):
    x = x_ref[...].astype(BF16)

    def mm(w):
        return jnp.dot(x, w, preferred_element_type=F32)

    def mm_t(a, b):
        return lax.dot_general(a, b, NT_DIMS, preferred_element_type=F32)

    if transposed_qv:
        q_ref[0] = mm_t(wqt_ref[...], x).astype(q_ref.dtype)
        v_ref[0] = mm_t(wvt_ref[...], x).astype(v_ref.dtype)
    else:
        q_ref[...] = mm_t(x, wqt_ref[...]).astype(q_ref.dtype)
        v_ref[...] = mm_t(x, wvt_ref[...]).astype(v_ref.dtype)
    k_ref[...] = (mm(wm_ref[:, 0:d_m]) * (dh_m ** -0.5)).astype(k_ref.dtype)
    o_ref[...] = mm(wm_ref[:, d_m:2 * d_m])
    gcol_ref[...] = mm(wg_ref[...]) + bcol_ref[...]
    grow_ref[...] = (lax.dot_general(wgt_ref[...], x, NT_DIMS, preferred_element_type=F32)
                     + brow_ref[:, 0:1])

    cos = cos_ref[...]
    sin = sin_ref[...]
    lane = lax.broadcasted_iota(jnp.int32, cos.shape, 1)
    first_half = (lane % dqk) < (dqk // 2)

    def rope(t):
        swapped = jnp.where(first_half, pltpu.roll(t, LANES - dqk // 2, 1),
                            pltpu.roll(t, dqk // 2, 1))
        return t * cos + swapped * sin

    tm = x.shape[0]
    dq = mm(wd_ref[:, 0:d_d])
    dk = mm(wd_ref[:, d_d:2 * d_d])
    dv = mm(wd_ref[:, 2 * d_d:3 * d_d])
    dvt_ref[0] = dv.T.astype(dvt_ref.dtype)
    for h in range(H_D):
        sl = slice(h * LANES, (h + 1) * LANES)
        dq_ref[:, sl] = (rope(dq[:, sl]) * (dqk ** -0.5)).astype(dq_ref.dtype)
        dk_h = rope(dk[:, sl])
        dkb_ref[:, sl] = dk_h.astype(dkb_ref.dtype)
        dkf_ref[pl.ds(h, tm, stride=H_D), :] = dk_h
        dvf_ref[pl.ds(h, tm, stride=H_D), :] = dv[:, sl]
    cq_ref[...] = mm(wd_ref[:, 3 * d_d:4 * d_d]).astype(cq_ref.dtype)
    gate_ref[...] = mm(wgate_ref[...])


def _input_projection(x2d, wts, cos_tab, sin_tab, tm, qkv_dtype, transposed_qv):
    m, d_model = x2d.shape
    wqt, wvt, wm, wg, wgt, wd, wgate, bcol, brow = wts
    d_m = wqt.shape[0]
    d_d = wd.shape[1] // 4
    d_mix = wgate.shape[1]
    dh_m = d_m // H_M
    dqk = d_d // H_D // 2
    grid = (m // tm,)

    def row_spec(n):
        return pl.BlockSpec((tm, n), lambda i: (i, 0))

    def whole(a):
        return pl.BlockSpec(a.shape, lambda i: (0,) * a.ndim, pipeline_mode=pl.Buffered(1))

    if transposed_qv:
        qv_shape = jax.ShapeDtypeStruct((m // tm, d_m, tm), qkv_dtype)
        qv_spec = pl.BlockSpec((1, d_m, tm), lambda i: (i, 0, 0))
    else:
        qv_shape = jax.ShapeDtypeStruct((m, d_m), qkv_dtype)
        qv_spec = row_spec(d_m)
    out_shape = (
        qv_shape,
        jax.ShapeDtypeStruct((m, d_m), qkv_dtype),
        qv_shape,
        jax.ShapeDtypeStruct((m, d_m), F32),
        jax.ShapeDtypeStruct((m, LANES), F32),
        jax.ShapeDtypeStruct((N_GATE, m), F32),
        jax.ShapeDtypeStruct((m, d_d), BF16),
        jax.ShapeDtypeStruct((m * H_D, LANES), F32),
        jax.ShapeDtypeStruct((m, d_d), BF16),
        jax.ShapeDtypeStruct((m * H_D, LANES), F32),
        jax.ShapeDtypeStruct((m // tm, d_d, tm), BF16),
        jax.ShapeDtypeStruct((m, d_d), BF16),
        jax.ShapeDtypeStruct((m, d_mix), F32),
    )
    head_rows = pl.BlockSpec((tm * H_D, LANES), lambda i: (i, 0))
    out_specs = (
        qv_spec, row_spec(d_m), qv_spec, row_spec(d_m), row_spec(LANES),
        pl.BlockSpec((N_GATE, tm), lambda i: (0, i)),
        row_spec(d_d), head_rows, row_spec(d_d), head_rows,
        pl.BlockSpec((1, d_d, tm), lambda i: (i, 0, 0)), row_spec(d_d),
        row_spec(d_mix),
    )
    n_tab = cos_tab.shape[0] // tm
    tab_spec = pl.BlockSpec((tm, LANES), lambda i: (i % n_tab, 0))
    in_specs = [row_spec(d_model), whole(wqt), whole(wvt), whole(wm), whole(wg), whole(wgt),
                whole(wd), whole(wgate), whole(bcol), whole(brow), tab_spec, tab_spec]
    return pl.pallas_call(
        functools.partial(_inproj_kernel, d_m=d_m, d_d=d_d, dh_m=dh_m, dqk=dqk,
                          transposed_qv=transposed_qv),
        grid=grid, in_specs=in_specs, out_specs=out_specs, out_shape=out_shape,
        compiler_params=_params(("parallel",)), name="input_projection",
    )(x2d, wqt, wvt, wm, wg, wgt, wd, wgate, bcol, brow, cos_tab, sin_tab)


def _mlstm_prompt_kernel(qt_ref, k_ref, vt_ref, o_ref, gate_ref, grow_ref, gcol_ref, gb_ref,
                         h_ref, c_ref, n_ref, m_ref, ct_sc, *, dh):
    j = pl.program_id(1)

    @pl.when(j == 0)
    def _():
        ct_sc[...] = jnp.zeros_like(ct_sc)
        n_ref[...] = jnp.zeros_like(n_ref)
        m_ref[...] = jnp.zeros_like(m_ref)

    c = k_ref.shape[0]
    s_idx = lax.broadcasted_iota(jnp.int32, (c, c), 0)
    t_idx = lax.broadcasted_iota(jnp.int32, (c, c), 1)
    causal = s_idx <= t_idx
    lower = t_idx <= s_idx

    heads = []
    for h in range(H_M):
        sl = slice(h * dh, (h + 1) * dh)
        qt = qt_ref[sl, :]
        k = k_ref[:, sl]
        ig_row = grow_ref[h:h + 1, :]
        lf_row = _log_sigmoid(grow_ref[H_M + h:H_M + h + 1, :])
        ig_col = gcol_ref[:, h:h + 1]
        lf_col = _log_sigmoid(gcol_ref[:, H_M + h:H_M + h + 1])
        bcum_col = jnp.sum(jnp.where(lower, lf_row, 0.0), axis=1, keepdims=True)
        bcum_row = jnp.sum(jnp.where(causal, lf_col, 0.0), axis=0, keepdims=True)
        m_prev = m_ref[:, h:h + 1]
        dlog = jnp.where(causal, bcum_row - bcum_col + ig_col, -jnp.inf)
        inter = bcum_row + m_prev
        mt = jnp.maximum(jnp.max(dlog, axis=0, keepdims=True), inter)
        w_inter = jnp.exp(inter - mt)
        st = jnp.dot(k, qt, preferred_element_type=F32) * jnp.exp(dlog - mt)
        ct_old = ct_sc[h]
        n_old = n_ref[h:h + 1, :]
        cq = jnp.dot(ct_old.astype(BF16), qt, preferred_element_type=F32)
        nq2 = jnp.dot(_split_rows(n_old, ("hi", "lo")), qt, preferred_element_type=F32)
        m_new = mt[:, c - 1:c]
        b_last = bcum_row[:, c - 1:c]
        w_end = jnp.exp(b_last - bcum_row + ig_row - m_new)
        decay = jnp.exp(b_last + m_prev - m_new)
        heads.append(dict(sl=sl, k=k, mt=mt, w_inter=w_inter, st=st, ct_old=ct_old, n_old=n_old,
                          cq=cq, nq=nq2[0:1, :] + nq2[1:2, :], m_new=m_new, w_end=w_end,
                          decay=decay))

    for h, d in enumerate(heads):
        vt = vt_ref[d["sl"], :]
        num = jnp.dot(vt, d["st"].astype(BF16), preferred_element_type=F32) + d["w_inter"] * d["cq"]
        den = jnp.sum(d["st"], axis=0, keepdims=True) + d["w_inter"] * d["nq"]
        d["hc"] = num * (1.0 / jnp.maximum(jnp.abs(den), jnp.exp(-d["mt"])))
        vtw = (vt.astype(F32) * d["w_end"]).astype(BF16)
        ct_sc[h] = d["decay"] * d["ct_old"] + jnp.dot(vtw, d["k"], preferred_element_type=F32)
        kw2 = jnp.dot(_split_rows(d["w_end"], ("hi", "lo")), d["k"], preferred_element_type=F32)
        n_ref[h:h + 1, :] = d["decay"] * d["n_old"] + kw2[0:1, :] + kw2[1:2, :]
        m_ref[:, h:h + 1] = d["m_new"]

    for h, d in enumerate(heads):
        sl = d["sl"]
        hc = d["hc"]
        mu = jnp.mean(hc, axis=0, keepdims=True)
        var = jnp.mean(jnp.square(hc - mu), axis=0, keepdims=True)
        hn = ((hc - mu) * lax.rsqrt(var + EPS) * gb_ref[sl, :]).T
        hn = hn * jax.nn.sigmoid(o_ref[:, sl])
        h_ref[:, sl] = (hn * _silu(gate_ref[:, sl])).astype(h_ref.dtype)

    @pl.when(j == pl.num_programs(1) - 1)
    def _():
        for h in range(H_M):
            c_ref[h] = ct_sc[h].T


def _mlstm_prompt(q_t, k, v_t, o_pre, gate, grow, gcol, g_norm, b, l):
    d_m = k.shape[1]
    dh = d_m // H_M
    nc = l // CHUNK
    tm = q_t.shape[2]
    per = tm // CHUNK
    tok = pl.BlockSpec((CHUNK, d_m), lambda i, j: (i * nc + j, 0))
    tok_t = pl.BlockSpec((None, d_m, CHUNK), lambda i, j: ((i * nc + j) // per, 0, j % per))
    g_bcast = jnp.broadcast_to(g_norm.reshape(d_m, 1), (d_m, CHUNK))
    in_specs = [tok_t, tok, tok_t, tok, tok,
                pl.BlockSpec((N_GATE, CHUNK), lambda i, j: (0, i * nc + j)),
                pl.BlockSpec((CHUNK, LANES), lambda i, j: (i * nc + j, 0)),
                pl.BlockSpec((d_m, CHUNK), lambda i, j: (0, 0))]
    out_shape = (jax.ShapeDtypeStruct((b * l, d_m), BF16),
                 jax.ShapeDtypeStruct((b, H_M, dh, dh), F32),
                 jax.ShapeDtypeStruct((b, H_M, dh), F32),
                 jax.ShapeDtypeStruct((b, 1, LANES), F32))
    out_specs = (tok,
                 pl.BlockSpec((None, H_M, dh, dh), lambda i, j: (i, 0, 0, 0)),
                 pl.BlockSpec((None, H_M, dh), lambda i, j: (i, 0, 0)),
                 pl.BlockSpec((None, 1, LANES), lambda i, j: (i, 0, 0)))
    return pl.pallas_call(
        functools.partial(_mlstm_prompt_kernel, dh=dh),
        grid=(b, nc), in_specs=in_specs, out_specs=out_specs, out_shape=out_shape,
        scratch_shapes=[pltpu.VMEM((H_M, dh, dh), F32)],
        compiler_params=_params(("parallel", "arbitrary")), name="mlstm_prompt",
    )(q_t, k, v_t, o_pre, gate, grow, gcol, g_bcast)


def _diff_prompt_kernel(lq1_ref, lk1_ref, lq2_ref, lk2_ref, q_ref, k_ref, vt_ref, gate_ref, g_ref,
                        o_ref, m_sc, l_sc, acc_sc, *, lam_init, dqk):
    tq = q_ref.shape[0]
    dv = 2 * dqk
    qi = pl.program_id(1)
    lam = _lambda(lq1_ref, lk1_ref, lq2_ref, lk2_ref, lam_init)
    lane = lax.broadcasted_iota(jnp.int32, (tq, dv), 1)
    key_i = lax.broadcasted_iota(jnp.int32, (tq, 2 * tq), 0)
    qry_i = lax.broadcasted_iota(jnp.int32, (tq, 2 * tq), 1)
    visible = key_i <= jnp.where(qry_i >= tq, qry_i - tq, qry_i)
    q_both = []
    for h in range(H_D):
        q = q_ref[:, h * dv:(h + 1) * dv]
        zero = jnp.zeros_like(q)
        q_both.append(jnp.concatenate([jnp.where(lane < dqk, q, zero),
                                       jnp.where(lane >= dqk, q, zero)], axis=0))
    m_sc[...] = jnp.full(m_sc.shape, -jnp.inf, F32)
    l_sc[...] = jnp.zeros_like(l_sc)
    acc_sc[...] = jnp.zeros_like(acc_sc)

    def key_span(first_tile, n_tiles, on_diagonal):
        start = pl.multiple_of(first_tile * tq, tq)
        scores = []
        for h in range(H_D):
            hs = slice(h * dv, (h + 1) * dv)
            k = k_ref[pl.ds(start, n_tiles * tq), hs]
            scores.append(lax.dot_general(k, q_both[h], NT_DIMS,
                                          preferred_element_type=F32))
        for h in range(H_D):
            hs = slice(h * dv, (h + 1) * dv)
            st = scores[h]
            if on_diagonal:
                st = jnp.where(visible, st, -jnp.inf)
            m_old = m_sc[h]
            m_new = jnp.maximum(m_old, jnp.max(st, axis=0, keepdims=True))
            alpha = jnp.exp(m_old - m_new)
            pt = jnp.exp(st - m_new)
            l_sc[h] = alpha * l_sc[h] + jnp.sum(pt, axis=0, keepdims=True)
            pt = pt.astype(BF16)
            pv = jnp.dot(vt_ref[first_tile, hs, :], pt[0:tq], preferred_element_type=F32)
            for t in range(1, n_tiles):
                pv += jnp.dot(vt_ref[first_tile + t, hs, :], pt[t * tq:(t + 1) * tq],
                              preferred_element_type=F32)
            acc_sc[h] = alpha * acc_sc[h] + pv
            m_sc[h] = m_new

    def tile_pair(i, carry):
        key_span(2 * i, 2, False)
        return carry

    lax.fori_loop(0, qi // 2, tile_pair, 0)

    @pl.when(qi % 2 == 1)
    def _():
        key_span(qi - 1, 1, False)

    key_span(qi, 1, True)

    for h in range(H_D):
        hs = slice(h * dv, (h + 1) * dv)
        on = acc_sc[h] / l_sc[h]
        o = (on[:, 0:tq] - lam * on[:, tq:2 * tq]).T
        o = o * lax.rsqrt(jnp.mean(o * o, axis=1, keepdims=True) + EPS) * g_ref[...]
        o_ref[:, hs] = (o * (1.0 - lam_init) * _silu(gate_ref[:, hs])).astype(o_ref.dtype)


def _diff_prompt(lams, dq, dk, dv_t, gate, g_norm, b, l, lam_init, gate_col0):
    d_d = dq.shape[1]
    dv_d = d_d // H_D
    tq = dv_t.shape[2]
    nq = l // tq
    lam_spec = pl.BlockSpec(lams[0].shape, lambda i, j: (0, 0))
    tok = pl.BlockSpec((tq, d_d), lambda i, j: (i * nq + j, 0))
    in_specs = [lam_spec] * 4 + [
        tok,
        pl.BlockSpec((l, d_d), lambda i, j: (i, 0)),
        pl.BlockSpec((nq, d_d, tq), lambda i, j: (i, 0, 0)),
        pl.BlockSpec((tq, d_d), lambda i, j: (i * nq + j, gate_col0 // d_d)),
        pl.BlockSpec((1, dv_d), lambda i, j: (0, 0))]
    return pl.pallas_call(
        functools.partial(_diff_prompt_kernel, lam_init=lam_init, dqk=dv_d // 2),
        grid=(b, nq), in_specs=in_specs, out_specs=tok,
        out_shape=jax.ShapeDtypeStruct((b * l, d_d), BF16),
        scratch_shapes=[pltpu.VMEM((H_D, 1, 2 * tq), F32), pltpu.VMEM((H_D, 1, 2 * tq), F32),
                        pltpu.VMEM((H_D, dv_d, 2 * tq), F32)],
        compiler_params=_params(("parallel", "parallel")), name="diff_attn_prompt",
    )(*lams, dq, dk, dv_t, gate, g_norm)


def _mem_kv_kernel(x_ref, w_ref, kf_ref, kb_ref, vf_ref, vb_ref):
    tm, d_c = kb_ref.shape
    dh = d_c // H_C
    x = x_ref[...]
    mk = jnp.dot(x, w_ref[:, 0:d_c], preferred_element_type=F32)
    mv = jnp.dot(x, w_ref[:, d_c:2 * d_c], preferred_element_type=F32)
    kb_ref[...] = mk.astype(kb_ref.dtype)
    vb_ref[...] = mv.astype(vb_ref.dtype)
    for h in range(H_C):
        sl = slice(h * dh, (h + 1) * dh)
        kf_ref[pl.ds(h, tm, stride=H_C), :] = mk[:, sl]
        vf_ref[pl.ds(h, tm, stride=H_C), :] = mv[:, sl]


def _mem_kv(mem2d, w, tm):
    m, d_model = mem2d.shape
    d_c = w.shape[1] // 2
    dh = d_c // H_C
    row = pl.BlockSpec((tm, d_c), lambda i: (i, 0))
    head_rows = pl.BlockSpec((tm * H_C, dh), lambda i: (i, 0))
    return pl.pallas_call(
        _mem_kv_kernel, grid=(m // tm,),
        in_specs=[pl.BlockSpec((tm, d_model), lambda i: (i, 0)),
                  pl.BlockSpec(w.shape, lambda i: (0, 0))],
        out_specs=(head_rows, row, head_rows, row),
        out_shape=(jax.ShapeDtypeStruct((m * H_C, dh), F32), jax.ShapeDtypeStruct((m, d_c), BF16),
                   jax.ShapeDtypeStruct((m * H_C, dh), F32), jax.ShapeDtypeStruct((m, d_c), BF16)),
        compiler_params=_params(("parallel",)), name="memory_kv",
    )(mem2d, w)


def _mem_prompt_kernel(q_ref, k_ref, v_ref, gate_ref, o_ref, *, dh):
    for h in range(H_C):
        sl = slice(h * dh, (h + 1) * dh)
        s = lax.dot_general(q_ref[:, sl], k_ref[:, sl], NT_DIMS,
                            preferred_element_type=F32) * (dh ** -0.5)
        p = jnp.exp(s - jnp.max(s, axis=1, keepdims=True))
        p = p / jnp.sum(p, axis=1, keepdims=True)
        o = jnp.dot(p.astype(BF16), v_ref[:, sl], preferred_element_type=F32)
        o_ref[:, sl] = (o * _silu(gate_ref[:, sl])).astype(o_ref.dtype)


def _mem_prompt(cq, mk, mv, gate, b, l, n_mem, tl, gate_col0):
    d_c = cq.shape[1]
    nl = l // tl
    tok = pl.BlockSpec((tl, d_c), lambda i, j: (i * nl + j, 0))
    mem = pl.BlockSpec((n_mem, d_c), lambda i, j: (i, 0))
    return pl.pallas_call(
        functools.partial(_mem_prompt_kernel, dh=d_c // H_C),
        grid=(b, nl),
        in_specs=[tok, mem, mem,
                  pl.BlockSpec((tl, d_c), lambda i, j: (i * nl + j, gate_col0 // d_c))],
        out_specs=tok, out_shape=jax.ShapeDtypeStruct((b * l, d_c), BF16),
        compiler_params=_params(("parallel", "parallel")), name="memory_attn_prompt",
    )(cq, mk, mv, gate)


def _merge_kernel(x_ref, hm_ref, hd_ref, hc_ref, w_ref, g_ref, b_ref, y_ref, *, alpha):
    d_m = hm_ref.shape[1]
    d_d = hd_ref.shape[1]
    acc = jnp.dot(hm_ref[...], w_ref[0:d_m, :], preferred_element_type=F32)
    acc += jnp.dot(hd_ref[...], w_ref[d_m:d_m + d_d, :], preferred_element_type=F32)
    acc += jnp.dot(hc_ref[...], w_ref[d_m + d_d:, :], preferred_element_type=F32)
    y = alpha * x_ref[...] + acc
    mu = jnp.mean(y, axis=1, keepdims=True)
    var = jnp.mean(jnp.square(y - mu), axis=1, keepdims=True)
    y_ref[...] = (y - mu) * lax.rsqrt(var + EPS) * g_ref[...] + b_ref[...]


def _merge(x2d, h_m, h_d, h_c, w_out, ln_g, ln_b, tm, alpha):
    m, d_model = x2d.shape

    def row(n):
        return pl.BlockSpec((tm, n), lambda i: (i, 0))

    def whole(a):
        return pl.BlockSpec(a.shape, lambda i: (0, 0))

    return pl.pallas_call(
        functools.partial(_merge_kernel, alpha=alpha), grid=(m // tm,),
        in_specs=[row(d_model), row(h_m.shape[1]), row(h_d.shape[1]), row(h_c.shape[1]),
                  whole(w_out), whole(ln_g), whole(ln_b)],
        out_specs=row(d_model), out_shape=jax.ShapeDtypeStruct((m, d_model), F32),
        compiler_params=_params(("parallel",)), name="output_projection",
    )(x2d, h_m, h_d, h_c, w_out, ln_g, ln_b)


def _mlstm_step_kernel(q_ref, k_ref, v_ref, o_ref, gate_ref, ig_ref, fg_ref, m_ref, n_ref, c_ref,
                       g_ref, h_ref, cn_ref, nn_ref, mn_ref, *, dh):
    tb = q_ref.shape[0]
    for b in range(tb):
        for h in range(H_M):
            sl = slice(h * dh, (h + 1) * dh)
            q = q_ref[b, :, sl]
            k = k_ref[b, :, sl]
            v = v_ref[b, :, sl]
            ig = ig_ref[b, h]
            lf = _log_sigmoid(fg_ref[b, h])
            m_prev = m_ref[b, h]
            n_old = n_ref[b, h]
            c_old = c_ref[b, h]

            inter = lf + m_prev
            mt = jnp.maximum(ig, inter)
            w_in = jnp.exp(ig - mt)
            w_st = jnp.exp(inter - mt)
            s = jnp.sum(q * k, axis=1, keepdims=True) * w_in
            qc2 = jnp.dot(_split_rows(q, ("hi", "lo")), c_old.astype(BF16),
                          preferred_element_type=F32)
            qc = qc2[0:1, :] + qc2[1:2, :]
            outer = lax.dot_general(_split_rows(k, ("hi", "hi", "lo")),
                                    _split_rows(v, ("hi", "lo", "hi")), TN_DIMS,
                                    preferred_element_type=F32)
            num = s * v + w_st * qc
            den = s + w_st * jnp.sum(q * n_old, axis=1, keepdims=True)
            hc = num / jnp.maximum(jnp.abs(den), jnp.exp(-mt))

            cn_ref[b, h] = w_st * c_old + w_in * outer
            nn_ref[b, h] = w_st * n_old + w_in * k
            mn_ref[b, h] = mt

            mu = jnp.mean(hc, axis=1, keepdims=True)
            var = jnp.mean(jnp.square(hc - mu), axis=1, keepdims=True)
            hn = (hc - mu) * lax.rsqrt(var + EPS) * g_ref[:, sl]
            hn = hn * jax.nn.sigmoid(o_ref[b, :, sl])
            h_ref[b, :, sl] = (hn * _silu(gate_ref[b, :, sl])).astype(h_ref.dtype)


def _mlstm_step(q, k, v, o_pre, gate, ig, fg, m0, n0, c0, g_norm, tb):
    bd, _, d_m = q.shape
    dh = d_m // H_M

    def blk(shape):
        nd = len(shape)
        return pl.BlockSpec((tb,) + tuple(shape), lambda i: (i,) + (0,) * nd)

    tok = blk((1, d_m))
    one = blk((H_M, 1, 1))
    in_specs = [tok, tok, tok, tok, tok, one, one, one, blk((H_M, 1, dh)), blk((H_M, dh, dh)),
                pl.BlockSpec((1, d_m), lambda i: (0, 0))]
    out_shape = (jax.ShapeDtypeStruct((bd, 1, d_m), BF16),
                 jax.ShapeDtypeStruct((bd, H_M, dh, dh), F32),
                 jax.ShapeDtypeStruct((bd, H_M, 1, dh), F32),
                 jax.ShapeDtypeStruct((bd, H_M, 1, 1), F32))
    out_specs = (tok, blk((H_M, dh, dh)), blk((H_M, 1, dh)), one)
    return pl.pallas_call(
        functools.partial(_mlstm_step_kernel, dh=dh), grid=(bd // tb,),
        in_specs=in_specs, out_specs=out_specs, out_shape=out_shape,
        compiler_params=_params(("parallel",)), name="mlstm_step",
    )(q, k, v, o_pre, gate, ig, fg, m0, n0, c0, g_norm)


def _head_rows(row, n_heads, reps):
    dh = row.shape[1] // n_heads
    heads = [row[:, h * dh:(h + 1) * dh] for h in range(n_heads)]
    return jnp.concatenate(heads * reps, axis=0)


def _own_head(n_rows, n_cols, n_heads):
    r = lax.broadcasted_iota(jnp.int32, (n_rows, n_cols), 0)
    c = lax.broadcasted_iota(jnp.int32, (n_rows, n_cols), 1)
    return (c % n_heads) == (r % n_heads)


def _diff_decode_kernel(pt_ref, lq1_ref, lk1_ref, lq2_ref, lk2_ref, q_ref, kn_ref, vn_ref,
                        gate_ref, g_ref, *rest, n_pages, lam_init):
    del pt_ref
    k_refs = rest[:n_pages]
    v_refs = rest[n_pages:2 * n_pages]
    o_ref = rest[2 * n_pages]
    n_rows, dv_d = k_refs[0].shape
    dqk = dv_d // 2
    lam = _lambda(lq1_ref, lk1_ref, lq2_ref, lk2_ref, lam_init)

    q8 = _head_rows(q_ref[0].astype(F32), H_D, 2)
    r_idx = lax.broadcasted_iota(jnp.int32, q8.shape, 0)
    l_idx = lax.broadcasted_iota(jnp.int32, q8.shape, 1)
    q8 = jnp.where((l_idx // dqk) == (r_idx // H_D), q8, 0.0)
    q8b = q8.astype(BF16)
    own = _own_head(2 * H_D, n_rows, H_D)

    scores = [jnp.where(own, lax.dot_general(q8b, k_refs[j][...].astype(BF16), NT_DIMS,
                                             preferred_element_type=F32), -jnp.inf)
              for j in range(n_pages)]
    kn4 = kn_ref[0]
    vn4 = vn_ref[0]
    s_new = jnp.sum(q8 * jnp.concatenate([kn4, kn4], axis=0), axis=1, keepdims=True)
    mx = s_new
    for s in scores:
        mx = jnp.maximum(mx, jnp.max(s, axis=1, keepdims=True))
    probs = [jnp.exp(s - mx) for s in scores]
    p_new = jnp.exp(s_new - mx)
    total = p_new
    for p in probs:
        total = total + jnp.sum(p, axis=1, keepdims=True)
    inv = 1.0 / total

    def combine(p):
        pn = p * inv
        return pn - lam * pltpu.roll(pn, SUBLANES - H_D, 0)

    acc = (combine(jnp.broadcast_to(p_new, (SUBLANES, LANES)))[:, 0:1]
           * jnp.concatenate([vn4, vn4], axis=0))
    for j in range(n_pages):
        acc = acc + jnp.dot(combine(probs[j]).astype(BF16), v_refs[j][...].astype(BF16),
                            preferred_element_type=F32)
    for h in range(H_D):
        sl = slice(h * dv_d, (h + 1) * dv_d)
        o = acc[h:h + 1, :]
        o = o * lax.rsqrt(jnp.mean(o * o, axis=1, keepdims=True) + EPS) * g_ref[...]
        o_ref[0, :, sl] = (o * (1.0 - lam_init) * _silu(gate_ref[0, :, sl])).astype(o_ref.dtype)


def _diff_decode(page_table, lams, dq, dk_new, dv_new, gate, g_norm, cache_k, cache_v,
                 lam_init, gate_col0):
    bd, _, d_d = dq.shape
    n_pages = page_table.shape[1]
    _, n_rows, dv_d = cache_k.shape
    lam_spec = pl.BlockSpec(lams[0].shape, lambda i, pt: (0, 0))
    tok = pl.BlockSpec((1, 1, d_d), lambda i, pt: (i, 0, 0))
    page_specs = [pl.BlockSpec((None, n_rows, dv_d), lambda i, pt, j=j: (pt[i, j], 0, 0))
                  for j in range(n_pages)]
    new_tok = pl.BlockSpec((1, H_D, dv_d), lambda i, pt: (i, 0, 0))
    in_specs = ([lam_spec] * 4 + [tok, new_tok, new_tok,
                                  pl.BlockSpec((1, 1, d_d), lambda i, pt: (i, 0, gate_col0 // d_d)),
                                  pl.BlockSpec((1, dv_d), lambda i, pt: (0, 0))]
                + page_specs + page_specs)
    grid_spec = pltpu.PrefetchScalarGridSpec(
        num_scalar_prefetch=1, grid=(bd,), in_specs=in_specs, out_specs=tok)
    return pl.pallas_call(
        functools.partial(_diff_decode_kernel, n_pages=n_pages, lam_init=lam_init),
        grid_spec=grid_spec, out_shape=jax.ShapeDtypeStruct((bd, 1, d_d), BF16),
        compiler_params=_params(("parallel",)), name="diff_attn_decode",
    )(page_table, *lams, dq, dk_new, dv_new, gate, g_norm,
      *([cache_k] * n_pages), *([cache_v] * n_pages))


def _mem_decode_kernel(q_ref, k_ref, v_ref, gate_ref, o_ref):
    tb, n_rows, dh = k_ref.shape
    own = _own_head(SUBLANES, n_rows, H_C)
    for b in range(tb):
        q8 = _head_rows(q_ref[b].astype(F32), H_C, SUBLANES // H_C).astype(BF16)
        s = lax.dot_general(q8, k_ref[b].astype(BF16), NT_DIMS,
                            preferred_element_type=F32) * (dh ** -0.5)
        s = jnp.where(own, s, -jnp.inf)
        p = jnp.exp(s - jnp.max(s, axis=1, keepdims=True))
        p = p / jnp.sum(p, axis=1, keepdims=True)
        acc = jnp.dot(p.astype(BF16), v_ref[b].astype(BF16), preferred_element_type=F32)
        for h in range(H_C):
            sl = slice(h * dh, (h + 1) * dh)
            o_ref[b, :, sl] = (acc[h:h + 1, :] * _silu(gate_ref[b, :, sl])).astype(o_ref.dtype)


def _mem_decode(cq, mem_k, mem_v, gate, gate_col0, tb):
    bd, _, d_c = cq.shape
    _, n_rows, dh = mem_k.shape
    tok = pl.BlockSpec((tb, 1, d_c), lambda i: (i, 0, 0))
    mem = pl.BlockSpec((tb, n_rows, dh), lambda i: (i, 0, 0))
    return pl.pallas_call(
        _mem_decode_kernel, grid=(bd // tb,),
        in_specs=[tok, mem, mem, pl.BlockSpec((tb, 1, d_c), lambda i: (i, 0, gate_col0 // d_c))],
        out_specs=tok, out_shape=jax.ShapeDtypeStruct((bd, 1, d_c), BF16),
        compiler_params=_params(("parallel",)), name="memory_attn_decode",
    )(cq, mem_k, mem_v, gate)


def _rope_tables(pos, n_rows, dqk):
    half = dqk // 2
    inv_freq = ROPE_THETA ** (-jnp.arange(half, dtype=F32) / half)
    ang = pos.astype(F32)[:, None] * inv_freq[None, :]
    cos = jnp.cos(ang)
    sin = jnp.sin(ang)
    reps = LANES // dqk
    cos_t = jnp.tile(jnp.concatenate([cos, cos], axis=1), (1, reps))
    sin_t = jnp.tile(jnp.concatenate([-sin, sin], axis=1), (1, reps))
    return (jnp.broadcast_to(cos_t, (n_rows, LANES)), jnp.broadcast_to(sin_t, (n_rows, LANES)))


def _split_weights(w_in_l, b_if_l, d_m, d_d, d_mix):
    g0 = 4 * d_m
    d0 = g0 + N_GATE
    wqt = w_in_l[:, 0:d_m].T.astype(BF16)
    wvt = w_in_l[:, 2 * d_m:3 * d_m].T.astype(BF16)
    wm = jnp.concatenate([w_in_l[:, d_m:2 * d_m], w_in_l[:, 3 * d_m:g0]], axis=1).astype(BF16)
    w_gates = w_in_l[:, g0:d0]
    wg = jnp.pad(w_gates, ((0, 0), (0, LANES - N_GATE))).astype(BF16)
    wgt = w_gates.T.astype(BF16)
    wd = w_in_l[:, d0:d0 + 4 * d_d].astype(BF16)
    wgate = w_in_l[:, d0 + 4 * d_d:].astype(BF16)
    bcol = jnp.pad(b_if_l, (0, LANES - N_GATE)).reshape(1, LANES).astype(F32)
    brow = jnp.broadcast_to(b_if_l.astype(F32)[:, None], (N_GATE, LANES))
    return wqt, wvt, wm, wg, wgt, wd, wgate, bcol, brow


def kernel(x_prompt, x_sample, state_mlstm_c, state_mlstm_n, state_mlstm_m, cache_diff_k, cache_diff_v,
           cache_mem_k, cache_mem_v, page_table, mem_prompt, w_in, b_if, w_mlstm_norm,
           lambda_q1, lambda_k1, lambda_q2, lambda_k2, w_diff_norm, w_mem_kv, w_out, ln_g, ln_b):
    depth = w_in.shape[0]
    assert depth == 1, "single-layer trunk"
    bp, lp, d_model = x_prompt.shape
    bd, ls, _ = x_sample.shape
    assert ls == 1
    d_mix = w_out.shape[1]
    d_m = d_mix // 2
    d_d = d_mix // 4
    d_c = d_mix // 4
    dh_m = d_m // H_M
    dv_d = d_d // H_D
    dqk = dv_d // 2
    assert dv_d == LANES and d_c // H_C == LANES, "one attention head per lane tile"
    n_mem = mem_prompt.shape[1]
    n_pool, page = cache_diff_k.shape[1], cache_diff_k.shape[2]
    past_len = page_table.shape[1] * page
    alpha = (2 * depth) ** 0.25
    layer = 0
    lam_init = 0.8 - 0.6 * math.exp(-0.3 * layer)

    wts = _split_weights(w_in[layer], b_if[layer], d_m, d_d, d_mix)
    lams = tuple(a[layer].reshape(1, dqk).astype(F32)
                 for a in (lambda_q1, lambda_k1, lambda_q2, lambda_k2))
    g_m = w_mlstm_norm[layer].reshape(1, d_m).astype(F32)
    g_d = w_diff_norm[layer].reshape(1, dv_d).astype(F32)
    w_out_b = w_out[layer].astype(BF16)
    g_ln = ln_g[layer].reshape(1, d_model).astype(F32)
    b_ln = ln_b[layer].reshape(1, d_model).astype(F32)
    gate_d0 = d_m
    gate_c0 = d_m + d_d

    mp = bp * lp
    xp2d = x_prompt.reshape(mp, d_model)
    cos_p, sin_p = _rope_tables(jnp.arange(lp), lp, dqk)
    (q_m, k_m, v_m, o_pre, gcol, grow, dq, dk_f, dk_b, dv_f, dv_t, cq, gate) = _input_projection(
        xp2d, wts, cos_p, sin_p, tm=256, qkv_dtype=BF16, transposed_qv=True)
    h_m, c_p, n_p, m_p = _mlstm_prompt(q_m, k_m, v_m, o_pre, gate, grow, gcol, g_m, bp, lp)
    h_d = _diff_prompt(lams, dq, dk_b, dv_t, gate, g_d, bp, lp, lam_init, gate_d0)
    mk_f, mk_b, mv_f, mv_b = _mem_kv(mem_prompt.reshape(bp * n_mem, d_model).astype(BF16),
                                     w_mem_kv[layer].astype(BF16), tm=512)
    h_c = _mem_prompt(cq, mk_b, mv_b, gate, bp, lp, n_mem, 512, gate_c0)
    y_p = _merge(xp2d, h_m, h_d, h_c, w_out_b, g_ln, b_ln, 512, alpha)

    xs2d = x_sample.reshape(bd, d_model)
    cos_s, sin_s = _rope_tables(jnp.full((1,), past_len), bd, dqk)
    (qs, ks, vs, os_pre, gcol_s, _, dq_s, dk_s, _, dv_s, _, cq_s, gate_s) = _input_projection(
        xs2d, wts, cos_s, sin_s, tm=bd, qkv_dtype=F32, transposed_qv=False)

    def tok3(a):
        return a.reshape(bd, 1, a.shape[1])

    ig_s = gcol_s[:, 0:H_M].reshape(bd, H_M, 1, 1)
    fg_s = gcol_s[:, H_M:N_GATE].reshape(bd, H_M, 1, 1)
    hs_m, c_s, n_s, m_s = _mlstm_step(
        tok3(qs), tok3(ks), tok3(vs), tok3(os_pre), tok3(gate_s), ig_s, fg_s,
        state_mlstm_m[layer].astype(F32).reshape(bd, H_M, 1, 1),
        state_mlstm_n[layer].astype(F32).reshape(bd, H_M, 1, dh_m),
        state_mlstm_c[layer].astype(F32), g_m, tb=4)
    hs_d = _diff_decode(page_table, lams, tok3(dq_s), dk_s.reshape(bd, H_D, dv_d),
                        dv_s.reshape(bd, H_D, dv_d), tok3(gate_s), g_d,
                        cache_diff_k[layer].reshape(n_pool, page * H_D, dv_d),
                        cache_diff_v[layer].reshape(n_pool, page * H_D, dv_d), lam_init, gate_d0)
    hs_c = _mem_decode(tok3(cq_s), cache_mem_k[layer].reshape(bd, n_mem * H_C, d_c // H_C),
                       cache_mem_v[layer].reshape(bd, n_mem * H_C, d_c // H_C), tok3(gate_s),
                       gate_c0, tb=4)
    y_s = _merge(xs2d, hs_m.reshape(bd, d_m), hs_d.reshape(bd, d_d), hs_c.reshape(bd, d_c),
                 w_out_b, g_ln, b_ln, bd, alpha)

    return (y_p.reshape(bp, lp, d_model), y_s.reshape(bd, ls, d_model),
            c_p[None], n_p[None], m_p[:, 0, 0:H_M][None],
            dk_f.reshape(1, bp, lp, H_D, dv_d), dv_f.reshape(1, bp, lp, H_D, dv_d),
            mk_f.reshape(1, bp, n_mem, H_C, d_c // H_C), mv_f.reshape(1, bp, n_mem, H_C, d_c // H_C),
            c_s[None], n_s.reshape(1, bd, H_M, dh_m), m_s.reshape(1, bd, H_M),
            dk_s.reshape(1, bd, ls, H_D, dv_d), dv_s.reshape(1, bd, ls, H_D, dv_d))
```

```python
import functools
import math

import jax
import jax.numpy as jnp
from jax import lax
from jax.experimental import pallas as pl
from jax.experimental.pallas import tpu as pltpu

F32 = jnp.float32
BF16 = jnp.bfloat16

H_M = 4
H_D = 4
H_C = 4
N_GATE = 2 * H_M
CHUNK = 128
ROPE_THETA = 10000.0
EPS = 1e-5
LANES = 128
SUBLANES = 8
VMEM_LIMIT = 56 * 1024 * 1024

NT_DIMS = (((1,), (1,)), ((), ()))
TN_DIMS = (((0,), (0,)), ((), ()))


def _params(semantics):
    return pltpu.CompilerParams(dimension_semantics=semantics, vmem_limit_bytes=VMEM_LIMIT)


def _log_sigmoid(x):
    return jnp.minimum(x, 0.0) - jnp.log1p(jnp.exp(-jnp.abs(x)))


def _silu(x):
    return x * jax.nn.sigmoid(x)


def _split_rows(row, order):
    n_rows = 2 * SUBLANES
    r_idx = lax.broadcasted_iota(jnp.int32, (n_rows, row.shape[1]), 0)
    hi = row.astype(BF16).astype(F32)
    parts = {"hi": hi, "lo": row - hi}
    out = jnp.zeros((n_rows, row.shape[1]), F32)
    for i, name in enumerate(order):
        out = jnp.where(r_idx == i, parts[name], out)
    return out.astype(BF16)


def _lambda(lq1_ref, lk1_ref, lq2_ref, lk2_ref, lam_init):
    a = jnp.sum(lq1_ref[...] * lk1_ref[...], axis=1, keepdims=True)
    b = jnp.sum(lq2_ref[...] * lk2_ref[...], axis=1, keepdims=True)
    return jnp.exp(a) - jnp.exp(b) + lam_init


def _inproj_kernel(x_ref, wqt_ref, wvt_ref, wm_ref, wg_ref, wgt_ref, wd_ref, wgate_ref,
                   bcol_ref, brow_ref,
                   cos_ref, sin_ref,
                   q_ref, k_ref, v_ref, gcol_ref, grow_ref,
                   dq_ref, dkf_ref, dkb_ref, dvf_ref, dvt_ref, cq_ref, gate_ref,
                   *, d_m, d_d, dh_m, dqk, transposed_qv):
    x = x_ref[...].astype(BF16)

    def mm(w):
        return jnp.dot(x, w, preferred_element_type=F32)

    def mm_t(a, b):
        return lax.dot_general(a, b, NT_DIMS, preferred_element_type=F32)

    if transposed_qv:
        q_ref[0] = mm_t(wqt_ref[...], x).astype(q_ref.dtype)
        v_ref[0] = mm_t(wvt_ref[...], x).astype(v_ref.dtype)
    else:
        q_ref[...] = mm_t(x, wqt_ref[...]).astype(q_ref.dtype)
        v_ref[...] = mm_t(x, wvt_ref[...]).astype(v_ref.dtype)
    k_ref[...] = (mm(wm_ref[:, 0:d_m]) * (dh_m ** -0.5)).astype(k_ref.dtype)
    gate_ref[:, 0:d_m] = (jax.nn.sigmoid(mm(wm_ref[:, d_m:2 * d_m]))
                          * _silu(mm(wgate_ref[:, 0:d_m]))).astype(gate_ref.dtype)
    gate_ref[:, d_m:] = _silu(mm(wgate_ref[:, d_m:])).astype(gate_ref.dtype)
    gcol_ref[...] = mm(wg_ref[...]) + bcol_ref[...]
    grow_ref[...] = (lax.dot_general(wgt_ref[...], x, NT_DIMS, preferred_element_type=F32)
                     + brow_ref[:, 0:1])

    cos = cos_ref[...]
    sin = sin_ref[...]
    lane = lax.broadcasted_iota(jnp.int32, cos.shape, 1)
    first_half = (lane % dqk) < (dqk // 2)

    def rope(t):
        swapped = jnp.where(first_half, pltpu.roll(t, LANES - dqk // 2, 1),
                            pltpu.roll(t, dqk // 2, 1))
        return t * cos + swapped * sin

    tm = x.shape[0]
    dq = mm(wd_ref[:, 0:d_d])
    dk = mm(wd_ref[:, d_d:2 * d_d])
    dv = mm(wd_ref[:, 2 * d_d:3 * d_d])
    dvt_ref[0] = dv.T.astype(dvt_ref.dtype)
    for h in range(H_D):
        sl = slice(h * LANES, (h + 1) * LANES)
        dq_ref[:, sl] = (rope(dq[:, sl]) * (dqk ** -0.5)).astype(dq_ref.dtype)
        dk_h = rope(dk[:, sl])
        dkb_ref[:, sl] = dk_h.astype(dkb_ref.dtype)
        dkf_ref[pl.ds(h, tm, stride=H_D), :] = dk_h
        dvf_ref[pl.ds(h, tm, stride=H_D), :] = dv[:, sl]
    cq_ref[...] = mm(wd_ref[:, 3 * d_d:4 * d_d]).astype(cq_ref.dtype)


def _input_projection(x2d, wts, cos_tab, sin_tab, tm, qkv_dtype, transposed_qv):
    m, d_model = x2d.shape
    wqt, wvt, wm, wg, wgt, wd, wgate, bcol, brow = wts
    d_m = wqt.shape[0]
    d_d = wd.shape[1] // 4
    d_mix = wgate.shape[1]
    dh_m = d_m // H_M
    dqk = d_d // H_D // 2
    grid = (m // tm,)

    def row_spec(n):
        return pl.BlockSpec((tm, n), lambda i: (i, 0))

    def whole(a):
        return pl.BlockSpec(a.shape, lambda i: (0,) * a.ndim, pipeline_mode=pl.Buffered(1))

    if transposed_qv:
        qv_shape = jax.ShapeDtypeStruct((m // tm, d_m, tm), qkv_dtype)
        qv_spec = pl.BlockSpec((1, d_m, tm), lambda i: (i, 0, 0))
    else:
        qv_shape = jax.ShapeDtypeStruct((m, d_m), qkv_dtype)
        qv_spec = row_spec(d_m)
    out_shape = (
        qv_shape,
        jax.ShapeDtypeStruct((m, d_m), qkv_dtype),
        qv_shape,
        jax.ShapeDtypeStruct((m, LANES), F32),
        jax.ShapeDtypeStruct((N_GATE, m), F32),
        jax.ShapeDtypeStruct((m, d_d), BF16),
        jax.ShapeDtypeStruct((m * H_D, LANES), F32),
        jax.ShapeDtypeStruct((m, d_d), BF16),
        jax.ShapeDtypeStruct((m * H_D, LANES), F32),
        jax.ShapeDtypeStruct((m // tm, d_d, tm), BF16),
        jax.ShapeDtypeStruct((m, d_d), BF16),
        jax.ShapeDtypeStruct((m, d_mix), BF16),
    )
    head_rows = pl.BlockSpec((tm * H_D, LANES), lambda i: (i, 0))
    out_specs = (
        qv_spec, row_spec(d_m), qv_spec, row_spec(LANES),
        pl.BlockSpec((N_GATE, tm), lambda i: (0, i)),
        row_spec(d_d), head_rows, row_spec(d_d), head_rows,
        pl.BlockSpec((1, d_d, tm), lambda i: (i, 0, 0)), row_spec(d_d),
        row_spec(d_mix),
    )
    n_tab = cos_tab.shape[0] // tm
    tab_spec = pl.BlockSpec((tm, LANES), lambda i: (i % n_tab, 0))
    in_specs = [row_spec(d_model), whole(wqt), whole(wvt), whole(wm), whole(wg), whole(wgt),
                whole(wd), whole(wgate), whole(bcol), whole(brow), tab_spec, tab_spec]
    return pl.pallas_call(
        functools.partial(_inproj_kernel, d_m=d_m, d_d=d_d, dh_m=dh_m, dqk=dqk,
                          transposed_qv=transposed_qv),
        grid=grid, in_specs=in_specs, out_specs=out_specs, out_shape=out_shape,
        compiler_params=_params(("parallel",)), name="input_projection",
    )(x2d, wqt, wvt, wm, wg, wgt, wd, wgate, bcol, brow, cos_tab, sin_tab)


def _mlstm_prompt_kernel(qt_ref, k_ref, vt_ref, gate_ref, grow_ref, gcol_ref, gb_ref,
                         h_ref, c_ref, n_ref, m_ref, ct_sc, *, dh):
    j = pl.program_id(1)

    @pl.when(j == 0)
    def _():
        ct_sc[...] = jnp.zeros_like(ct_sc)
        n_ref[...] = jnp.zeros_like(n_ref)
        m_ref[...] = jnp.zeros_like(m_ref)

    c = k_ref.shape[0]
    s_idx = lax.broadcasted_iota(jnp.int32, (c, c), 0)
    t_idx = lax.broadcasted_iota(jnp.int32, (c, c), 1)
    causal = s_idx <= t_idx
    lower = t_idx <= s_idx

    heads = []
    for h in range(H_M):
        sl = slice(h * dh, (h + 1) * dh)
        qt = qt_ref[sl, :]
        k = k_ref[:, sl]
        ig_row = grow_ref[h:h + 1, :]
        lf_row = _log_sigmoid(grow_ref[H_M + h:H_M + h + 1, :])
        ig_col = gcol_ref[:, h:h + 1]
        lf_col = _log_sigmoid(gcol_ref[:, H_M + h:H_M + h + 1])
        bcum_col = jnp.sum(jnp.where(lower, lf_row, 0.0), axis=1, keepdims=True)
        bcum_row = jnp.sum(jnp.where(causal, lf_col, 0.0), axis=0, keepdims=True)
        m_prev = m_ref[:, h:h + 1]
        dlog = jnp.where(causal, bcum_row - bcum_col + ig_col, -jnp.inf)
        inter = bcum_row + m_prev
        mt = jnp.maximum(jnp.max(dlog, axis=0, keepdims=True), inter)
        w_inter = jnp.exp(inter - mt)
        st = jnp.dot(k, qt, preferred_element_type=F32) * jnp.exp(dlog - mt)
        ct_old = ct_sc[h]
        n_old = n_ref[h:h + 1, :]
        cq = jnp.dot(ct_old.astype(BF16), qt, preferred_element_type=F32)
        nq2 = jnp.dot(_split_rows(n_old, ("hi", "lo")), qt, preferred_element_type=F32)
        m_new = mt[:, c - 1:c]
        b_last = bcum_row[:, c - 1:c]
        w_end = jnp.exp(b_last - bcum_row + ig_row - m_new)
        decay = jnp.exp(b_last + m_prev - m_new)
        heads.append(dict(sl=sl, k=k, mt=mt, w_inter=w_inter, st=st, ct_old=ct_old, n_old=n_old,
                          cq=cq, nq=nq2[0:1, :] + nq2[1:2, :], m_new=m_new, w_end=w_end,
                          decay=decay))

    for h, d in enumerate(heads):
        vt = vt_ref[d["sl"], :]
        num = jnp.dot(vt, d["st"].astype(BF16), preferred_element_type=F32) + d["w_inter"] * d["cq"]
        den = jnp.sum(d["st"], axis=0, keepdims=True) + d["w_inter"] * d["nq"]
        d["hc"] = num * (1.0 / jnp.maximum(jnp.abs(den), jnp.exp(-d["mt"])))
        vtw = (vt.astype(F32) * d["w_end"]).astype(BF16)
        ct_sc[h] = d["decay"] * d["ct_old"] + jnp.dot(vtw, d["k"], preferred_element_type=F32)
        kw2 = jnp.dot(_split_rows(d["w_end"], ("hi", "lo")), d["k"], preferred_element_type=F32)
        n_ref[h:h + 1, :] = d["decay"] * d["n_old"] + kw2[0:1, :] + kw2[1:2, :]
        m_ref[:, h:h + 1] = d["m_new"]

    for h, d in enumerate(heads):
        sl = d["sl"]
        hc = d["hc"]
        mu = jnp.mean(hc, axis=0, keepdims=True)
        var = jnp.mean(jnp.square(hc - mu), axis=0, keepdims=True)
        hn = ((hc - mu) * lax.rsqrt(var + EPS) * gb_ref[sl, :]).T
        h_ref[:, sl] = (hn * gate_ref[:, sl]).astype(h_ref.dtype)

    @pl.when(j == pl.num_programs(1) - 1)
    def _():
        for h in range(H_M):
            c_ref[h] = ct_sc[h].T


def _mlstm_prompt(q_t, k, v_t, gate, grow, gcol, g_norm, b, l):
    d_m = k.shape[1]
    dh = d_m // H_M
    nc = l // CHUNK
    tm = q_t.shape[2]
    per = tm // CHUNK
    tok = pl.BlockSpec((CHUNK, d_m), lambda i, j: (i * nc + j, 0))
    tok_t = pl.BlockSpec((None, d_m, CHUNK), lambda i, j: ((i * nc + j) // per, 0, j % per))
    g_bcast = jnp.broadcast_to(g_norm.reshape(d_m, 1), (d_m, CHUNK))
    in_specs = [tok_t, tok, tok_t, tok,
                pl.BlockSpec((N_GATE, CHUNK), lambda i, j: (0, i * nc + j)),
                pl.BlockSpec((CHUNK, LANES), lambda i, j: (i * nc + j, 0)),
                pl.BlockSpec((d_m, CHUNK), lambda i, j: (0, 0))]
    out_shape = (jax.ShapeDtypeStruct((b * l, d_m), BF16),
                 jax.ShapeDtypeStruct((b, H_M, dh, dh), F32),
                 jax.ShapeDtypeStruct((b, H_M, dh), F32),
                 jax.ShapeDtypeStruct((b, 1, LANES), F32))
    out_specs = (tok,
                 pl.BlockSpec((None, H_M, dh, dh), lambda i, j: (i, 0, 0, 0)),
                 pl.BlockSpec((None, H_M, dh), lambda i, j: (i, 0, 0)),
                 pl.BlockSpec((None, 1, LANES), lambda i, j: (i, 0, 0)))
    return pl.pallas_call(
        functools.partial(_mlstm_prompt_kernel, dh=dh),
        grid=(b, nc), in_specs=in_specs, out_specs=out_specs, out_shape=out_shape,
        scratch_shapes=[pltpu.VMEM((H_M, dh, dh), F32)],
        compiler_params=_params(("parallel", "arbitrary")), name="mlstm_prompt",
    )(q_t, k, v_t, gate, grow, gcol, g_bcast)


def _diff_prompt_kernel(lq1_ref, lk1_ref, lq2_ref, lk2_ref, q_ref, k_ref, vt_ref, gate_ref, g_ref,
                        o_ref, m_sc, l_sc, acc_sc, *, lam_init, dqk):
    tq = q_ref.shape[0]
    dv = 2 * dqk
    qi = pl.program_id(1)
    lam = _lambda(lq1_ref, lk1_ref, lq2_ref, lk2_ref, lam_init)
    lane = lax.broadcasted_iota(jnp.int32, (tq, dv), 1)
    key_i = lax.broadcasted_iota(jnp.int32, (tq, 2 * tq), 0)
    qry_i = lax.broadcasted_iota(jnp.int32, (tq, 2 * tq), 1)
    visible = key_i <= jnp.where(qry_i >= tq, qry_i - tq, qry_i)
    ones = jnp.ones((2 * SUBLANES, tq), BF16)
    q_both = []
    for h in range(H_D):
        q = q_ref[:, h * dv:(h + 1) * dv]
        zero = jnp.zeros_like(q)
        q_both.append(jnp.concatenate([jnp.where(lane < dqk, q, zero),
                                       jnp.where(lane >= dqk, q, zero)], axis=0))
    m_sc[...] = jnp.full(m_sc.shape, -jnp.inf, F32)
    l_sc[...] = jnp.zeros_like(l_sc)
    acc_sc[...] = jnp.zeros_like(acc_sc)

    def key_span(first_tile, n_tiles, on_diagonal):
        start = pl.multiple_of(first_tile * tq, tq)
        scores = []
        for h in range(H_D):
            hs = slice(h * dv, (h + 1) * dv)
            k = k_ref[pl.ds(start, n_tiles * tq), hs]
            scores.append(lax.dot_general(k, q_both[h], NT_DIMS,
                                          preferred_element_type=F32))
        for h in range(H_D):
            hs = slice(h * dv, (h + 1) * dv)
            st = scores[h]
            if on_diagonal:
                st = jnp.where(visible, st, -jnp.inf)
            m_old = m_sc[h]
            m_new = jnp.maximum(m_old, jnp.max(st, axis=0, keepdims=True))
            alpha = jnp.exp(m_old - m_new)
            pt = jnp.exp(st - m_new).astype(BF16)
            pv = jnp.dot(jnp.concatenate([vt_ref[first_tile, hs, :], ones], axis=0), pt[0:tq],
                         preferred_element_type=F32)
            for t in range(1, n_tiles):
                pv += jnp.dot(jnp.concatenate([vt_ref[first_tile + t, hs, :], ones], axis=0),
                              pt[t * tq:(t + 1) * tq], preferred_element_type=F32)
            l_sc[h] = alpha * l_sc[h] + pv[dv:dv + 1, :]
            acc_sc[h] = alpha * acc_sc[h] + pv[0:dv, :]
            m_sc[h] = m_new

    def tile_pair(i, carry):
        key_span(2 * i, 2, False)
        return carry

    lax.fori_loop(0, qi // 2, tile_pair, 0)

    @pl.when(qi % 2 == 1)
    def _():
        key_span(qi - 1, 1, False)

    key_span(qi, 1, True)

    for h in range(H_D):
        hs = slice(h * dv, (h + 1) * dv)
        on = acc_sc[h] / l_sc[h]
        o = (on[:, 0:tq] - lam * on[:, tq:2 * tq]).T
        o = o * lax.rsqrt(jnp.mean(o * o, axis=1, keepdims=True) + EPS) * g_ref[...]
        o_ref[:, hs] = (o * (1.0 - lam_init) * gate_ref[:, hs]).astype(o_ref.dtype)


def _diff_prompt(lams, dq, dk, dv_t, gate, g_norm, b, l, lam_init, gate_col0):
    d_d = dq.shape[1]
    dv_d = d_d // H_D
    tq = dv_t.shape[2]
    nq = l // tq
    lam_spec = pl.BlockSpec(lams[0].shape, lambda i, j: (0, 0))
    tok = pl.BlockSpec((tq, d_d), lambda i, j: (i * nq + j, 0))
    in_specs = [lam_spec] * 4 + [
        tok,
        pl.BlockSpec((l, d_d), lambda i, j: (i, 0)),
        pl.BlockSpec((nq, d_d, tq), lambda i, j: (i, 0, 0)),
        pl.BlockSpec((tq, d_d), lambda i, j: (i * nq + j, gate_col0 // d_d)),
        pl.BlockSpec((1, dv_d), lambda i, j: (0, 0))]
    return pl.pallas_call(
        functools.partial(_diff_prompt_kernel, lam_init=lam_init, dqk=dv_d // 2),
        grid=(b, nq), in_specs=in_specs, out_specs=tok,
        out_shape=jax.ShapeDtypeStruct((b * l, d_d), BF16),
        scratch_shapes=[pltpu.VMEM((H_D, 1, 2 * tq), F32), pltpu.VMEM((H_D, 1, 2 * tq), F32),
                        pltpu.VMEM((H_D, dv_d, 2 * tq), F32)],
        compiler_params=_params(("parallel", "parallel")), name="diff_attn_prompt",
    )(*lams, dq, dk, dv_t, gate, g_norm)


def _mem_kv_kernel(x_ref, w_ref, kf_ref, kb_ref, vf_ref, vb_ref):
    tm, d_c = kb_ref.shape
    dh = d_c // H_C
    x = x_ref[...]
    mk = jnp.dot(x, w_ref[:, 0:d_c], preferred_element_type=F32)
    mv = jnp.dot(x, w_ref[:, d_c:2 * d_c], preferred_element_type=F32)
    kb_ref[...] = mk.astype(kb_ref.dtype)
    vb_ref[...] = mv.astype(vb_ref.dtype)
    for h in range(H_C):
        sl = slice(h * dh, (h + 1) * dh)
        kf_ref[pl.ds(h, tm, stride=H_C), :] = mk[:, sl]
        vf_ref[pl.ds(h, tm, stride=H_C), :] = mv[:, sl]


def _mem_kv(mem2d, w, tm):
    m, d_model = mem2d.shape
    d_c = w.shape[1] // 2
    dh = d_c // H_C
    row = pl.BlockSpec((tm, d_c), lambda i: (i, 0))
    head_rows = pl.BlockSpec((tm * H_C, dh), lambda i: (i, 0))
    return pl.pallas_call(
        _mem_kv_kernel, grid=(m // tm,),
        in_specs=[pl.BlockSpec((tm, d_model), lambda i: (i, 0)),
                  pl.BlockSpec(w.shape, lambda i: (0, 0))],
        out_specs=(head_rows, row, head_rows, row),
        out_shape=(jax.ShapeDtypeStruct((m * H_C, dh), F32), jax.ShapeDtypeStruct((m, d_c), BF16),
                   jax.ShapeDtypeStruct((m * H_C, dh), F32), jax.ShapeDtypeStruct((m, d_c), BF16)),
        compiler_params=_params(("parallel",)), name="memory_kv",
    )(mem2d, w)


def _mem_prompt_kernel(q_ref, k_ref, v_ref, gate_ref, o_ref, *, dh):
    for h in range(H_C):
        sl = slice(h * dh, (h + 1) * dh)
        s = lax.dot_general(q_ref[:, sl], k_ref[:, sl], NT_DIMS,
                            preferred_element_type=F32) * (dh ** -0.5)
        p = jnp.exp(s - jnp.max(s, axis=1, keepdims=True))
        p = p / jnp.sum(p, axis=1, keepdims=True)
        o = jnp.dot(p.astype(BF16), v_ref[:, sl], preferred_element_type=F32)
        o_ref[:, sl] = (o * gate_ref[:, sl]).astype(o_ref.dtype)


def _mem_prompt(cq, mk, mv, gate, b, l, n_mem, tl, gate_col0):
    d_c = cq.shape[1]
    nl = l // tl
    tok = pl.BlockSpec((tl, d_c), lambda i, j: (i * nl + j, 0))
    mem = pl.BlockSpec((n_mem, d_c), lambda i, j: (i, 0))
    return pl.pallas_call(
        functools.partial(_mem_prompt_kernel, dh=d_c // H_C),
        grid=(b, nl),
        in_specs=[tok, mem, mem,
                  pl.BlockSpec((tl, d_c), lambda i, j: (i * nl + j, gate_col0 // d_c))],
        out_specs=tok, out_shape=jax.ShapeDtypeStruct((b * l, d_c), BF16),
        compiler_params=_params(("parallel", "parallel")), name="memory_attn_prompt",
    )(cq, mk, mv, gate)


def _merge_kernel(x_ref, hm_ref, hd_ref, hc_ref, w_ref, g_ref, b_ref, y_ref, *, alpha):
    tm = x_ref.shape[0]
    n_strips = max(1, tm // 128)
    rows = tm // n_strips
    for r in range(n_strips):
        rs = slice(r * rows, (r + 1) * rows)
        h = jnp.concatenate([hm_ref[rs, :], hd_ref[rs, :], hc_ref[rs, :]], axis=1)
        y = alpha * x_ref[rs, :] + jnp.dot(h, w_ref[...], preferred_element_type=F32)
        mu = jnp.mean(y, axis=1, keepdims=True)
        var = jnp.mean(jnp.square(y - mu), axis=1, keepdims=True)
        y_ref[rs, :] = (y - mu) * lax.rsqrt(var + EPS) * g_ref[...] + b_ref[...]


def _merge(x2d, h_m, h_d, h_c, w_out, ln_g, ln_b, tm, alpha):
    m, d_model = x2d.shape

    def row(n):
        return pl.BlockSpec((tm, n), lambda i: (i, 0))

    def whole(a):
        return pl.BlockSpec(a.shape, lambda i: (0, 0))

    return pl.pallas_call(
        functools.partial(_merge_kernel, alpha=alpha), grid=(m // tm,),
        in_specs=[row(d_model), row(h_m.shape[1]), row(h_d.shape[1]), row(h_c.shape[1]),
                  whole(w_out), whole(ln_g), whole(ln_b)],
        out_specs=row(d_model), out_shape=jax.ShapeDtypeStruct((m, d_model), F32),
        compiler_params=_params(("parallel",)), name="output_projection",
    )(x2d, h_m, h_d, h_c, w_out, ln_g, ln_b)


def _mlstm_step_kernel(q_ref, k_ref, v_ref, gate_ref, ig_ref, fg_ref, m_ref, n_ref, c_ref,
                       g_ref, h_ref, cn_ref, nn_ref, mn_ref, *, dh):
    tb = q_ref.shape[0]
    for b in range(tb):
        for h in range(H_M):
            sl = slice(h * dh, (h + 1) * dh)
            q = q_ref[b, :, sl]
            k = k_ref[b, :, sl]
            v = v_ref[b, :, sl]
            ig = ig_ref[b, h]
            lf = _log_sigmoid(fg_ref[b, h])
            m_prev = m_ref[b, h]
            n_old = n_ref[b, h]
            c_old = c_ref[b, h]

            inter = lf + m_prev
            mt = jnp.maximum(ig, inter)
            w_in = jnp.exp(ig - mt)
            w_st = jnp.exp(inter - mt)
            s = jnp.sum(q * k, axis=1, keepdims=True) * w_in
            qc2 = jnp.dot(_split_rows(q, ("hi", "lo")), c_old.astype(BF16),
                          preferred_element_type=F32)
            qc = qc2[0:1, :] + qc2[1:2, :]
            outer = lax.dot_general(_split_rows(k, ("hi", "hi", "lo")),
                                    _split_rows(v, ("hi", "lo", "hi")), TN_DIMS,
                                    preferred_element_type=F32)
            num = s * v + w_st * qc
            den = s + w_st * jnp.sum(q * n_old, axis=1, keepdims=True)
            hc = num / jnp.maximum(jnp.abs(den), jnp.exp(-mt))

            cn_ref[b, h] = w_st * c_old + w_in * outer
            nn_ref[b, h] = w_st * n_old + w_in * k
            mn_ref[b, h] = mt

            mu = jnp.mean(hc, axis=1, keepdims=True)
            var = jnp.mean(jnp.square(hc - mu), axis=1, keepdims=True)
            hn = (hc - mu) * lax.rsqrt(var + EPS) * g_ref[:, sl]
            h_ref[b, :, sl] = (hn * gate_ref[b, :, sl]).astype(h_ref.dtype)


def _mlstm_step(q, k, v, gate, ig, fg, m0, n0, c0, g_norm, tb):
    bd, _, d_m = q.shape
    dh = d_m // H_M

    def blk(shape):
        nd = len(shape)
        return pl.BlockSpec((tb,) + tuple(shape), lambda i: (i,) + (0,) * nd)

    tok = blk((1, d_m))
    one = blk((H_M, 1, 1))
    in_specs = [tok, tok, tok, tok, one, one, one, blk((H_M, 1, dh)), blk((H_M, dh, dh)),
                pl.BlockSpec((1, d_m), lambda i: (0, 0))]
    out_shape = (jax.ShapeDtypeStruct((bd, 1, d_m), BF16),
                 jax.ShapeDtypeStruct((bd, H_M, dh, dh), F32),
                 jax.ShapeDtypeStruct((bd, H_M, 1, dh), F32),
                 jax.ShapeDtypeStruct((bd, H_M, 1, 1), F32))
    out_specs = (tok, blk((H_M, dh, dh)), blk((H_M, 1, dh)), one)
    return pl.pallas_call(
        functools.partial(_mlstm_step_kernel, dh=dh), grid=(bd // tb,),
        in_specs=in_specs, out_specs=out_specs, out_shape=out_shape,
        compiler_params=_params(("parallel",)), name="mlstm_step",
    )(q, k, v, gate, ig, fg, m0, n0, c0, g_norm)


def _head_rows(row, n_heads, reps):
    dh = row.shape[1] // n_heads
    heads = [row[:, h * dh:(h + 1) * dh] for h in range(n_heads)]
    return jnp.concatenate(heads * reps, axis=0)


def _own_head(n_rows, n_cols, n_heads):
    r = lax.broadcasted_iota(jnp.int32, (n_rows, n_cols), 0)
    c = lax.broadcasted_iota(jnp.int32, (n_rows, n_cols), 1)
    return (c % n_heads) == (r % n_heads)


def _diff_decode_kernel(pt_ref, lq1_ref, lk1_ref, lq2_ref, lk2_ref, q_ref, kn_ref, vn_ref,
                        gate_ref, g_ref, *rest, n_pages, lam_init):
    del pt_ref
    k_refs = rest[:n_pages]
    v_refs = rest[n_pages:2 * n_pages]
    o_ref = rest[2 * n_pages]
    n_rows, dv_d = k_refs[0].shape
    dqk = dv_d // 2
    lam = _lambda(lq1_ref, lk1_ref, lq2_ref, lk2_ref, lam_init)

    q8 = _head_rows(q_ref[0].astype(F32), H_D, 2)
    r_idx = lax.broadcasted_iota(jnp.int32, q8.shape, 0)
    l_idx = lax.broadcasted_iota(jnp.int32, q8.shape, 1)
    q8 = jnp.where((l_idx // dqk) == (r_idx // H_D), q8, 0.0)
    q8b = q8.astype(BF16)
    own = _own_head(2 * H_D, n_rows, H_D)

    scores = [jnp.where(own, lax.dot_general(q8b, k_refs[j][...].astype(BF16), NT_DIMS,
                                             preferred_element_type=F32), -jnp.inf)
              for j in range(n_pages)]
    kn4 = kn_ref[0]
    vn4 = vn_ref[0]
    s_new = jnp.sum(q8 * jnp.concatenate([kn4, kn4], axis=0), axis=1, keepdims=True)
    mx = s_new
    for s in scores:
        mx = jnp.maximum(mx, jnp.max(s, axis=1, keepdims=True))
    probs = [jnp.exp(s - mx) for s in scores]
    p_new = jnp.exp(s_new - mx)
    total = p_new
    for p in probs:
        total = total + jnp.sum(p, axis=1, keepdims=True)
    inv = 1.0 / total

    def combine(p):
        pn = p * inv
        return pn - lam * pltpu.roll(pn, SUBLANES - H_D, 0)

    acc = (combine(jnp.broadcast_to(p_new, (SUBLANES, LANES)))[:, 0:1]
           * jnp.concatenate([vn4, vn4], axis=0))
    for j in range(n_pages):
        acc = acc + jnp.dot(combine(probs[j]).astype(BF16), v_refs[j][...].astype(BF16),
                            preferred_element_type=F32)
    for h in range(H_D):
        sl = slice(h * dv_d, (h + 1) * dv_d)
        o = acc[h:h + 1, :]
        o = o * lax.rsqrt(jnp.mean(o * o, axis=1, keepdims=True) + EPS) * g_ref[...]
        o_ref[0, :, sl] = (o * (1.0 - lam_init) * gate_ref[0, :, sl]).astype(o_ref.dtype)


def _diff_decode(page_table, lams, dq, dk_new, dv_new, gate, g_norm, cache_k, cache_v,
                 lam_init, gate_col0):
    bd, _, d_d = dq.shape
    n_pages = page_table.shape[1]
    _, n_rows, dv_d = cache_k.shape
    lam_spec = pl.BlockSpec(lams[0].shape, lambda i, pt: (0, 0))
    tok = pl.BlockSpec((1, 1, d_d), lambda i, pt: (i, 0, 0))
    page_specs = [pl.BlockSpec((None, n_rows, dv_d), lambda i, pt, j=j: (pt[i, j], 0, 0))
                  for j in range(n_pages)]
    new_tok = pl.BlockSpec((1, H_D, dv_d), lambda i, pt: (i, 0, 0))
    in_specs = ([lam_spec] * 4 + [tok, new_tok, new_tok,
                                  pl.BlockSpec((1, 1, d_d), lambda i, pt: (i, 0, gate_col0 // d_d)),
                                  pl.BlockSpec((1, dv_d), lambda i, pt: (0, 0))]
                + page_specs + page_specs)
    grid_spec = pltpu.PrefetchScalarGridSpec(
        num_scalar_prefetch=1, grid=(bd,), in_specs=in_specs, out_specs=tok)
    return pl.pallas_call(
        functools.partial(_diff_decode_kernel, n_pages=n_pages, lam_init=lam_init),
        grid_spec=grid_spec, out_shape=jax.ShapeDtypeStruct((bd, 1, d_d), BF16),
        compiler_params=_params(("parallel",)), name="diff_attn_decode",
    )(page_table, *lams, dq, dk_new, dv_new, gate, g_norm,
      *([cache_k] * n_pages), *([cache_v] * n_pages))


def _mem_decode_kernel(q_ref, k_ref, v_ref, gate_ref, o_ref):
    tb, n_rows, dh = k_ref.shape
    own = _own_head(SUBLANES, n_rows, H_C)
    for b in range(tb):
        q8 = _head_rows(q_ref[b].astype(F32), H_C, SUBLANES // H_C).astype(BF16)
        s = lax.dot_general(q8, k_ref[b].astype(BF16), NT_DIMS,
                            preferred_element_type=F32) * (dh ** -0.5)
        s = jnp.where(own, s, -jnp.inf)
        p = jnp.exp(s - jnp.max(s, axis=1, keepdims=True))
        p = p / jnp.sum(p, axis=1, keepdims=True)
        acc = jnp.dot(p.astype(BF16), v_ref[b].astype(BF16), preferred_element_type=F32)
        for h in range(H_C):
            sl = slice(h * dh, (h + 1) * dh)
            o_ref[b, :, sl] = (acc[h:h + 1, :] * gate_ref[b, :, sl]).astype(o_ref.dtype)


def _mem_decode(cq, mem_k, mem_v, gate, gate_col0, tb):
    bd, _, d_c = cq.shape
    _, n_rows, dh = mem_k.shape
    tok = pl.BlockSpec((tb, 1, d_c), lambda i: (i, 0, 0))
    mem = pl.BlockSpec((tb, n_rows, dh), lambda i: (i, 0, 0))
    return pl.pallas_call(
        _mem_decode_kernel, grid=(bd // tb,),
        in_specs=[tok, mem, mem, pl.BlockSpec((tb, 1, d_c), lambda i: (i, 0, gate_col0 // d_c))],
        out_specs=tok, out_shape=jax.ShapeDtypeStruct((bd, 1, d_c), BF16),
        compiler_params=_params(("parallel",)), name="memory_attn_decode",
    )(cq, mem_k, mem_v, gate)


def _rope_tables(pos, n_rows, dqk):
    half = dqk // 2
    inv_freq = ROPE_THETA ** (-jnp.arange(half, dtype=F32) / half)
    ang = pos.astype(F32)[:, None] * inv_freq[None, :]
    cos = jnp.cos(ang)
    sin = jnp.sin(ang)
    reps = LANES // dqk
    cos_t = jnp.tile(jnp.concatenate([cos, cos], axis=1), (1, reps))
    sin_t = jnp.tile(jnp.concatenate([-sin, sin], axis=1), (1, reps))
    return (jnp.broadcast_to(cos_t, (n_rows, LANES)), jnp.broadcast_to(sin_t, (n_rows, LANES)))


def _split_weights(w_in_l, b_if_l, d_m, d_d, d_mix):
    g0 = 4 * d_m
    d0 = g0 + N_GATE
    wqt = w_in_l[:, 0:d_m].T.astype(BF16)
    wvt = w_in_l[:, 2 * d_m:3 * d_m].T.astype(BF16)
    wm = jnp.concatenate([w_in_l[:, d_m:2 * d_m], w_in_l[:, 3 * d_m:g0]], axis=1).astype(BF16)
    w_gates = w_in_l[:, g0:d0]
    wg = jnp.pad(w_gates, ((0, 0), (0, LANES - N_GATE))).astype(BF16)
    wgt = w_gates.T.astype(BF16)
    wd = w_in_l[:, d0:d0 + 4 * d_d].astype(BF16)
    wgate = w_in_l[:, d0 + 4 * d_d:].astype(BF16)
    bcol = jnp.pad(b_if_l, (0, LANES - N_GATE)).reshape(1, LANES).astype(F32)
    brow = jnp.broadcast_to(b_if_l.astype(F32)[:, None], (N_GATE, LANES))
    return wqt, wvt, wm, wg, wgt, wd, wgate, bcol, brow


def kernel(x_prompt, x_sample, state_mlstm_c, state_mlstm_n, state_mlstm_m, cache_diff_k, cache_diff_v,
           cache_mem_k, cache_mem_v, page_table, mem_prompt, w_in, b_if, w_mlstm_norm,
           lambda_q1, lambda_k1, lambda_q2, lambda_k2, w_diff_norm, w_mem_kv, w_out, ln_g, ln_b):
    depth = w_in.shape[0]
    assert depth == 1, "single-layer trunk"
    bp, lp, d_model = x_prompt.shape
    bd, ls, _ = x_sample.shape
    assert ls == 1
    d_mix = w_out.shape[1]
    d_m = d_mix // 2
    d_d = d_mix // 4
    d_c = d_mix // 4
    dh_m = d_m // H_M
    dv_d = d_d // H_D
    dqk = dv_d // 2
    assert dv_d == LANES and d_c // H_C == LANES, "one attention head per lane tile"
    n_mem = mem_prompt.shape[1]
    n_pool, page = cache_diff_k.shape[1], cache_diff_k.shape[2]
    past_len = page_table.shape[1] * page
    alpha = (2 * depth) ** 0.25
    layer = 0
    lam_init = 0.8 - 0.6 * math.exp(-0.3 * layer)

    wts = _split_weights(w_in[layer], b_if[layer], d_m, d_d, d_mix)
    lams = tuple(a[layer].reshape(1, dqk).astype(F32)
                 for a in (lambda_q1, lambda_k1, lambda_q2, lambda_k2))
    g_m = w_mlstm_norm[layer].reshape(1, d_m).astype(F32)
    g_d = w_diff_norm[layer].reshape(1, dv_d).astype(F32)
    w_out_b = w_out[layer].astype(BF16)
    g_ln = ln_g[layer].reshape(1, d_model).astype(F32)
    b_ln = ln_b[layer].reshape(1, d_model).astype(F32)
    gate_d0 = d_m
    gate_c0 = d_m + d_d

    mp = bp * lp
    xp2d = x_prompt.reshape(mp, d_model)
    cos_p, sin_p = _rope_tables(jnp.arange(lp), lp, dqk)
    (q_m, k_m, v_m, gcol, grow, dq, dk_f, dk_b, dv_f, dv_t, cq, gate) = _input_projection(
        xp2d, wts, cos_p, sin_p, tm=256, qkv_dtype=BF16, transposed_qv=True)
    h_m, c_p, n_p, m_p = _mlstm_prompt(q_m, k_m, v_m, gate, grow, gcol, g_m, bp, lp)
    h_d = _diff_prompt(lams, dq, dk_b, dv_t, gate, g_d, bp, lp, lam_init, gate_d0)
    mk_f, mk_b, mv_f, mv_b = _mem_kv(mem_prompt.reshape(bp * n_mem, d_model).astype(BF16),
                                     w_mem_kv[layer].astype(BF16), tm=512)
    h_c = _mem_prompt(cq, mk_b, mv_b, gate, bp, lp, n_mem, 512, gate_c0)
    y_p = _merge(xp2d, h_m, h_d, h_c, w_out_b, g_ln, b_ln, 512, alpha)

    xs2d = x_sample.reshape(bd, d_model)
    cos_s, sin_s = _rope_tables(jnp.full((1,), past_len), bd, dqk)
    (qs, ks, vs, gcol_s, _, dq_s, dk_s, _, dv_s, _, cq_s, gate_s) = _input_projection(
        xs2d, wts, cos_s, sin_s, tm=bd, qkv_dtype=F32, transposed_qv=False)

    def tok3(a):
        return a.reshape(bd, 1, a.shape[1])

    ig_s = gcol_s[:, 0:H_M].reshape(bd, H_M, 1, 1)
    fg_s = gcol_s[:, H_M:N_GATE].reshape(bd, H_M, 1, 1)
    hs_m, c_s, n_s, m_s = _mlstm_step(
        tok3(qs), tok3(ks), tok3(vs), tok3(gate_s), ig_s, fg_s,
        state_mlstm_m[layer].astype(F32).reshape(bd, H_M, 1, 1),
        state_mlstm_n[layer].astype(F32).reshape(bd, H_M, 1, dh_m),
        state_mlstm_c[layer].astype(F32), g_m, tb=4)
    hs_d = _diff_decode(page_table, lams, tok3(dq_s), dk_s.reshape(bd, H_D, dv_d),
                        dv_s.reshape(bd, H_D, dv_d), tok3(gate_s), g_d,
                        cache_diff_k[layer].reshape(n_pool, page * H_D, dv_d),
                        cache_diff_v[layer].reshape(n_pool, page * H_D, dv_d), lam_init, gate_d0)
    hs_c = _mem_decode(tok3(cq_s), cache_mem_k[layer].reshape(bd, n_mem * H_C, d_c // H_C),
                       cache_mem_v[layer].reshape(bd, n_mem * H_C, d_c // H_C), tok3(gate_s),
                       gate_c0, tb=8)
    y_s = _merge(xs2d, hs_m.reshape(bd, d_m), hs_d.reshape(bd, d_d), hs_c.reshape(bd, d_c),
                 w_out_b, g_ln, b_ln, bd, alpha)

    return (y_p.reshape(bp, lp, d_model), y_s.reshape(bd, ls, d_model),
            c_p[None], n_p[None], m_p[:, 0, 0:H_M][None],
            dk_f.reshape(1, bp, lp, H_D, dv_d), dv_f.reshape(1, bp, lp, H_D, dv_d),
            mk_f.reshape(1, bp, n_mem, H_C, d_c // H_C), mv_f.reshape(1, bp, n_mem, H_C, d_c // H_C),
            c_s[None], n_s.reshape(1, bd, H_M, dh_m), m_s.reshape(1, bd, H_M),
            dk_s.reshape(1, bd, ls, H_D, dv_d), dv_s.reshape(1, bd, ls, H_D, dv_d))
```

```python
import functools
import math

import jax
import jax.numpy as jnp
from jax import lax
from jax.experimental import pallas as pl
from jax.experimental.pallas import tpu as pltpu

F32 = jnp.float32
BF16 = jnp.bfloat16

H_M = 4
H_D = 4
H_C = 4
N_GATE = 2 * H_M
CHUNK = 128
ROPE_THETA = 10000.0
EPS = 1e-5
LANES = 128
SUBLANES = 8
VMEM_LIMIT = 56 * 1024 * 1024

NT_DIMS = (((1,), (1,)), ((), ()))
TN_DIMS = (((0,), (0,)), ((), ()))


def _params(semantics):
    return pltpu.CompilerParams(dimension_semantics=semantics, vmem_limit_bytes=VMEM_LIMIT)


def _log_sigmoid(x):
    return jnp.minimum(x, 0.0) - jnp.log1p(jnp.exp(-jnp.abs(x)))


def _silu(x):
    return x * jax.nn.sigmoid(x)


def _split_rows(row, order):
    n_rows = 2 * SUBLANES
    r_idx = lax.broadcasted_iota(jnp.int32, (n_rows, row.shape[1]), 0)
    hi = row.astype(BF16).astype(F32)
    parts = {"hi": hi, "lo": row - hi}
    out = jnp.zeros((n_rows, row.shape[1]), F32)
    for i, name in enumerate(order):
        out = jnp.where(r_idx == i, parts[name], out)
    return out.astype(BF16)


def _lambda(lq1_ref, lk1_ref, lq2_ref, lk2_ref, lam_init):
    a = jnp.sum(lq1_ref[...] * lk1_ref[...], axis=1, keepdims=True)
    b = jnp.sum(lq2_ref[...] * lk2_ref[...], axis=1, keepdims=True)
    return jnp.exp(a) - jnp.exp(b) + lam_init


def _inproj_kernel(x_ref, wqt_ref, wvt_ref, wm_ref, wg_ref, wgt_ref, wd_ref, wgate_ref,
                   bcol_ref, brow_ref,
                   cos_ref, sin_ref,
                   q_ref, k_ref, v_ref, gcol_ref, grow_ref,
                   dq_ref, dkf_ref, dkb_ref, dvf_ref, dvt_ref, cq_ref, gate_ref,
                   *, d_m, d_d, dh_m, dqk, transposed_qv):
    x = x_ref[...].astype(BF16)

    def mm(w):
        return jnp.dot(x, w, preferred_element_type=F32)

    def mm_t(a, b):
        return lax.dot_general(a, b, NT_DIMS, preferred_element_type=F32)

    if transposed_qv:
        q_ref[0] = mm_t(wqt_ref[...], x).astype(q_ref.dtype)
        v_ref[0] = mm_t(wvt_ref[...], x).astype(v_ref.dtype)
    else:
        q_ref[...] = mm_t(x, wqt_ref[...]).astype(q_ref.dtype)
        v_ref[...] = mm_t(x, wvt_ref[...]).astype(v_ref.dtype)
    k_ref[...] = (mm(wm_ref[:, 0:d_m]) * (dh_m ** -0.5)).astype(k_ref.dtype)
    gate_ref[:, 0:d_m] = (jax.nn.sigmoid(mm(wm_ref[:, d_m:2 * d_m]))
                          * _silu(mm(wgate_ref[:, 0:d_m]))).astype(gate_ref.dtype)
    gate_ref[:, d_m:] = _silu(mm(wgate_ref[:, d_m:])).astype(gate_ref.dtype)
    gcol_ref[...] = mm(wg_ref[...]) + bcol_ref[...]
    grow_ref[...] = (lax.dot_general(wgt_ref[...], x, NT_DIMS, preferred_element_type=F32)
                     + brow_ref[:, 0:1])

    cos = cos_ref[...]
    sin = sin_ref[...]
    lane = lax.broadcasted_iota(jnp.int32, cos.shape, 1)
    first_half = (lane % dqk) < (dqk // 2)

    def rope(t):
        swapped = jnp.where(first_half, pltpu.roll(t, LANES - dqk // 2, 1),
                            pltpu.roll(t, dqk // 2, 1))
        return t * cos + swapped * sin

    tm = x.shape[0]
    dq = mm(wd_ref[:, 0:d_d])
    dk = mm(wd_ref[:, d_d:2 * d_d])
    dv = mm(wd_ref[:, 2 * d_d:3 * d_d])
    dvt_ref[0] = dv.T.astype(dvt_ref.dtype)
    for h in range(H_D):
        sl = slice(h * LANES, (h + 1) * LANES)
        dq_ref[:, sl] = (rope(dq[:, sl]) * (dqk ** -0.5)).astype(dq_ref.dtype)
        dk_h = rope(dk[:, sl])
        dkb_ref[:, sl] = dk_h.astype(dkb_ref.dtype)
        dkf_ref[pl.ds(h, tm, stride=H_D), :] = dk_h
        dvf_ref[pl.ds(h, tm, stride=H_D), :] = dv[:, sl]
    cq_ref[...] = mm(wd_ref[:, 3 * d_d:4 * d_d]).astype(cq_ref.dtype)


def _input_projection(x2d, wts, cos_tab, sin_tab, tm, qkv_dtype, transposed_qv):
    m, d_model = x2d.shape
    wqt, wvt, wm, wg, wgt, wd, wgate, bcol, brow = wts
    d_m = wqt.shape[0]
    d_d = wd.shape[1] // 4
    d_mix = wgate.shape[1]
    dh_m = d_m // H_M
    dqk = d_d // H_D // 2
    grid = (m // tm,)

    def row_spec(n):
        return pl.BlockSpec((tm, n), lambda i: (i, 0))

    def whole(a):
        return pl.BlockSpec(a.shape, lambda i: (0,) * a.ndim, pipeline_mode=pl.Buffered(1))

    if transposed_qv:
        qv_shape = jax.ShapeDtypeStruct((m // tm, d_m, tm), qkv_dtype)
        qv_spec = pl.BlockSpec((1, d_m, tm), lambda i: (i, 0, 0))
    else:
        qv_shape = jax.ShapeDtypeStruct((m, d_m), qkv_dtype)
        qv_spec = row_spec(d_m)
    out_shape = (
        qv_shape,
        jax.ShapeDtypeStruct((m, d_m), qkv_dtype),
        qv_shape,
        jax.ShapeDtypeStruct((m, LANES), F32),
        jax.ShapeDtypeStruct((N_GATE, m), F32),
        jax.ShapeDtypeStruct((m, d_d), BF16),
        jax.ShapeDtypeStruct((m * H_D, LANES), F32),
        jax.ShapeDtypeStruct((m, d_d), BF16),
        jax.ShapeDtypeStruct((m * H_D, LANES), F32),
        jax.ShapeDtypeStruct((m // tm, d_d, tm), BF16),
        jax.ShapeDtypeStruct((m, d_d), BF16),
        jax.ShapeDtypeStruct((m, d_mix), BF16),
    )
    head_rows = pl.BlockSpec((tm * H_D, LANES), lambda i: (i, 0))
    out_specs = (
        qv_spec, row_spec(d_m), qv_spec, row_spec(LANES),
        pl.BlockSpec((N_GATE, tm), lambda i: (0, i)),
        row_spec(d_d), head_rows, row_spec(d_d), head_rows,
        pl.BlockSpec((1, d_d, tm), lambda i: (i, 0, 0)), row_spec(d_d),
        row_spec(d_mix),
    )
    n_tab = cos_tab.shape[0] // tm
    tab_spec = pl.BlockSpec((tm, LANES), lambda i: (i % n_tab, 0))
    in_specs = [row_spec(d_model), whole(wqt), whole(wvt), whole(wm), whole(wg), whole(wgt),
                whole(wd), whole(wgate), whole(bcol), whole(brow), tab_spec, tab_spec]
    return pl.pallas_call(
        functools.partial(_inproj_kernel, d_m=d_m, d_d=d_d, dh_m=dh_m, dqk=dqk,
                          transposed_qv=transposed_qv),
        grid=grid, in_specs=in_specs, out_specs=out_specs, out_shape=out_shape,
        compiler_params=_params(("parallel",)), name="input_projection",
    )(x2d, wqt, wvt, wm, wg, wgt, wd, wgate, bcol, brow, cos_tab, sin_tab)


def _mlstm_prompt_kernel(qt_ref, k_ref, vt_ref, gate_ref, grow_ref, gcol_ref, gb_ref,
                         h_ref, c_ref, n_ref, m_ref, ct_sc, *, dh):
    j = pl.program_id(1)

    @pl.when(j == 0)
    def _():
        ct_sc[...] = jnp.zeros_like(ct_sc)
        n_ref[...] = jnp.zeros_like(n_ref)
        m_ref[...] = jnp.zeros_like(m_ref)

    c = k_ref.shape[0]
    s_idx = lax.broadcasted_iota(jnp.int32, (c, c), 0)
    t_idx = lax.broadcasted_iota(jnp.int32, (c, c), 1)
    causal = s_idx <= t_idx
    lower = t_idx <= s_idx

    heads = []
    for h in range(H_M):
        sl = slice(h * dh, (h + 1) * dh)
        qt = qt_ref[sl, :]
        k = k_ref[:, sl]
        ig_row = grow_ref[h:h + 1, :]
        lf_row = _log_sigmoid(grow_ref[H_M + h:H_M + h + 1, :])
        ig_col = gcol_ref[:, h:h + 1]
        lf_col = _log_sigmoid(gcol_ref[:, H_M + h:H_M + h + 1])
        bcum_col = jnp.sum(jnp.where(lower, lf_row, 0.0), axis=1, keepdims=True)
        bcum_row = jnp.sum(jnp.where(causal, lf_col, 0.0), axis=0, keepdims=True)
        m_prev = m_ref[:, h:h + 1]
        dlog = jnp.where(causal, bcum_row - bcum_col + ig_col, -jnp.inf)
        inter = bcum_row + m_prev
        mt = jnp.maximum(jnp.max(dlog, axis=0, keepdims=True), inter)
        w_inter = jnp.exp(inter - mt)
        st = jnp.dot(k, qt, preferred_element_type=F32) * jnp.exp(dlog - mt)
        ct_old = ct_sc[h]
        n_old = n_ref[h:h + 1, :]
        cq = jnp.dot(ct_old.astype(BF16), qt, preferred_element_type=F32)
        nq2 = jnp.dot(_split_rows(n_old, ("hi", "lo")), qt, preferred_element_type=F32)
        m_new = mt[:, c - 1:c]
        b_last = bcum_row[:, c - 1:c]
        w_end = jnp.exp(b_last - bcum_row + ig_row - m_new)
        decay = jnp.exp(b_last + m_prev - m_new)
        heads.append(dict(sl=sl, k=k, mt=mt, w_inter=w_inter, st=st, ct_old=ct_old, n_old=n_old,
                          cq=cq, nq=nq2[0:1, :] + nq2[1:2, :], m_new=m_new, w_end=w_end,
                          decay=decay))

    for h, d in enumerate(heads):
        vt = vt_ref[d["sl"], :]
        num = jnp.dot(vt, d["st"].astype(BF16), preferred_element_type=F32) + d["w_inter"] * d["cq"]
        den = jnp.sum(d["st"], axis=0, keepdims=True) + d["w_inter"] * d["nq"]
        d["hc"] = num * (1.0 / jnp.maximum(jnp.abs(den), jnp.exp(-d["mt"])))
        vtw = (vt.astype(F32) * d["w_end"]).astype(BF16)
        ct_sc[h] = d["decay"] * d["ct_old"] + jnp.dot(vtw, d["k"], preferred_element_type=F32)
        kw2 = jnp.dot(_split_rows(d["w_end"], ("hi", "lo")), d["k"], preferred_element_type=F32)
        n_ref[h:h + 1, :] = d["decay"] * d["n_old"] + kw2[0:1, :] + kw2[1:2, :]
        m_ref[:, h:h + 1] = d["m_new"]

    for h, d in enumerate(heads):
        sl = d["sl"]
        hc = d["hc"]
        mu = jnp.mean(hc, axis=0, keepdims=True)
        var = jnp.mean(jnp.square(hc - mu), axis=0, keepdims=True)
        hn = ((hc - mu) * lax.rsqrt(var + EPS) * gb_ref[sl, :]).T
        h_ref[:, sl] = (hn * gate_ref[:, sl]).astype(h_ref.dtype)

    @pl.when(j == pl.num_programs(1) - 1)
    def _():
        for h in range(H_M):
            c_ref[h] = ct_sc[h].T


def _mlstm_prompt(q_t, k, v_t, gate, grow, gcol, g_norm, b, l):
    d_m = k.shape[1]
    dh = d_m // H_M
    nc = l // CHUNK
    tm = q_t.shape[2]
    per = tm // CHUNK
    tok = pl.BlockSpec((CHUNK, d_m), lambda i, j: (i * nc + j, 0))
    tok_t = pl.BlockSpec((None, d_m, CHUNK), lambda i, j: ((i * nc + j) // per, 0, j % per))
    g_bcast = jnp.broadcast_to(g_norm.reshape(d_m, 1), (d_m, CHUNK))
    in_specs = [tok_t, tok, tok_t, tok,
                pl.BlockSpec((N_GATE, CHUNK), lambda i, j: (0, i * nc + j)),
                pl.BlockSpec((CHUNK, LANES), lambda i, j: (i * nc + j, 0)),
                pl.BlockSpec((d_m, CHUNK), lambda i, j: (0, 0))]
    out_shape = (jax.ShapeDtypeStruct((b * l, d_m), BF16),
                 jax.ShapeDtypeStruct((b, H_M, dh, dh), F32),
                 jax.ShapeDtypeStruct((b, H_M, dh), F32),
                 jax.ShapeDtypeStruct((b, 1, LANES), F32))
    out_specs = (tok,
                 pl.BlockSpec((None, H_M, dh, dh), lambda i, j: (i, 0, 0, 0)),
                 pl.BlockSpec((None, H_M, dh), lambda i, j: (i, 0, 0)),
                 pl.BlockSpec((None, 1, LANES), lambda i, j: (i, 0, 0)))
    return pl.pallas_call(
        functools.partial(_mlstm_prompt_kernel, dh=dh),
        grid=(b, nc), in_specs=in_specs, out_specs=out_specs, out_shape=out_shape,
        scratch_shapes=[pltpu.VMEM((H_M, dh, dh), F32)],
        compiler_params=_params(("parallel", "arbitrary")), name="mlstm_prompt",
    )(q_t, k, v_t, gate, grow, gcol, g_bcast)


def _diff_prompt_kernel(lq1_ref, lk1_ref, lq2_ref, lk2_ref, q_ref, k_ref, vt_ref, gate_ref, g_ref,
                        o_ref, m_sc, l_sc, acc_sc, *, lam_init, dqk):
    tq = q_ref.shape[0]
    dv = 2 * dqk
    qi = pl.program_id(1)
    lam = _lambda(lq1_ref, lk1_ref, lq2_ref, lk2_ref, lam_init)
    lane = lax.broadcasted_iota(jnp.int32, (tq, dv), 1)
    key_i = lax.broadcasted_iota(jnp.int32, (tq, 2 * tq), 0)
    qry_i = lax.broadcasted_iota(jnp.int32, (tq, 2 * tq), 1)
    visible = key_i <= jnp.where(qry_i >= tq, qry_i - tq, qry_i)
    ones = jnp.ones((2 * SUBLANES, tq), BF16)
    q_both = []
    for h in range(H_D):
        q = q_ref[:, h * dv:(h + 1) * dv]
        zero = jnp.zeros_like(q)
        q_both.append(jnp.concatenate([jnp.where(lane < dqk, q, zero),
                                       jnp.where(lane >= dqk, q, zero)], axis=0))
    m_sc[...] = jnp.full(m_sc.shape, -jnp.inf, F32)
    l_sc[...] = jnp.zeros_like(l_sc)
    acc_sc[...] = jnp.zeros_like(acc_sc)

    def key_span(first_tile, n_tiles, on_diagonal):
        start = pl.multiple_of(first_tile * tq, tq)
        scores = []
        for h in range(H_D):
            hs = slice(h * dv, (h + 1) * dv)
            k = k_ref[pl.ds(start, n_tiles * tq), hs]
            scores.append(lax.dot_general(k, q_both[h], NT_DIMS,
                                          preferred_element_type=F32))
        for h in range(H_D):
            hs = slice(h * dv, (h + 1) * dv)
            st = scores[h]
            if on_diagonal:
                st = jnp.where(visible, st, -jnp.inf)
            m_old = m_sc[h]
            m_new = jnp.maximum(m_old, jnp.max(st, axis=0, keepdims=True))
            alpha = jnp.exp(m_old - m_new)
            pt = jnp.exp(st - m_new).astype(BF16)
            pv = jnp.dot(jnp.concatenate([vt_ref[first_tile, hs, :], ones], axis=0), pt[0:tq],
                         preferred_element_type=F32)
            for t in range(1, n_tiles):
                pv += jnp.dot(jnp.concatenate([vt_ref[first_tile + t, hs, :], ones], axis=0),
                              pt[t * tq:(t + 1) * tq], preferred_element_type=F32)
            l_sc[h] = alpha * l_sc[h] + pv[dv:dv + 1, :]
            acc_sc[h] = alpha * acc_sc[h] + pv[0:dv, :]
            m_sc[h] = m_new

    def tile_pair(i, carry):
        key_span(2 * i, 2, False)
        return carry

    lax.fori_loop(0, qi // 2, tile_pair, 0)

    @pl.when(qi % 2 == 1)
    def _():
        key_span(qi - 1, 1, False)

    key_span(qi, 1, True)

    for h in range(H_D):
        hs = slice(h * dv, (h + 1) * dv)
        on = acc_sc[h] / l_sc[h]
        o = (on[:, 0:tq] - lam * on[:, tq:2 * tq]).T
        o = o * lax.rsqrt(jnp.mean(o * o, axis=1, keepdims=True) + EPS) * g_ref[...]
        o_ref[:, hs] = (o * (1.0 - lam_init) * gate_ref[:, hs]).astype(o_ref.dtype)


def _diff_prompt(lams, dq, dk, dv_t, gate, g_norm, b, l, lam_init, gate_col0):
    d_d = dq.shape[1]
    dv_d = d_d // H_D
    tq = dv_t.shape[2]
    nq = l // tq
    lam_spec = pl.BlockSpec(lams[0].shape, lambda i, j: (0, 0))
    tok = pl.BlockSpec((tq, d_d), lambda i, j: (i * nq + j, 0))
    in_specs = [lam_spec] * 4 + [
        tok,
        pl.BlockSpec((l, d_d), lambda i, j: (i, 0)),
        pl.BlockSpec((nq, d_d, tq), lambda i, j: (i, 0, 0)),
        pl.BlockSpec((tq, d_d), lambda i, j: (i * nq + j, gate_col0 // d_d)),
        pl.BlockSpec((1, dv_d), lambda i, j: (0, 0))]
    return pl.pallas_call(
        functools.partial(_diff_prompt_kernel, lam_init=lam_init, dqk=dv_d // 2),
        grid=(b, nq), in_specs=in_specs, out_specs=tok,
        out_shape=jax.ShapeDtypeStruct((b * l, d_d), BF16),
        scratch_shapes=[pltpu.VMEM((H_D, 1, 2 * tq), F32), pltpu.VMEM((H_D, 1, 2 * tq), F32),
                        pltpu.VMEM((H_D, dv_d, 2 * tq), F32)],
        compiler_params=_params(("parallel", "parallel")), name="diff_attn_prompt",
    )(*lams, dq, dk, dv_t, gate, g_norm)


def _mem_kv_kernel(x_ref, w_ref, kf_ref, kb_ref, vf_ref, vb_ref):
    tm, d_c = kb_ref.shape
    dh = d_c // H_C
    x = x_ref[...]
    mk = jnp.dot(x, w_ref[:, 0:d_c], preferred_element_type=F32)
    mv = jnp.dot(x, w_ref[:, d_c:2 * d_c], preferred_element_type=F32)
    kb_ref[...] = mk.astype(kb_ref.dtype)
    vb_ref[...] = mv.astype(vb_ref.dtype)
    for h in range(H_C):
        sl = slice(h * dh, (h + 1) * dh)
        kf_ref[pl.ds(h, tm, stride=H_C), :] = mk[:, sl]
        vf_ref[pl.ds(h, tm, stride=H_C), :] = mv[:, sl]


def _mem_kv(mem2d, w, tm):
    m, d_model = mem2d.shape
    d_c = w.shape[1] // 2
    dh = d_c // H_C
    row = pl.BlockSpec((tm, d_c), lambda i: (i, 0))
    head_rows = pl.BlockSpec((tm * H_C, dh), lambda i: (i, 0))
    return pl.pallas_call(
        _mem_kv_kernel, grid=(m // tm,),
        in_specs=[pl.BlockSpec((tm, d_model), lambda i: (i, 0)),
                  pl.BlockSpec(w.shape, lambda i: (0, 0))],
        out_specs=(head_rows, row, head_rows, row),
        out_shape=(jax.ShapeDtypeStruct((m * H_C, dh), F32), jax.ShapeDtypeStruct((m, d_c), BF16),
                   jax.ShapeDtypeStruct((m * H_C, dh), F32), jax.ShapeDtypeStruct((m, d_c), BF16)),
        compiler_params=_params(("parallel",)), name="memory_kv",
    )(mem2d, w)


def _mem_prompt_kernel(q_ref, k_ref, v_ref, gate_ref, o_ref, *, dh):
    slices = [slice(h * dh, (h + 1) * dh) for h in range(H_C)]
    scores = [lax.dot_general(q_ref[:, sl], k_ref[:, sl], NT_DIMS, preferred_element_type=F32)
              for sl in slices]
    probs = []
    for s in scores:
        s = s * (dh ** -0.5)
        p = jnp.exp(s - jnp.max(s, axis=1, keepdims=True))
        probs.append((p / jnp.sum(p, axis=1, keepdims=True)).astype(BF16))
    for p, sl in zip(probs, slices):
        o = jnp.dot(p, v_ref[:, sl], preferred_element_type=F32)
        o_ref[:, sl] = (o * gate_ref[:, sl]).astype(o_ref.dtype)


def _mem_prompt(cq, mk, mv, gate, b, l, n_mem, tl, gate_col0):
    d_c = cq.shape[1]
    nl = l // tl
    tok = pl.BlockSpec((tl, d_c), lambda i, j: (i * nl + j, 0))
    mem = pl.BlockSpec((n_mem, d_c), lambda i, j: (i, 0))
    return pl.pallas_call(
        functools.partial(_mem_prompt_kernel, dh=d_c // H_C),
        grid=(b, nl),
        in_specs=[tok, mem, mem,
                  pl.BlockSpec((tl, d_c), lambda i, j: (i * nl + j, gate_col0 // d_c))],
        out_specs=tok, out_shape=jax.ShapeDtypeStruct((b * l, d_c), BF16),
        compiler_params=_params(("parallel", "parallel")), name="memory_attn_prompt",
    )(cq, mk, mv, gate)


def _merge_kernel(x_ref, hm_ref, hd_ref, hc_ref, w_ref, g_ref, b_ref, y_ref, *, alpha):
    tm = x_ref.shape[0]
    n_strips = max(1, tm // 128)
    rows = tm // n_strips
    for r in range(n_strips):
        rs = slice(r * rows, (r + 1) * rows)
        h = jnp.concatenate([hm_ref[rs, :], hd_ref[rs, :], hc_ref[rs, :]], axis=1)
        y = alpha * x_ref[rs, :] + jnp.dot(h, w_ref[...], preferred_element_type=F32)
        mu = jnp.mean(y, axis=1, keepdims=True)
        var = jnp.mean(jnp.square(y - mu), axis=1, keepdims=True)
        y_ref[rs, :] = (y - mu) * lax.rsqrt(var + EPS) * g_ref[...] + b_ref[...]


def _merge(x2d, h_m, h_d, h_c, w_out, ln_g, ln_b, tm, alpha):
    m, d_model = x2d.shape

    def row(n):
        return pl.BlockSpec((tm, n), lambda i: (i, 0))

    def whole(a):
        return pl.BlockSpec(a.shape, lambda i: (0, 0))

    return pl.pallas_call(
        functools.partial(_merge_kernel, alpha=alpha), grid=(m // tm,),
        in_specs=[row(d_model), row(h_m.shape[1]), row(h_d.shape[1]), row(h_c.shape[1]),
                  whole(w_out), whole(ln_g), whole(ln_b)],
        out_specs=row(d_model), out_shape=jax.ShapeDtypeStruct((m, d_model), F32),
        compiler_params=_params(("parallel",)), name="output_projection",
    )(x2d, h_m, h_d, h_c, w_out, ln_g, ln_b)


def _mlstm_step_kernel(q_ref, k_ref, v_ref, gate_ref, ig_ref, fg_ref, m_ref, n_ref, c_ref,
                       g_ref, h_ref, cn_ref, nn_ref, mn_ref, *, dh):
    def one_sequence(b, carry):
        for h in range(H_M):
            sl = slice(h * dh, (h + 1) * dh)
            q = q_ref[b, :, sl]
            k = k_ref[b, :, sl]
            v = v_ref[b, :, sl]
            ig = ig_ref[b, h]
            lf = _log_sigmoid(fg_ref[b, h])
            m_prev = m_ref[b, h]
            n_old = n_ref[b, h]
            c_old = c_ref[b, h]

            inter = lf + m_prev
            mt = jnp.maximum(ig, inter)
            w_in = jnp.exp(ig - mt)
            w_st = jnp.exp(inter - mt)
            s = jnp.sum(q * k, axis=1, keepdims=True) * w_in
            qc2 = jnp.dot(_split_rows(q, ("hi", "lo")), c_old.astype(BF16),
                          preferred_element_type=F32)
            qc = qc2[0:1, :] + qc2[1:2, :]
            outer = lax.dot_general(_split_rows(k, ("hi", "hi", "lo")),
                                    _split_rows(v, ("hi", "lo", "hi")), TN_DIMS,
                                    preferred_element_type=F32)
            num = s * v + w_st * qc
            den = s + w_st * jnp.sum(q * n_old, axis=1, keepdims=True)
            hc = num / jnp.maximum(jnp.abs(den), jnp.exp(-mt))

            cn_ref[b, h] = w_st * c_old + w_in * outer
            nn_ref[b, h] = w_st * n_old + w_in * k
            mn_ref[b, h] = mt

            mu = jnp.mean(hc, axis=1, keepdims=True)
            var = jnp.mean(jnp.square(hc - mu), axis=1, keepdims=True)
            hn = (hc - mu) * lax.rsqrt(var + EPS) * g_ref[:, sl]
            h_ref[b, :, sl] = (hn * gate_ref[b, :, sl]).astype(h_ref.dtype)
        return carry

    lax.fori_loop(0, q_ref.shape[0], one_sequence, 0)


def _mlstm_step(q, k, v, gate, ig, fg, m0, n0, c0, g_norm, tb):
    bd, _, d_m = q.shape
    dh = d_m // H_M

    def blk(shape):
        nd = len(shape)
        return pl.BlockSpec((tb,) + tuple(shape), lambda i: (i,) + (0,) * nd)

    tok = blk((1, d_m))
    one = blk((H_M, 1, 1))
    in_specs = [tok, tok, tok, tok, one, one, one, blk((H_M, 1, dh)), blk((H_M, dh, dh)),
                pl.BlockSpec((1, d_m), lambda i: (0, 0))]
    out_shape = (jax.ShapeDtypeStruct((bd, 1, d_m), BF16),
                 jax.ShapeDtypeStruct((bd, H_M, dh, dh), F32),
                 jax.ShapeDtypeStruct((bd, H_M, 1, dh), F32),
                 jax.ShapeDtypeStruct((bd, H_M, 1, 1), F32))
    out_specs = (tok, blk((H_M, dh, dh)), blk((H_M, 1, dh)), one)
    return pl.pallas_call(
        functools.partial(_mlstm_step_kernel, dh=dh), grid=(bd // tb,),
        in_specs=in_specs, out_specs=out_specs, out_shape=out_shape,
        compiler_params=_params(("parallel",)), name="mlstm_step",
    )(q, k, v, gate, ig, fg, m0, n0, c0, g_norm)


def _head_rows(row, n_heads, reps):
    dh = row.shape[1] // n_heads
    heads = [row[:, h * dh:(h + 1) * dh] for h in range(n_heads)]
    return jnp.concatenate(heads * reps, axis=0)


def _own_head(n_rows, n_cols, n_heads):
    r = lax.broadcasted_iota(jnp.int32, (n_rows, n_cols), 0)
    c = lax.broadcasted_iota(jnp.int32, (n_rows, n_cols), 1)
    return (c % n_heads) == (r % n_heads)


def _diff_decode_kernel(pt_ref, lq1_ref, lk1_ref, lq2_ref, lk2_ref, q_ref, kn_ref, vn_ref,
                        gate_ref, g_ref, *rest, n_pages, lam_init):
    del pt_ref
    tb = q_ref.shape[0]
    k_refs = rest[:tb * n_pages]
    v_refs = rest[tb * n_pages:2 * tb * n_pages]
    o_ref = rest[2 * tb * n_pages]
    n_rows, dv_d = k_refs[0].shape
    dqk = dv_d // 2
    lam = _lambda(lq1_ref, lk1_ref, lq2_ref, lk2_ref, lam_init)
    r_idx = lax.broadcasted_iota(jnp.int32, (2 * H_D, dv_d), 0)
    l_idx = lax.broadcasted_iota(jnp.int32, (2 * H_D, dv_d), 1)
    own_map = (l_idx // dqk) == (r_idx // H_D)
    own = _own_head(2 * H_D, n_rows, H_D)

    q8s, scores = [], []
    for b in range(tb):
        q8 = jnp.where(own_map, _head_rows(q_ref[b].astype(F32), H_D, 2), 0.0)
        q8b = q8.astype(BF16)
        q8s.append(q8)
        scores.append([jnp.where(own, lax.dot_general(q8b, k_refs[b * n_pages + j][...].astype(BF16),
                                                      NT_DIMS, preferred_element_type=F32), -jnp.inf)
                       for j in range(n_pages)])

    weights = []
    for b in range(tb):
        kn4 = kn_ref[b]
        s_new = jnp.sum(q8s[b] * jnp.concatenate([kn4, kn4], axis=0), axis=1, keepdims=True)
        mx = s_new
        for s in scores[b]:
            mx = jnp.maximum(mx, jnp.max(s, axis=1, keepdims=True))
        probs = [jnp.exp(s - mx) for s in scores[b]]
        p_new = jnp.exp(s_new - mx)
        total = p_new
        for p in probs:
            total = total + jnp.sum(p, axis=1, keepdims=True)
        inv = 1.0 / total

        def combine(p, inv=inv):
            pn = p * inv
            return pn - lam * pltpu.roll(pn, SUBLANES - H_D, 0)

        a_new = combine(jnp.broadcast_to(p_new, (SUBLANES, LANES)))[:, 0:1]
        weights.append((a_new, [combine(p).astype(BF16) for p in probs]))

    for b in range(tb):
        a_new, a_pages = weights[b]
        vn4 = vn_ref[b]
        acc = a_new * jnp.concatenate([vn4, vn4], axis=0)
        for j in range(n_pages):
            acc = acc + jnp.dot(a_pages[j], v_refs[b * n_pages + j][...].astype(BF16),
                                preferred_element_type=F32)
        for h in range(H_D):
            sl = slice(h * dv_d, (h + 1) * dv_d)
            o = acc[h:h + 1, :]
            o = o * lax.rsqrt(jnp.mean(o * o, axis=1, keepdims=True) + EPS) * g_ref[...]
            o_ref[b, :, sl] = (o * (1.0 - lam_init) * gate_ref[b, :, sl]).astype(o_ref.dtype)


def _diff_decode(page_table, lams, dq, dk_new, dv_new, gate, g_norm, cache_k, cache_v,
                 lam_init, gate_col0, tb):
    bd, _, d_d = dq.shape
    n_pages = page_table.shape[1]
    _, n_rows, dv_d = cache_k.shape
    lam_spec = pl.BlockSpec(lams[0].shape, lambda i, pt: (0, 0))
    tok = pl.BlockSpec((tb, 1, d_d), lambda i, pt: (i, 0, 0))
    page_specs = [pl.BlockSpec((None, n_rows, dv_d),
                               lambda i, pt, b=b, j=j: (pt[i * tb + b, j], 0, 0))
                  for b in range(tb) for j in range(n_pages)]
    new_tok = pl.BlockSpec((tb, H_D, dv_d), lambda i, pt: (i, 0, 0))
    in_specs = ([lam_spec] * 4 + [tok, new_tok, new_tok,
                                  pl.BlockSpec((tb, 1, d_d), lambda i, pt: (i, 0, gate_col0 // d_d)),
                                  pl.BlockSpec((1, dv_d), lambda i, pt: (0, 0))]
                + page_specs + page_specs)
    grid_spec = pltpu.PrefetchScalarGridSpec(
        num_scalar_prefetch=1, grid=(bd // tb,), in_specs=in_specs, out_specs=tok)
    n_ops = tb * n_pages
    return pl.pallas_call(
        functools.partial(_diff_decode_kernel, n_pages=n_pages, lam_init=lam_init),
        grid_spec=grid_spec, out_shape=jax.ShapeDtypeStruct((bd, 1, d_d), BF16),
        compiler_params=_params(("parallel",)), name="diff_attn_decode",
    )(page_table, *lams, dq, dk_new, dv_new, gate, g_norm,
      *([cache_k] * n_ops), *([cache_v] * n_ops))


def _mem_decode_kernel(q_ref, k_ref, v_ref, gate_ref, o_ref):
    tb, n_rows, dh = k_ref.shape
    own = _own_head(SUBLANES, n_rows, H_C)
    scores = []
    for b in range(tb):
        q8 = _head_rows(q_ref[b].astype(F32), H_C, SUBLANES // H_C).astype(BF16)
        scores.append(lax.dot_general(q8, k_ref[b].astype(BF16), NT_DIMS,
                                      preferred_element_type=F32))
    probs = []
    for s in scores:
        s = jnp.where(own, s * (dh ** -0.5), -jnp.inf)
        p = jnp.exp(s - jnp.max(s, axis=1, keepdims=True))
        probs.append((p / jnp.sum(p, axis=1, keepdims=True)).astype(BF16))
    for b in range(tb):
        acc = jnp.dot(probs[b], v_ref[b].astype(BF16), preferred_element_type=F32)
        for h in range(H_C):
            sl = slice(h * dh, (h + 1) * dh)
            o_ref[b, :, sl] = (acc[h:h + 1, :] * gate_ref[b, :, sl]).astype(o_ref.dtype)


def _mem_decode(cq, mem_k, mem_v, gate, gate_col0, tb):
    bd, _, d_c = cq.shape
    _, n_rows, dh = mem_k.shape
    tok = pl.BlockSpec((tb, 1, d_c), lambda i: (i, 0, 0))
    mem = pl.BlockSpec((tb, n_rows, dh), lambda i: (i, 0, 0))
    return pl.pallas_call(
        _mem_decode_kernel, grid=(bd // tb,),
        in_specs=[tok, mem, mem, pl.BlockSpec((tb, 1, d_c), lambda i: (i, 0, gate_col0 // d_c))],
        out_specs=tok, out_shape=jax.ShapeDtypeStruct((bd, 1, d_c), BF16),
        compiler_params=_params(("parallel",)), name="memory_attn_decode",
    )(cq, mem_k, mem_v, gate)


def _rope_tables(pos, n_rows, dqk):
    half = dqk // 2
    inv_freq = ROPE_THETA ** (-jnp.arange(half, dtype=F32) / half)
    ang = pos.astype(F32)[:, None] * inv_freq[None, :]
    cos = jnp.cos(ang)
    sin = jnp.sin(ang)
    reps = LANES // dqk
    cos_t = jnp.tile(jnp.concatenate([cos, cos], axis=1), (1, reps))
    sin_t = jnp.tile(jnp.concatenate([-sin, sin], axis=1), (1, reps))
    return (jnp.broadcast_to(cos_t, (n_rows, LANES)), jnp.broadcast_to(sin_t, (n_rows, LANES)))


def _split_weights(w_in_l, b_if_l, d_m, d_d, d_mix):
    g0 = 4 * d_m
    d0 = g0 + N_GATE
    wqt = w_in_l[:, 0:d_m].T.astype(BF16)
    wvt = w_in_l[:, 2 * d_m:3 * d_m].T.astype(BF16)
    wm = jnp.concatenate([w_in_l[:, d_m:2 * d_m], w_in_l[:, 3 * d_m:g0]], axis=1).astype(BF16)
    w_gates = w_in_l[:, g0:d0]
    wg = jnp.pad(w_gates, ((0, 0), (0, LANES - N_GATE))).astype(BF16)
    wgt = w_gates.T.astype(BF16)
    wd = w_in_l[:, d0:d0 + 4 * d_d].astype(BF16)
    wgate = w_in_l[:, d0 + 4 * d_d:].astype(BF16)
    bcol = jnp.pad(b_if_l, (0, LANES - N_GATE)).reshape(1, LANES).astype(F32)
    brow = jnp.broadcast_to(b_if_l.astype(F32)[:, None], (N_GATE, LANES))
    return wqt, wvt, wm, wg, wgt, wd, wgate, bcol, brow


def kernel(x_prompt, x_sample, state_mlstm_c, state_mlstm_n, state_mlstm_m, cache_diff_k, cache_diff_v,
           cache_mem_k, cache_mem_v, page_table, mem_prompt, w_in, b_if, w_mlstm_norm,
           lambda_q1, lambda_k1, lambda_q2, lambda_k2, w_diff_norm, w_mem_kv, w_out, ln_g, ln_b):
    depth = w_in.shape[0]
    assert depth == 1, "single-layer trunk"
    bp, lp, d_model = x_prompt.shape
    bd, ls, _ = x_sample.shape
    assert ls == 1
    d_mix = w_out.shape[1]
    d_m = d_mix // 2
    d_d = d_mix // 4
    d_c = d_mix // 4
    dh_m = d_m // H_M
    dv_d = d_d // H_D
    dqk = dv_d // 2
    assert dv_d == LANES and d_c // H_C == LANES, "one attention head per lane tile"
    n_mem = mem_prompt.shape[1]
    n_pool, page = cache_diff_k.shape[1], cache_diff_k.shape[2]
    past_len = page_table.shape[1] * page
    alpha = (2 * depth) ** 0.25
    layer = 0
    lam_init = 0.8 - 0.6 * math.exp(-0.3 * layer)

    wts = _split_weights(w_in[layer], b_if[layer], d_m, d_d, d_mix)
    lams = tuple(a[layer].reshape(1, dqk).astype(F32)
                 for a in (lambda_q1, lambda_k1, lambda_q2, lambda_k2))
    g_m = w_mlstm_norm[layer].reshape(1, d_m).astype(F32)
    g_d = w_diff_norm[layer].reshape(1, dv_d).astype(F32)
    w_out_b = w_out[layer].astype(BF16)
    g_ln = ln_g[layer].reshape(1, d_model).astype(F32)
    b_ln = ln_b[layer].reshape(1, d_model).astype(F32)
    gate_d0 = d_m
    gate_c0 = d_m + d_d

    mp = bp * lp
    xp2d = x_prompt.reshape(mp, d_model)
    cos_p, sin_p = _rope_tables(jnp.arange(lp), lp, dqk)
    (q_m, k_m, v_m, gcol, grow, dq, dk_f, dk_b, dv_f, dv_t, cq, gate) = _input_projection(
        xp2d, wts, cos_p, sin_p, tm=256, qkv_dtype=BF16, transposed_qv=True)
    h_m, c_p, n_p, m_p = _mlstm_prompt(q_m, k_m, v_m, gate, grow, gcol, g_m, bp, lp)
    h_d = _diff_prompt(lams, dq, dk_b, dv_t, gate, g_d, bp, lp, lam_init, gate_d0)
    mk_f, mk_b, mv_f, mv_b = _mem_kv(mem_prompt.reshape(bp * n_mem, d_model).astype(BF16),
                                     w_mem_kv[layer].astype(BF16), tm=512)
    h_c = _mem_prompt(cq, mk_b, mv_b, gate, bp, lp, n_mem, 512, gate_c0)
    y_p = _merge(xp2d, h_m, h_d, h_c, w_out_b, g_ln, b_ln, 512, alpha)

    xs2d = x_sample.reshape(bd, d_model)
    cos_s, sin_s = _rope_tables(jnp.full((1,), past_len), bd, dqk)
    (qs, ks, vs, gcol_s, _, dq_s, dk_s, _, dv_s, _, cq_s, gate_s) = _input_projection(
        xs2d, wts, cos_s, sin_s, tm=bd, qkv_dtype=F32, transposed_qv=False)

    def tok3(a):
        return a.reshape(bd, 1, a.shape[1])

    ig_s = gcol_s[:, 0:H_M].reshape(bd, H_M, 1, 1)
    fg_s = gcol_s[:, H_M:N_GATE].reshape(bd, H_M, 1, 1)
    hs_m, c_s, n_s, m_s = _mlstm_step(
        tok3(qs), tok3(ks), tok3(vs), tok3(gate_s), ig_s, fg_s,
        state_mlstm_m[layer].astype(F32).reshape(bd, H_M, 1, 1),
        state_mlstm_n[layer].astype(F32).reshape(bd, H_M, 1, dh_m),
        state_mlstm_c[layer].astype(F32), g_m, tb=4)
    hs_d = _diff_decode(page_table, lams, tok3(dq_s), dk_s.reshape(bd, H_D, dv_d),
                        dv_s.reshape(bd, H_D, dv_d), tok3(gate_s), g_d,
                        cache_diff_k[layer].reshape(n_pool, page * H_D, dv_d),
                        cache_diff_v[layer].reshape(n_pool, page * H_D, dv_d), lam_init, gate_d0,
                        tb=2)
    hs_c = _mem_decode(tok3(cq_s), cache_mem_k[layer].reshape(bd, n_mem * H_C, d_c // H_C),
                       cache_mem_v[layer].reshape(bd, n_mem * H_C, d_c // H_C), tok3(gate_s),
                       gate_c0, tb=8)
    y_s = _merge(xs2d, hs_m.reshape(bd, d_m), hs_d.reshape(bd, d_d), hs_c.reshape(bd, d_c),
                 w_out_b, g_ln, b_ln, bd, alpha)

    return (y_p.reshape(bp, lp, d_model), y_s.reshape(bd, ls, d_model),
            c_p[None], n_p[None], m_p[:, 0, 0:H_M][None],
            dk_f.reshape(1, bp, lp, H_D, dv_d), dv_f.reshape(1, bp, lp, H_D, dv_d),
            mk_f.reshape(1, bp, n_mem, H_C, d_c // H_C), mv_f.reshape(1, bp, n_mem, H_C, d_c // H_C),
            c_s[None], n_s.reshape(1, bd, H_M, dh_m), m_s.reshape(1, bd, H_M),
            dk_s.reshape(1, bd, ls, H_D, dv_d), dv_s.reshape(1, bd, ls, H_D, dv_d))
```

```python
import functools
import math

import jax
import jax.numpy as jnp
from jax import lax
from jax.experimental import pallas as pl
from jax.experimental.pallas import tpu as pltpu

F32 = jnp.float32
BF16 = jnp.bfloat16

H_M = 4
H_D = 4
H_C = 4
N_GATE = 2 * H_M
CHUNK = 128
ROPE_THETA = 10000.0
EPS = 1e-5
LANES = 128
SUBLANES = 8
VMEM_LIMIT = 56 * 1024 * 1024

NT_DIMS = (((1,), (1,)), ((), ()))
TN_DIMS = (((0,), (0,)), ((), ()))


def _params(semantics):
    return pltpu.CompilerParams(dimension_semantics=semantics, vmem_limit_bytes=VMEM_LIMIT)


def _log_sigmoid(x):
    return jnp.minimum(x, 0.0) - jnp.log1p(jnp.exp(-jnp.abs(x)))


def _silu(x):
    return x * jax.nn.sigmoid(x)


def _split_rows(row, order):
    n_rows = 2 * SUBLANES
    r_idx = lax.broadcasted_iota(jnp.int32, (n_rows, row.shape[1]), 0)
    hi = row.astype(BF16).astype(F32)
    parts = {"hi": hi, "lo": row - hi}
    out = jnp.zeros((n_rows, row.shape[1]), F32)
    for i, name in enumerate(order):
        out = jnp.where(r_idx == i, parts[name], out)
    return out.astype(BF16)


def _lambda(lq1_ref, lk1_ref, lq2_ref, lk2_ref, lam_init):
    a = jnp.sum(lq1_ref[...] * lk1_ref[...], axis=1, keepdims=True)
    b = jnp.sum(lq2_ref[...] * lk2_ref[...], axis=1, keepdims=True)
    return jnp.exp(a) - jnp.exp(b) + lam_init


def _inproj_kernel(x_ref, wqt_ref, wvt_ref, wm_ref, wg_ref, wgt_ref, wd_ref, wgate_ref,
                   bcol_ref, brow_ref,
                   cos_ref, sin_ref,
                   q_ref, k_ref, v_ref, gcol_ref, grow_ref,
                   dq_ref, dkf_ref, dkb_ref, dvf_ref, dvt_ref, cq_ref, gate_ref,
                   *, d_m, d_d, dh_m, dqk, transposed_qv):
    x = x_ref[...].astype(BF16)

    def mm(w):
        return jnp.dot(x, w, preferred_element_type=F32)

    def mm_t(a, b):
        return lax.dot_general(a, b, NT_DIMS, preferred_element_type=F32)

    if transposed_qv:
        q_ref[0] = mm_t(wqt_ref[...], x).astype(q_ref.dtype)
        v_ref[0] = mm_t(wvt_ref[...], x).astype(v_ref.dtype)
    else:
        q_ref[...] = mm_t(x, wqt_ref[...]).astype(q_ref.dtype)
        v_ref[...] = mm_t(x, wvt_ref[...]).astype(v_ref.dtype)
    k_ref[...] = (mm(wm_ref[:, 0:d_m]) * (dh_m ** -0.5)).astype(k_ref.dtype)
    gate_ref[:, 0:d_m] = (jax.nn.sigmoid(mm(wm_ref[:, d_m:2 * d_m]))
                          * _silu(mm(wgate_ref[:, 0:d_m]))).astype(gate_ref.dtype)
    gate_ref[:, d_m:] = _silu(mm(wgate_ref[:, d_m:])).astype(gate_ref.dtype)
    gcol_ref[...] = mm(wg_ref[...]) + bcol_ref[...]
    grow_ref[...] = (lax.dot_general(wgt_ref[...], x, NT_DIMS, preferred_element_type=F32)
                     + brow_ref[:, 0:1])

    cos = cos_ref[...]
    sin = sin_ref[...]
    lane = lax.broadcasted_iota(jnp.int32, cos.shape, 1)
    first_half = (lane % dqk) < (dqk // 2)

    def rope(t):
        swapped = jnp.where(first_half, pltpu.roll(t, LANES - dqk // 2, 1),
                            pltpu.roll(t, dqk // 2, 1))
        return t * cos + swapped * sin

    tm = x.shape[0]
    dq = mm(wd_ref[:, 0:d_d])
    dk = mm(wd_ref[:, d_d:2 * d_d])
    dv = mm(wd_ref[:, 2 * d_d:3 * d_d])
    dvt_ref[0] = dv.T.astype(dvt_ref.dtype)
    for h in range(H_D):
        sl = slice(h * LANES, (h + 1) * LANES)
        dq_ref[:, sl] = (rope(dq[:, sl]) * (dqk ** -0.5)).astype(dq_ref.dtype)
        dk_h = rope(dk[:, sl])
        dkb_ref[:, sl] = dk_h.astype(dkb_ref.dtype)
        dkf_ref[pl.ds(h, tm, stride=H_D), :] = dk_h
        dvf_ref[pl.ds(h, tm, stride=H_D), :] = dv[:, sl]
    cq_ref[...] = mm(wd_ref[:, 3 * d_d:4 * d_d]).astype(cq_ref.dtype)


def _input_projection(x2d, wts, cos_tab, sin_tab, tm, qkv_dtype, transposed_qv):
    m, d_model = x2d.shape
    wqt, wvt, wm, wg, wgt, wd, wgate, bcol, brow = wts
    d_m = wqt.shape[0]
    d_d = wd.shape[1] // 4
    d_mix = wgate.shape[1]
    dh_m = d_m // H_M
    dqk = d_d // H_D // 2
    grid = (m // tm,)

    def row_spec(n):
        return pl.BlockSpec((tm, n), lambda i: (i, 0))

    def whole(a):
        return pl.BlockSpec(a.shape, lambda i: (0,) * a.ndim, pipeline_mode=pl.Buffered(1))

    if transposed_qv:
        qv_shape = jax.ShapeDtypeStruct((m // tm, d_m, tm), qkv_dtype)
        qv_spec = pl.BlockSpec((1, d_m, tm), lambda i: (i, 0, 0))
    else:
        qv_shape = jax.ShapeDtypeStruct((m, d_m), qkv_dtype)
        qv_spec = row_spec(d_m)
    out_shape = (
        qv_shape,
        jax.ShapeDtypeStruct((m, d_m), qkv_dtype),
        qv_shape,
        jax.ShapeDtypeStruct((m, LANES), F32),
        jax.ShapeDtypeStruct((N_GATE, m), F32),
        jax.ShapeDtypeStruct((m, d_d), BF16),
        jax.ShapeDtypeStruct((m * H_D, LANES), F32),
        jax.ShapeDtypeStruct((m, d_d), BF16),
        jax.ShapeDtypeStruct((m * H_D, LANES), F32),
        jax.ShapeDtypeStruct((m // tm, d_d, tm), BF16),
        jax.ShapeDtypeStruct((m, d_d), BF16),
        jax.ShapeDtypeStruct((m, d_mix), BF16),
    )
    head_rows = pl.BlockSpec((tm * H_D, LANES), lambda i: (i, 0))
    out_specs = (
        qv_spec, row_spec(d_m), qv_spec, row_spec(LANES),
        pl.BlockSpec((N_GATE, tm), lambda i: (0, i)),
        row_spec(d_d), head_rows, row_spec(d_d), head_rows,
        pl.BlockSpec((1, d_d, tm), lambda i: (i, 0, 0)), row_spec(d_d),
        row_spec(d_mix),
    )
    n_tab = cos_tab.shape[0] // tm
    tab_spec = pl.BlockSpec((tm, LANES), lambda i: (i % n_tab, 0))
    in_specs = [row_spec(d_model), whole(wqt), whole(wvt), whole(wm), whole(wg), whole(wgt),
                whole(wd), whole(wgate), whole(bcol), whole(brow), tab_spec, tab_spec]
    return pl.pallas_call(
        functools.partial(_inproj_kernel, d_m=d_m, d_d=d_d, dh_m=dh_m, dqk=dqk,
                          transposed_qv=transposed_qv),
        grid=grid, in_specs=in_specs, out_specs=out_specs, out_shape=out_shape,
        compiler_params=_params(("parallel",)), name="input_projection",
    )(x2d, wqt, wvt, wm, wg, wgt, wd, wgate, bcol, brow, cos_tab, sin_tab)


def _mlstm_prompt_kernel(qt_ref, k_ref, vt_ref, gate_ref, grow_ref, gcol_ref, gb_ref,
                         h_ref, c_ref, n_ref, m_ref, ct_sc, *, dh, c):
    j = pl.program_id(1)

    @pl.when(j == 0)
    def _():
        ct_sc[...] = jnp.zeros_like(ct_sc)
        n_ref[...] = jnp.zeros_like(n_ref)
        m_ref[...] = jnp.zeros_like(m_ref)

    s_idx = lax.broadcasted_iota(jnp.int32, (c, c), 0)
    t_idx = lax.broadcasted_iota(jnp.int32, (c, c), 1)
    causal = s_idx <= t_idx
    lower = t_idx <= s_idx

    for cc in range(k_ref.shape[0] // c):
        _mlstm_chunk(slice(cc * c, (cc + 1) * c), causal, lower, qt_ref, k_ref, vt_ref, gate_ref,
                     grow_ref, gcol_ref, gb_ref, h_ref, n_ref, m_ref, ct_sc, dh=dh, c=c)

    @pl.when(j == pl.num_programs(1) - 1)
    def _():
        for h in range(H_M):
            c_ref[h] = ct_sc[h].T


def _mlstm_chunk(ts, causal, lower, qt_ref, k_ref, vt_ref, gate_ref, grow_ref, gcol_ref, gb_ref,
                 h_ref, n_ref, m_ref, ct_sc, *, dh, c):
    heads = []
    for h in range(H_M):
        sl = slice(h * dh, (h + 1) * dh)
        qt = qt_ref[sl, ts]
        k = k_ref[ts, sl]
        ig_row = grow_ref[h:h + 1, ts]
        lf_row = _log_sigmoid(grow_ref[H_M + h:H_M + h + 1, ts])
        ig_col = gcol_ref[ts, h:h + 1]
        lf_col = _log_sigmoid(gcol_ref[ts, H_M + h:H_M + h + 1])
        bcum_col = jnp.sum(jnp.where(lower, lf_row, 0.0), axis=1, keepdims=True)
        bcum_row = jnp.sum(jnp.where(causal, lf_col, 0.0), axis=0, keepdims=True)
        m_prev = m_ref[:, h:h + 1]
        dlog = jnp.where(causal, bcum_row - bcum_col + ig_col, -jnp.inf)
        inter = bcum_row + m_prev
        mt = jnp.maximum(jnp.max(dlog, axis=0, keepdims=True), inter)
        w_inter = jnp.exp(inter - mt)
        st = jnp.dot(k, qt, preferred_element_type=F32) * jnp.exp(dlog - mt)
        ct_old = ct_sc[h]
        n_old = n_ref[h:h + 1, :]
        cq = jnp.dot(ct_old.astype(BF16), qt, preferred_element_type=F32)
        nq2 = jnp.dot(_split_rows(n_old, ("hi", "lo")), qt, preferred_element_type=F32)
        m_new = mt[:, c - 1:c]
        b_last = bcum_row[:, c - 1:c]
        w_end = jnp.exp(b_last - bcum_row + ig_row - m_new)
        decay = jnp.exp(b_last + m_prev - m_new)
        heads.append(dict(sl=sl, k=k, mt=mt, w_inter=w_inter, st=st, ct_old=ct_old, n_old=n_old,
                          cq=cq, nq=nq2[0:1, :] + nq2[1:2, :], m_new=m_new, w_end=w_end,
                          decay=decay))

    for h, d in enumerate(heads):
        vt = vt_ref[d["sl"], ts]
        num = jnp.dot(vt, d["st"].astype(BF16), preferred_element_type=F32) + d["w_inter"] * d["cq"]
        den = jnp.sum(d["st"], axis=0, keepdims=True) + d["w_inter"] * d["nq"]
        d["hc"] = num * (1.0 / jnp.maximum(jnp.abs(den), jnp.exp(-d["mt"])))
        vtw = (vt.astype(F32) * d["w_end"]).astype(BF16)
        ct_sc[h] = d["decay"] * d["ct_old"] + jnp.dot(vtw, d["k"], preferred_element_type=F32)
        kw2 = jnp.dot(_split_rows(d["w_end"], ("hi", "lo")), d["k"], preferred_element_type=F32)
        n_ref[h:h + 1, :] = d["decay"] * d["n_old"] + kw2[0:1, :] + kw2[1:2, :]
        m_ref[:, h:h + 1] = d["m_new"]

    for h, d in enumerate(heads):
        sl = d["sl"]
        hc = d["hc"]
        mu = jnp.mean(hc, axis=0, keepdims=True)
        var = jnp.mean(jnp.square(hc - mu), axis=0, keepdims=True)
        hn = ((hc - mu) * lax.rsqrt(var + EPS) * gb_ref[sl, :]).T
        h_ref[ts, sl] = (hn * gate_ref[ts, sl]).astype(h_ref.dtype)


def _mlstm_prompt(q_t, k, v_t, gate, grow, gcol, g_norm, b, l):
    d_m = k.shape[1]
    dh = d_m // H_M
    nc = l // CHUNK
    tm = q_t.shape[2]
    per = tm // CHUNK
    ns = nc // per
    tok = pl.BlockSpec((tm, d_m), lambda i, j: (i * ns + j, 0))
    tok_t = pl.BlockSpec((None, d_m, tm), lambda i, j: (i * ns + j, 0, 0))
    g_bcast = jnp.broadcast_to(g_norm.reshape(d_m, 1), (d_m, CHUNK))
    in_specs = [tok_t, tok, tok_t, tok,
                pl.BlockSpec((N_GATE, tm), lambda i, j: (0, i * ns + j)),
                pl.BlockSpec((tm, LANES), lambda i, j: (i * ns + j, 0)),
                pl.BlockSpec((d_m, CHUNK), lambda i, j: (0, 0))]
    out_shape = (jax.ShapeDtypeStruct((b * l, d_m), BF16),
                 jax.ShapeDtypeStruct((b, H_M, dh, dh), F32),
                 jax.ShapeDtypeStruct((b, H_M, dh), F32),
                 jax.ShapeDtypeStruct((b, 1, LANES), F32))
    out_specs = (tok,
                 pl.BlockSpec((None, H_M, dh, dh), lambda i, j: (i, 0, 0, 0)),
                 pl.BlockSpec((None, H_M, dh), lambda i, j: (i, 0, 0)),
                 pl.BlockSpec((None, 1, LANES), lambda i, j: (i, 0, 0)))
    return pl.pallas_call(
        functools.partial(_mlstm_prompt_kernel, dh=dh, c=CHUNK),
        grid=(b, ns), in_specs=in_specs, out_specs=out_specs, out_shape=out_shape,
        scratch_shapes=[pltpu.VMEM((H_M, dh, dh), F32)],
        compiler_params=_params(("parallel", "arbitrary")), name="mlstm_prompt",
    )(q_t, k, v_t, gate, grow, gcol, g_bcast)


def _diff_prompt_kernel(lq1_ref, lk1_ref, lq2_ref, lk2_ref, q_ref, k_ref, vt_ref, gate_ref, g_ref,
                        o_ref, m_sc, l_sc, acc_sc, *, lam_init, dqk):
    tq = q_ref.shape[0]
    dv = 2 * dqk
    qi = pl.program_id(1)
    lam = _lambda(lq1_ref, lk1_ref, lq2_ref, lk2_ref, lam_init)
    lane = lax.broadcasted_iota(jnp.int32, (tq, dv), 1)
    key_i = lax.broadcasted_iota(jnp.int32, (tq, 2 * tq), 0)
    qry_i = lax.broadcasted_iota(jnp.int32, (tq, 2 * tq), 1)
    visible = key_i <= jnp.where(qry_i >= tq, qry_i - tq, qry_i)
    ones = jnp.ones((2 * SUBLANES, tq), BF16)
    q_both = []
    for h in range(H_D):
        q = q_ref[:, h * dv:(h + 1) * dv]
        zero = jnp.zeros_like(q)
        q_both.append(jnp.concatenate([jnp.where(lane < dqk, q, zero),
                                       jnp.where(lane >= dqk, q, zero)], axis=0))
    m_sc[...] = jnp.full(m_sc.shape, -jnp.inf, F32)
    l_sc[...] = jnp.zeros_like(l_sc)
    acc_sc[...] = jnp.zeros_like(acc_sc)

    def key_span(first_tile, n_tiles, on_diagonal):
        start = pl.multiple_of(first_tile * tq, tq)
        scores = []
        for h in range(H_D):
            hs = slice(h * dv, (h + 1) * dv)
            k = k_ref[pl.ds(start, n_tiles * tq), hs]
            scores.append(lax.dot_general(k, q_both[h], NT_DIMS,
                                          preferred_element_type=F32))
        for h in range(H_D):
            hs = slice(h * dv, (h + 1) * dv)
            st = scores[h]
            if on_diagonal:
                st = jnp.where(visible, st, -jnp.inf)
            m_old = m_sc[h]
            m_new = jnp.maximum(m_old, jnp.max(st, axis=0, keepdims=True))
            alpha = jnp.exp(m_old - m_new)
            pt = jnp.exp(st - m_new).astype(BF16)
            pv = jnp.dot(jnp.concatenate([vt_ref[first_tile, hs, :], ones], axis=0), pt[0:tq],
                         preferred_element_type=F32)
            for t in range(1, n_tiles):
                pv += jnp.dot(jnp.concatenate([vt_ref[first_tile + t, hs, :], ones], axis=0),
                              pt[t * tq:(t + 1) * tq], preferred_element_type=F32)
            l_sc[h] = alpha * l_sc[h] + pv[dv:dv + 1, :]
            acc_sc[h] = alpha * acc_sc[h] + pv[0:dv, :]
            m_sc[h] = m_new

    def tile_pair(i, carry):
        key_span(2 * i, 2, False)
        return carry

    lax.fori_loop(0, qi // 2, tile_pair, 0)

    @pl.when(qi % 2 == 1)
    def _():
        key_span(qi - 1, 1, False)

    key_span(qi, 1, True)

    for h in range(H_D):
        hs = slice(h * dv, (h + 1) * dv)
        on = acc_sc[h] / l_sc[h]
        o = (on[:, 0:tq] - lam * on[:, tq:2 * tq]).T
        o = o * lax.rsqrt(jnp.mean(o * o, axis=1, keepdims=True) + EPS) * g_ref[...]
        o_ref[:, hs] = (o * (1.0 - lam_init) * gate_ref[:, hs]).astype(o_ref.dtype)


def _diff_prompt(lams, dq, dk, dv_t, gate, g_norm, b, l, lam_init, gate_col0):
    d_d = dq.shape[1]
    dv_d = d_d // H_D
    tq = dv_t.shape[2]
    nq = l // tq
    lam_spec = pl.BlockSpec(lams[0].shape, lambda i, j: (0, 0))
    tok = pl.BlockSpec((tq, d_d), lambda i, j: (i * nq + j, 0))
    in_specs = [lam_spec] * 4 + [
        tok,
        pl.BlockSpec((l, d_d), lambda i, j: (i, 0)),
        pl.BlockSpec((nq, d_d, tq), lambda i, j: (i, 0, 0)),
        pl.BlockSpec((tq, d_d), lambda i, j: (i * nq + j, gate_col0 // d_d)),
        pl.BlockSpec((1, dv_d), lambda i, j: (0, 0))]
    return pl.pallas_call(
        functools.partial(_diff_prompt_kernel, lam_init=lam_init, dqk=dv_d // 2),
        grid=(b, nq), in_specs=in_specs, out_specs=tok,
        out_shape=jax.ShapeDtypeStruct((b * l, d_d), BF16),
        scratch_shapes=[pltpu.VMEM((H_D, 1, 2 * tq), F32), pltpu.VMEM((H_D, 1, 2 * tq), F32),
                        pltpu.VMEM((H_D, dv_d, 2 * tq), F32)],
        compiler_params=_params(("parallel", "parallel")), name="diff_attn_prompt",
    )(*lams, dq, dk, dv_t, gate, g_norm)


def _mem_kv_kernel(x_ref, w_ref, kf_ref, kb_ref, vf_ref, vb_ref):
    tm, d_c = kb_ref.shape
    dh = d_c // H_C
    x = x_ref[...]
    mk = jnp.dot(x, w_ref[:, 0:d_c], preferred_element_type=F32)
    mv = jnp.dot(x, w_ref[:, d_c:2 * d_c], preferred_element_type=F32)
    kb_ref[...] = mk.astype(kb_ref.dtype)
    vb_ref[...] = mv.astype(vb_ref.dtype)
    for h in range(H_C):
        sl = slice(h * dh, (h + 1) * dh)
        kf_ref[pl.ds(h, tm, stride=H_C), :] = mk[:, sl]
        vf_ref[pl.ds(h, tm, stride=H_C), :] = mv[:, sl]


def _mem_kv(mem2d, w, tm):
    m, d_model = mem2d.shape
    d_c = w.shape[1] // 2
    dh = d_c // H_C
    row = pl.BlockSpec((tm, d_c), lambda i: (i, 0))
    head_rows = pl.BlockSpec((tm * H_C, dh), lambda i: (i, 0))
    return pl.pallas_call(
        _mem_kv_kernel, grid=(m // tm,),
        in_specs=[pl.BlockSpec((tm, d_model), lambda i: (i, 0)),
                  pl.BlockSpec(w.shape, lambda i: (0, 0))],
        out_specs=(head_rows, row, head_rows, row),
        out_shape=(jax.ShapeDtypeStruct((m * H_C, dh), F32), jax.ShapeDtypeStruct((m, d_c), BF16),
                   jax.ShapeDtypeStruct((m * H_C, dh), F32), jax.ShapeDtypeStruct((m, d_c), BF16)),
        compiler_params=_params(("parallel",)), name="memory_kv",
    )(mem2d, w)


def _mem_prompt_kernel(q_ref, k_ref, v_ref, gate_ref, o_ref, *, dh):
    slices = [slice(h * dh, (h + 1) * dh) for h in range(H_C)]
    scores = [lax.dot_general(q_ref[:, sl], k_ref[:, sl], NT_DIMS, preferred_element_type=F32)
              for sl in slices]
    probs = []
    for s in scores:
        s = s * (dh ** -0.5)
        p = jnp.exp(s - jnp.max(s, axis=1, keepdims=True))
        probs.append((p / jnp.sum(p, axis=1, keepdims=True)).astype(BF16))
    for p, sl in zip(probs, slices):
        o = jnp.dot(p, v_ref[:, sl], preferred_element_type=F32)
        o_ref[:, sl] = (o * gate_ref[:, sl]).astype(o_ref.dtype)


def _mem_prompt(cq, mk, mv, gate, b, l, n_mem, tl, gate_col0):
    d_c = cq.shape[1]
    nl = l // tl
    tok = pl.BlockSpec((tl, d_c), lambda i, j: (i * nl + j, 0))
    mem = pl.BlockSpec((n_mem, d_c), lambda i, j: (i, 0))
    return pl.pallas_call(
        functools.partial(_mem_prompt_kernel, dh=d_c // H_C),
        grid=(b, nl),
        in_specs=[tok, mem, mem,
                  pl.BlockSpec((tl, d_c), lambda i, j: (i * nl + j, gate_col0 // d_c))],
        out_specs=tok, out_shape=jax.ShapeDtypeStruct((b * l, d_c), BF16),
        compiler_params=_params(("parallel", "parallel")), name="memory_attn_prompt",
    )(cq, mk, mv, gate)


def _merge_kernel(x_ref, hm_ref, hd_ref, hc_ref, w_ref, g_ref, b_ref, y_ref, *, alpha):
    tm = x_ref.shape[0]
    n_strips = max(1, tm // 128)
    rows = tm // n_strips
    for r in range(n_strips):
        rs = slice(r * rows, (r + 1) * rows)
        h = jnp.concatenate([hm_ref[rs, :], hd_ref[rs, :], hc_ref[rs, :]], axis=1)
        y = alpha * x_ref[rs, :] + jnp.dot(h, w_ref[...], preferred_element_type=F32)
        mu = jnp.mean(y, axis=1, keepdims=True)
        var = jnp.mean(jnp.square(y - mu), axis=1, keepdims=True)
        y_ref[rs, :] = (y - mu) * lax.rsqrt(var + EPS) * g_ref[...] + b_ref[...]


def _merge(x2d, h_m, h_d, h_c, w_out, ln_g, ln_b, tm, alpha):
    m, d_model = x2d.shape

    def row(n):
        return pl.BlockSpec((tm, n), lambda i: (i, 0))

    def whole(a):
        return pl.BlockSpec(a.shape, lambda i: (0, 0))

    return pl.pallas_call(
        functools.partial(_merge_kernel, alpha=alpha), grid=(m // tm,),
        in_specs=[row(d_model), row(h_m.shape[1]), row(h_d.shape[1]), row(h_c.shape[1]),
                  whole(w_out), whole(ln_g), whole(ln_b)],
        out_specs=row(d_model), out_shape=jax.ShapeDtypeStruct((m, d_model), F32),
        compiler_params=_params(("parallel",)), name="output_projection",
    )(x2d, h_m, h_d, h_c, w_out, ln_g, ln_b)


def _mlstm_step_kernel(q_ref, k_ref, v_ref, gate_ref, ig_ref, fg_ref, m_ref, n_ref, c_ref,
                       g_ref, h_ref, cn_ref, nn_ref, mn_ref, *, dh):
    def one_sequence(b, carry):
        for h in range(H_M):
            sl = slice(h * dh, (h + 1) * dh)
            q = q_ref[b, :, sl]
            k = k_ref[b, :, sl]
            v = v_ref[b, :, sl]
            ig = ig_ref[b, h]
            lf = _log_sigmoid(fg_ref[b, h])
            m_prev = m_ref[b, h]
            n_old = n_ref[b, h]
            c_old = c_ref[b, h]

            inter = lf + m_prev
            mt = jnp.maximum(ig, inter)
            w_in = jnp.exp(ig - mt)
            w_st = jnp.exp(inter - mt)
            s = jnp.sum(q * k, axis=1, keepdims=True) * w_in
            qc2 = jnp.dot(_split_rows(q, ("hi", "lo")), c_old.astype(BF16),
                          preferred_element_type=F32)
            qc = qc2[0:1, :] + qc2[1:2, :]
            outer = lax.dot_general(_split_rows(k, ("hi", "hi", "lo")),
                                    _split_rows(v, ("hi", "lo", "hi")), TN_DIMS,
                                    preferred_element_type=F32)
            num = s * v + w_st * qc
            den = s + w_st * jnp.sum(q * n_old, axis=1, keepdims=True)
            hc = num / jnp.maximum(jnp.abs(den), jnp.exp(-mt))

            cn_ref[b, h] = w_st * c_old + w_in * outer
            nn_ref[b, h] = w_st * n_old + w_in * k
            mn_ref[b, h] = mt

            mu = jnp.mean(hc, axis=1, keepdims=True)
            var = jnp.mean(jnp.square(hc - mu), axis=1, keepdims=True)
            hn = (hc - mu) * lax.rsqrt(var + EPS) * g_ref[:, sl]
            h_ref[b, :, sl] = (hn * gate_ref[b, :, sl]).astype(h_ref.dtype)
        return carry

    lax.fori_loop(0, q_ref.shape[0], one_sequence, 0)


def _mlstm_step(q, k, v, gate, ig, fg, m0, n0, c0, g_norm, tb):
    bd, _, d_m = q.shape
    dh = d_m // H_M

    def blk(shape):
        nd = len(shape)
        return pl.BlockSpec((tb,) + tuple(shape), lambda i: (i,) + (0,) * nd)

    tok = blk((1, d_m))
    one = blk((H_M, 1, 1))
    in_specs = [tok, tok, tok, tok, one, one, one, blk((H_M, 1, dh)), blk((H_M, dh, dh)),
                pl.BlockSpec((1, d_m), lambda i: (0, 0))]
    out_shape = (jax.ShapeDtypeStruct((bd, 1, d_m), BF16),
                 jax.ShapeDtypeStruct((bd, H_M, dh, dh), F32),
                 jax.ShapeDtypeStruct((bd, H_M, 1, dh), F32),
                 jax.ShapeDtypeStruct((bd, H_M, 1, 1), F32))
    out_specs = (tok, blk((H_M, dh, dh)), blk((H_M, 1, dh)), one)
    return pl.pallas_call(
        functools.partial(_mlstm_step_kernel, dh=dh), grid=(bd // tb,),
        in_specs=in_specs, out_specs=out_specs, out_shape=out_shape,
        compiler_params=_params(("parallel",)), name="mlstm_step",
    )(q, k, v, gate, ig, fg, m0, n0, c0, g_norm)


def _head_rows(row, n_heads, reps):
    dh = row.shape[1] // n_heads
    heads = [row[:, h * dh:(h + 1) * dh] for h in range(n_heads)]
    return jnp.concatenate(heads * reps, axis=0)


def _own_head(n_rows, n_cols, n_heads):
    r = lax.broadcasted_iota(jnp.int32, (n_rows, n_cols), 0)
    c = lax.broadcasted_iota(jnp.int32, (n_rows, n_cols), 1)
    return (c % n_heads) == (r % n_heads)


def _diff_decode_kernel(pt_ref, lq1_ref, lk1_ref, lq2_ref, lk2_ref, q_ref, kn_ref, vn_ref,
                        gate_ref, g_ref, *rest, n_pages, lam_init):
    del pt_ref
    tb = q_ref.shape[0]
    k_refs = rest[:tb * n_pages]
    v_refs = rest[tb * n_pages:2 * tb * n_pages]
    o_ref = rest[2 * tb * n_pages]
    n_rows, dv_d = k_refs[0].shape
    dqk = dv_d // 2
    lam = _lambda(lq1_ref, lk1_ref, lq2_ref, lk2_ref, lam_init)
    r_idx = lax.broadcasted_iota(jnp.int32, (2 * H_D, dv_d), 0)
    l_idx = lax.broadcasted_iota(jnp.int32, (2 * H_D, dv_d), 1)
    own_map = (l_idx // dqk) == (r_idx // H_D)
    own = _own_head(2 * H_D, n_rows, H_D)

    q8s, scores = [], []
    for b in range(tb):
        q8 = jnp.where(own_map, _head_rows(q_ref[b].astype(F32), H_D, 2), 0.0)
        q8b = q8.astype(BF16)
        q8s.append(q8)
        scores.append([jnp.where(own, lax.dot_general(q8b, k_refs[b * n_pages + j][...].astype(BF16),
                                                      NT_DIMS, preferred_element_type=F32), -jnp.inf)
                       for j in range(n_pages)])

    weights = []
    for b in range(tb):
        kn4 = kn_ref[b]
        s_new = jnp.sum(q8s[b] * jnp.concatenate([kn4, kn4], axis=0), axis=1, keepdims=True)
        mx = s_new
        for s in scores[b]:
            mx = jnp.maximum(mx, jnp.max(s, axis=1, keepdims=True))
        probs = [jnp.exp(s - mx) for s in scores[b]]
        p_new = jnp.exp(s_new - mx)
        total = p_new
        for p in probs:
            total = total + jnp.sum(p, axis=1, keepdims=True)
        inv = 1.0 / total

        def combine(p, inv=inv):
            pn = p * inv
            return pn - lam * pltpu.roll(pn, SUBLANES - H_D, 0)

        a_new = combine(jnp.broadcast_to(p_new, (SUBLANES, LANES)))[:, 0:1]
        weights.append((a_new, [combine(p).astype(BF16) for p in probs]))

    for b in range(tb):
        a_new, a_pages = weights[b]
        vn4 = vn_ref[b]
        acc = a_new * jnp.concatenate([vn4, vn4], axis=0)
        for j in range(n_pages):
            acc = acc + jnp.dot(a_pages[j], v_refs[b * n_pages + j][...].astype(BF16),
                                preferred_element_type=F32)
        for h in range(H_D):
            sl = slice(h * dv_d, (h + 1) * dv_d)
            o = acc[h:h + 1, :]
            o = o * lax.rsqrt(jnp.mean(o * o, axis=1, keepdims=True) + EPS) * g_ref[...]
            o_ref[b, :, sl] = (o * (1.0 - lam_init) * gate_ref[b, :, sl]).astype(o_ref.dtype)


def _diff_decode(page_table, lams, dq, dk_new, dv_new, gate, g_norm, cache_k, cache_v,
                 lam_init, gate_col0, tb):
    bd, _, d_d = dq.shape
    n_pages = page_table.shape[1]
    _, n_rows, dv_d = cache_k.shape
    lam_spec = pl.BlockSpec(lams[0].shape, lambda i, pt: (0, 0))
    tok = pl.BlockSpec((tb, 1, d_d), lambda i, pt: (i, 0, 0))
    page_specs = [pl.BlockSpec((None, n_rows, dv_d),
                               lambda i, pt, b=b, j=j: (pt[i * tb + b, j], 0, 0))
                  for b in range(tb) for j in range(n_pages)]
    new_tok = pl.BlockSpec((tb, H_D, dv_d), lambda i, pt: (i, 0, 0))
    in_specs = ([lam_spec] * 4 + [tok, new_tok, new_tok,
                                  pl.BlockSpec((tb, 1, d_d), lambda i, pt: (i, 0, gate_col0 // d_d)),
                                  pl.BlockSpec((1, dv_d), lambda i, pt: (0, 0))]
                + page_specs + page_specs)
    grid_spec = pltpu.PrefetchScalarGridSpec(
        num_scalar_prefetch=1, grid=(bd // tb,), in_specs=in_specs, out_specs=tok)
    n_ops = tb * n_pages
    return pl.pallas_call(
        functools.partial(_diff_decode_kernel, n_pages=n_pages, lam_init=lam_init),
        grid_spec=grid_spec, out_shape=jax.ShapeDtypeStruct((bd, 1, d_d), BF16),
        compiler_params=_params(("parallel",)), name="diff_attn_decode",
    )(page_table, *lams, dq, dk_new, dv_new, gate, g_norm,
      *([cache_k] * n_ops), *([cache_v] * n_ops))


def _mem_decode_kernel(q_ref, k_ref, v_ref, gate_ref, o_ref):
    tb, n_rows, dh = k_ref.shape
    own = _own_head(SUBLANES, n_rows, H_C)
    scores = []
    for b in range(tb):
        q8 = _head_rows(q_ref[b].astype(F32), H_C, SUBLANES // H_C).astype(BF16)
        scores.append(lax.dot_general(q8, k_ref[b].astype(BF16), NT_DIMS,
                                      preferred_element_type=F32))
    probs = []
    for s in scores:
        s = jnp.where(own, s * (dh ** -0.5), -jnp.inf)
        p = jnp.exp(s - jnp.max(s, axis=1, keepdims=True))
        probs.append((p / jnp.sum(p, axis=1, keepdims=True)).astype(BF16))
    for b in range(tb):
        acc = jnp.dot(probs[b], v_ref[b].astype(BF16), preferred_element_type=F32)
        for h in range(H_C):
            sl = slice(h * dh, (h + 1) * dh)
            o_ref[b, :, sl] = (acc[h:h + 1, :] * gate_ref[b, :, sl]).astype(o_ref.dtype)


def _mem_decode(cq, mem_k, mem_v, gate, gate_col0, tb):
    bd, _, d_c = cq.shape
    _, n_rows, dh = mem_k.shape
    tok = pl.BlockSpec((tb, 1, d_c), lambda i: (i, 0, 0))
    mem = pl.BlockSpec((tb, n_rows, dh), lambda i: (i, 0, 0))
    return pl.pallas_call(
        _mem_decode_kernel, grid=(bd // tb,),
        in_specs=[tok, mem, mem, pl.BlockSpec((tb, 1, d_c), lambda i: (i, 0, gate_col0 // d_c))],
        out_specs=tok, out_shape=jax.ShapeDtypeStruct((bd, 1, d_c), BF16),
        compiler_params=_params(("parallel",)), name="memory_attn_decode",
    )(cq, mem_k, mem_v, gate)


def _rope_tables(pos, n_rows, dqk):
    half = dqk // 2
    lane = jnp.arange(LANES)
    inv_freq = ROPE_THETA ** (-(lane % half).astype(F32) / half)
    ang = pos.astype(F32)[:, None] * inv_freq[None, :]
    sin_t = jnp.where((lane % dqk) < half, -jnp.sin(ang), jnp.sin(ang))
    return (jnp.broadcast_to(jnp.cos(ang), (n_rows, LANES)), jnp.broadcast_to(sin_t, (n_rows, LANES)))


def _split_weights(w_in_l, b_if_l, d_m, d_d, d_mix):
    g0 = 4 * d_m
    d0 = g0 + N_GATE
    wqt = w_in_l[:, 0:d_m].T.astype(BF16)
    wvt = w_in_l[:, 2 * d_m:3 * d_m].T.astype(BF16)
    wm = jnp.concatenate([w_in_l[:, d_m:2 * d_m], w_in_l[:, 3 * d_m:g0]], axis=1).astype(BF16)
    w_gates = w_in_l[:, g0:d0]
    wg = jnp.pad(w_gates, ((0, 0), (0, LANES - N_GATE))).astype(BF16)
    wgt = w_gates.T.astype(BF16)
    wd = w_in_l[:, d0:d0 + 4 * d_d].astype(BF16)
    wgate = w_in_l[:, d0 + 4 * d_d:].astype(BF16)
    bcol = jnp.pad(b_if_l, (0, LANES - N_GATE)).reshape(1, LANES).astype(F32)
    brow = jnp.broadcast_to(b_if_l.astype(F32)[:, None], (N_GATE, LANES))
    return wqt, wvt, wm, wg, wgt, wd, wgate, bcol, brow


def kernel(x_prompt, x_sample, state_mlstm_c, state_mlstm_n, state_mlstm_m, cache_diff_k, cache_diff_v,
           cache_mem_k, cache_mem_v, page_table, mem_prompt, w_in, b_if, w_mlstm_norm,
           lambda_q1, lambda_k1, lambda_q2, lambda_k2, w_diff_norm, w_mem_kv, w_out, ln_g, ln_b):
    depth = w_in.shape[0]
    assert depth == 1, "single-layer trunk"
    bp, lp, d_model = x_prompt.shape
    bd, ls, _ = x_sample.shape
    assert ls == 1
    d_mix = w_out.shape[1]
    d_m = d_mix // 2
    d_d = d_mix // 4
    d_c = d_mix // 4
    dh_m = d_m // H_M
    dv_d = d_d // H_D
    dqk = dv_d // 2
    assert dv_d == LANES and d_c // H_C == LANES, "one attention head per lane tile"
    n_mem = mem_prompt.shape[1]
    n_pool, page = cache_diff_k.shape[1], cache_diff_k.shape[2]
    past_len = page_table.shape[1] * page
    alpha = (2 * depth) ** 0.25
    layer = 0
    lam_init = 0.8 - 0.6 * math.exp(-0.3 * layer)

    wts = _split_weights(w_in[layer], b_if[layer], d_m, d_d, d_mix)
    lams = tuple(a[layer].reshape(1, dqk).astype(F32)
                 for a in (lambda_q1, lambda_k1, lambda_q2, lambda_k2))
    g_m = w_mlstm_norm[layer].reshape(1, d_m).astype(F32)
    g_d = w_diff_norm[layer].reshape(1, dv_d).astype(F32)
    w_out_b = w_out[layer].astype(BF16)
    g_ln = ln_g[layer].reshape(1, d_model).astype(F32)
    b_ln = ln_b[layer].reshape(1, d_model).astype(F32)
    gate_d0 = d_m
    gate_c0 = d_m + d_d

    mp = bp * lp
    xp2d = x_prompt.reshape(mp, d_model)
    cos_p, sin_p = _rope_tables(jnp.arange(lp), lp, dqk)
    (q_m, k_m, v_m, gcol, grow, dq, dk_f, dk_b, dv_f, dv_t, cq, gate) = _input_projection(
        xp2d, wts, cos_p, sin_p, tm=256, qkv_dtype=BF16, transposed_qv=True)
    h_m, c_p, n_p, m_p = _mlstm_prompt(q_m, k_m, v_m, gate, grow, gcol, g_m, bp, lp)
    h_d = _diff_prompt(lams, dq, dk_b, dv_t, gate, g_d, bp, lp, lam_init, gate_d0)
    mk_f, mk_b, mv_f, mv_b = _mem_kv(mem_prompt.reshape(bp * n_mem, d_model).astype(BF16),
                                     w_mem_kv[layer].astype(BF16), tm=512)
    h_c = _mem_prompt(cq, mk_b, mv_b, gate, bp, lp, n_mem, 512, gate_c0)
    y_p = _merge(xp2d, h_m, h_d, h_c, w_out_b, g_ln, b_ln, 1024, alpha)

    xs2d = x_sample.reshape(bd, d_model)
    cos_s, sin_s = _rope_tables(jnp.full((1,), past_len), bd, dqk)
    (qs, ks, vs, gcol_s, _, dq_s, dk_s, _, dv_s, _, cq_s, gate_s) = _input_projection(
        xs2d, wts, cos_s, sin_s, tm=bd, qkv_dtype=F32, transposed_qv=False)

    def tok3(a):
        return a.reshape(bd, 1, a.shape[1])

    ig_s = gcol_s[:, 0:H_M].reshape(bd, H_M, 1, 1)
    fg_s = gcol_s[:, H_M:N_GATE].reshape(bd, H_M, 1, 1)
    hs_m, c_s, n_s, m_s = _mlstm_step(
        tok3(qs), tok3(ks), tok3(vs), tok3(gate_s), ig_s, fg_s,
        state_mlstm_m[layer].astype(F32).reshape(bd, H_M, 1, 1),
        state_mlstm_n[layer].astype(F32).reshape(bd, H_M, 1, dh_m),
        state_mlstm_c[layer].astype(F32), g_m, tb=8)
    hs_d = _diff_decode(page_table, lams, tok3(dq_s), dk_s.reshape(bd, H_D, dv_d),
                        dv_s.reshape(bd, H_D, dv_d), tok3(gate_s), g_d,
                        cache_diff_k[layer].reshape(n_pool, page * H_D, dv_d),
                        cache_diff_v[layer].reshape(n_pool, page * H_D, dv_d), lam_init, gate_d0,
                        tb=2)
    hs_c = _mem_decode(tok3(cq_s), cache_mem_k[layer].reshape(bd, n_mem * H_C, d_c // H_C),
                       cache_mem_v[layer].reshape(bd, n_mem * H_C, d_c // H_C), tok3(gate_s),
                       gate_c0, tb=8)
    y_s = _merge(xs2d, hs_m.reshape(bd, d_m), hs_d.reshape(bd, d_d), hs_c.reshape(bd, d_c),
                 w_out_b, g_ln, b_ln, bd, alpha)

    return (y_p.reshape(bp, lp, d_model), y_s.reshape(bd, ls, d_model),
            c_p[None], n_p[None], m_p[:, 0, 0:H_M][None],
            dk_f.reshape(1, bp, lp, H_D, dv_d), dv_f.reshape(1, bp, lp, H_D, dv_d),
            mk_f.reshape(1, bp, n_mem, H_C, d_c // H_C), mv_f.reshape(1, bp, n_mem, H_C, d_c // H_C),
            c_s[None], n_s.reshape(1, bd, H_M, dh_m), m_s.reshape(1, bd, H_M),
            dk_s.reshape(1, bd, ls, H_D, dv_d), dv_s.reshape(1, bd, ls, H_D, dv_d))
```

```python
import functools
import math

import jax
import jax.numpy as jnp
from jax import lax
from jax.experimental import pallas as pl
from jax.experimental.pallas import tpu as pltpu

F32 = jnp.float32
BF16 = jnp.bfloat16

H_M = 4
H_D = 4
H_C = 4
N_GATE = 2 * H_M
CHUNK = 128
ROPE_THETA = 10000.0
EPS = 1e-5
LANES = 128
SUBLANES = 8
VMEM_LIMIT = 56 * 1024 * 1024

NT_DIMS = (((1,), (1,)), ((), ()))
TN_DIMS = (((0,), (0,)), ((), ()))


def _params(semantics):
    return pltpu.CompilerParams(dimension_semantics=semantics, vmem_limit_bytes=VMEM_LIMIT)


def _log_sigmoid(x):
    return jnp.minimum(x, 0.0) - jnp.log1p(jnp.exp(-jnp.abs(x)))


def _silu(x):
    return x * jax.nn.sigmoid(x)


def _split_rows(row, order):
    n_rows = 2 * SUBLANES
    r_idx = lax.broadcasted_iota(jnp.int32, (n_rows, row.shape[1]), 0)
    hi = row.astype(BF16).astype(F32)
    parts = {"hi": hi, "lo": row - hi}
    out = jnp.zeros((n_rows, row.shape[1]), F32)
    for i, name in enumerate(order):
        out = jnp.where(r_idx == i, parts[name], out)
    return out.astype(BF16)


def _lambda(lq1_ref, lk1_ref, lq2_ref, lk2_ref, lam_init):
    a = jnp.sum(lq1_ref[...] * lk1_ref[...], axis=1, keepdims=True)
    b = jnp.sum(lq2_ref[...] * lk2_ref[...], axis=1, keepdims=True)
    return jnp.exp(a) - jnp.exp(b) + lam_init


def _inproj_kernel(x_ref, wqt_ref, wvt_ref, wm_ref, wg_ref, wgt_ref, wd_ref, wgate_ref,
                   bcol_ref, brow_ref,
                   cos_ref, sin_ref,
                   q_ref, k_ref, v_ref, gcol_ref, grow_ref,
                   dq_ref, dkf_ref, dkb_ref, dvf_ref, dvt_ref, cq_ref, gate_ref,
                   *, d_m, d_d, dh_m, dqk, transposed_qv):
    x = x_ref[...].astype(BF16)

    def mm(w):
        return jnp.dot(x, w, preferred_element_type=F32)

    def mm_t(a, b):
        return lax.dot_general(a, b, NT_DIMS, preferred_element_type=F32)

    if transposed_qv:
        q_ref[0] = mm_t(wqt_ref[...], x).astype(q_ref.dtype)
        v_ref[0] = mm_t(wvt_ref[...], x).astype(v_ref.dtype)
    else:
        q_ref[...] = mm_t(x, wqt_ref[...]).astype(q_ref.dtype)
        v_ref[...] = mm_t(x, wvt_ref[...]).astype(v_ref.dtype)
    k_ref[...] = (mm(wm_ref[:, 0:d_m]) * (dh_m ** -0.5)).astype(k_ref.dtype)
    gate_ref[:, 0:d_m] = (jax.nn.sigmoid(mm(wm_ref[:, d_m:2 * d_m]))
                          * _silu(mm(wgate_ref[:, 0:d_m]))).astype(gate_ref.dtype)
    gate_ref[:, d_m:] = _silu(mm(wgate_ref[:, d_m:])).astype(gate_ref.dtype)
    gcol_ref[...] = mm(wg_ref[...]) + bcol_ref[...]
    grow_ref[...] = (lax.dot_general(wgt_ref[...], x, NT_DIMS, preferred_element_type=F32)
                     + brow_ref[:, 0:1])

    cos = cos_ref[...]
    sin = sin_ref[...]
    lane = lax.broadcasted_iota(jnp.int32, cos.shape, 1)
    first_half = (lane % dqk) < (dqk // 2)

    def rope(t):
        swapped = jnp.where(first_half, pltpu.roll(t, LANES - dqk // 2, 1),
                            pltpu.roll(t, dqk // 2, 1))
        return t * cos + swapped * sin

    tm = x.shape[0]
    dq = mm(wd_ref[:, 0:d_d])
    dk = mm(wd_ref[:, d_d:2 * d_d])
    dv = mm(wd_ref[:, 2 * d_d:3 * d_d])
    dvt_ref[0] = dv.T.astype(dvt_ref.dtype)
    for h in range(H_D):
        sl = slice(h * LANES, (h + 1) * LANES)
        dq_ref[:, sl] = (rope(dq[:, sl]) * (dqk ** -0.5)).astype(dq_ref.dtype)
        dk_h = rope(dk[:, sl])
        dkb_ref[:, sl] = dk_h.astype(dkb_ref.dtype)
        dkf_ref[pl.ds(h, tm, stride=H_D), :] = dk_h
        dvf_ref[pl.ds(h, tm, stride=H_D), :] = dv[:, sl]
    cq_ref[...] = mm(wd_ref[:, 3 * d_d:4 * d_d]).astype(cq_ref.dtype)


def _input_projection(x2d, wts, cos_tab, sin_tab, tm, qkv_dtype, transposed_qv):
    m, d_model = x2d.shape
    wqt, wvt, wm, wg, wgt, wd, wgate, bcol, brow = wts
    d_m = wqt.shape[0]
    d_d = wd.shape[1] // 4
    d_mix = wgate.shape[1]
    dh_m = d_m // H_M
    dqk = d_d // H_D // 2
    grid = (m // tm,)

    def row_spec(n):
        return pl.BlockSpec((tm, n), lambda i: (i, 0))

    def whole(a):
        return pl.BlockSpec(a.shape, lambda i: (0,) * a.ndim, pipeline_mode=pl.Buffered(1))

    if transposed_qv:
        qv_shape = jax.ShapeDtypeStruct((m // tm, d_m, tm), qkv_dtype)
        qv_spec = pl.BlockSpec((1, d_m, tm), lambda i: (i, 0, 0))
    else:
        qv_shape = jax.ShapeDtypeStruct((m, d_m), qkv_dtype)
        qv_spec = row_spec(d_m)
    out_shape = (
        qv_shape,
        jax.ShapeDtypeStruct((m, d_m), qkv_dtype),
        qv_shape,
        jax.ShapeDtypeStruct((m, LANES), F32),
        jax.ShapeDtypeStruct((N_GATE, m), F32),
        jax.ShapeDtypeStruct((m, d_d), BF16),
        jax.ShapeDtypeStruct((m * H_D, LANES), F32),
        jax.ShapeDtypeStruct((m, d_d), BF16),
        jax.ShapeDtypeStruct((m * H_D, LANES), F32),
        jax.ShapeDtypeStruct((m // tm, d_d, tm), BF16),
        jax.ShapeDtypeStruct((m, d_d), BF16),
        jax.ShapeDtypeStruct((m, d_mix), BF16),
    )
    head_rows = pl.BlockSpec((tm * H_D, LANES), lambda i: (i, 0))
    out_specs = (
        qv_spec, row_spec(d_m), qv_spec, row_spec(LANES),
        pl.BlockSpec((N_GATE, tm), lambda i: (0, i)),
        row_spec(d_d), head_rows, row_spec(d_d), head_rows,
        pl.BlockSpec((1, d_d, tm), lambda i: (i, 0, 0)), row_spec(d_d),
        row_spec(d_mix),
    )
    n_tab = cos_tab.shape[0] // tm
    tab_spec = pl.BlockSpec((tm, LANES), lambda i: (i % n_tab, 0))
    in_specs = [row_spec(d_model), whole(wqt), whole(wvt), whole(wm), whole(wg), whole(wgt),
                whole(wd), whole(wgate), whole(bcol), whole(brow), tab_spec, tab_spec]
    return pl.pallas_call(
        functools.partial(_inproj_kernel, d_m=d_m, d_d=d_d, dh_m=dh_m, dqk=dqk,
                          transposed_qv=transposed_qv),
        grid=grid, in_specs=in_specs, out_specs=out_specs, out_shape=out_shape,
        compiler_params=_params(("parallel",)), name="input_projection",
    )(x2d, wqt, wvt, wm, wg, wgt, wd, wgate, bcol, brow, cos_tab, sin_tab)


def _mlstm_prompt_kernel(qt_ref, k_ref, vt_ref, gate_ref, grow_ref, gcol_ref, gb_ref,
                         h_ref, c_ref, n_ref, m_ref, ct_sc, *, dh, c):
    j = pl.program_id(1)

    @pl.when(j == 0)
    def _():
        ct_sc[...] = jnp.zeros_like(ct_sc)
        n_ref[...] = jnp.zeros_like(n_ref)
        m_ref[...] = jnp.zeros_like(m_ref)

    s_idx = lax.broadcasted_iota(jnp.int32, (c, c), 0)
    t_idx = lax.broadcasted_iota(jnp.int32, (c, c), 1)
    causal = s_idx <= t_idx
    lower = t_idx <= s_idx

    for cc in range(k_ref.shape[0] // c):
        _mlstm_chunk(slice(cc * c, (cc + 1) * c), causal, lower, qt_ref, k_ref, vt_ref, gate_ref,
                     grow_ref, gcol_ref, gb_ref, h_ref, n_ref, m_ref, ct_sc, dh=dh, c=c)

    @pl.when(j == pl.num_programs(1) - 1)
    def _():
        for h in range(H_M):
            c_ref[h] = ct_sc[h].T


def _mlstm_chunk(ts, causal, lower, qt_ref, k_ref, vt_ref, gate_ref, grow_ref, gcol_ref, gb_ref,
                 h_ref, n_ref, m_ref, ct_sc, *, dh, c):
    heads = []
    for h in range(H_M):
        sl = slice(h * dh, (h + 1) * dh)
        qt = qt_ref[sl, ts]
        k = k_ref[ts, sl]
        ig_row = grow_ref[h:h + 1, ts]
        lf_row = _log_sigmoid(grow_ref[H_M + h:H_M + h + 1, ts])
        ig_col = gcol_ref[ts, h:h + 1]
        lf_col = _log_sigmoid(gcol_ref[ts, H_M + h:H_M + h + 1])
        bcum_col = jnp.sum(jnp.where(lower, lf_row, 0.0), axis=1, keepdims=True)
        bcum_row = jnp.sum(jnp.where(causal, lf_col, 0.0), axis=0, keepdims=True)
        m_prev = m_ref[:, h:h + 1]
        dlog = jnp.where(causal, bcum_row - bcum_col + ig_col, -jnp.inf)
        inter = bcum_row + m_prev
        mt = jnp.maximum(jnp.max(dlog, axis=0, keepdims=True), inter)
        w_inter = jnp.exp(inter - mt)
        st = jnp.dot(k, qt, preferred_element_type=F32) * jnp.exp(dlog - mt)
        ct_old = ct_sc[h]
        n_old = n_ref[h:h + 1, :]
        cq = jnp.dot(ct_old.astype(BF16), qt, preferred_element_type=F32)
        nq2 = jnp.dot(_split_rows(n_old, ("hi", "lo")), qt, preferred_element_type=F32)
        m_new = mt[:, c - 1:c]
        b_last = bcum_row[:, c - 1:c]
        w_end = jnp.exp(b_last - bcum_row + ig_row - m_new)
        decay = jnp.exp(b_last + m_prev - m_new)
        heads.append(dict(sl=sl, k=k, mt=mt, w_inter=w_inter, st=st, ct_old=ct_old, n_old=n_old,
                          cq=cq, nq=nq2[0:1, :] + nq2[1:2, :], m_new=m_new, w_end=w_end,
                          decay=decay))

    for h, d in enumerate(heads):
        vt = vt_ref[d["sl"], ts]
        num = jnp.dot(vt, d["st"].astype(BF16), preferred_element_type=F32) + d["w_inter"] * d["cq"]
        den = jnp.sum(d["st"], axis=0, keepdims=True) + d["w_inter"] * d["nq"]
        d["hc"] = num * (1.0 / jnp.maximum(jnp.abs(den), jnp.exp(-d["mt"])))
        vtw = (vt.astype(F32) * d["w_end"]).astype(BF16)
        ct_sc[h] = d["decay"] * d["ct_old"] + jnp.dot(vtw, d["k"], preferred_element_type=F32)
        kw2 = jnp.dot(_split_rows(d["w_end"], ("hi", "lo")), d["k"], preferred_element_type=F32)
        n_ref[h:h + 1, :] = d["decay"] * d["n_old"] + kw2[0:1, :] + kw2[1:2, :]
        m_ref[:, h:h + 1] = d["m_new"]

    for h, d in enumerate(heads):
        sl = d["sl"]
        hc = d["hc"]
        mu = jnp.mean(hc, axis=0, keepdims=True)
        var = jnp.mean(jnp.square(hc - mu), axis=0, keepdims=True)
        hn = ((hc - mu) * lax.rsqrt(var + EPS) * gb_ref[sl, :]).T
        h_ref[ts, sl] = (hn * gate_ref[ts, sl]).astype(h_ref.dtype)


def _mlstm_prompt(q_t, k, v_t, gate, grow, gcol, g_norm, b, l):
    d_m = k.shape[1]
    dh = d_m // H_M
    nc = l // CHUNK
    tm = q_t.shape[2]
    per = tm // CHUNK
    ns = nc // per
    tok = pl.BlockSpec((tm, d_m), lambda i, j: (i * ns + j, 0))
    tok_t = pl.BlockSpec((None, d_m, tm), lambda i, j: (i * ns + j, 0, 0))
    g_bcast = jnp.broadcast_to(g_norm.reshape(d_m, 1), (d_m, CHUNK))
    in_specs = [tok_t, tok, tok_t, tok,
                pl.BlockSpec((N_GATE, tm), lambda i, j: (0, i * ns + j)),
                pl.BlockSpec((tm, LANES), lambda i, j: (i * ns + j, 0)),
                pl.BlockSpec((d_m, CHUNK), lambda i, j: (0, 0))]
    out_shape = (jax.ShapeDtypeStruct((b * l, d_m), BF16),
                 jax.ShapeDtypeStruct((b, H_M, dh, dh), F32),
                 jax.ShapeDtypeStruct((b, H_M, dh), F32),
                 jax.ShapeDtypeStruct((b, 1, LANES), F32))
    out_specs = (tok,
                 pl.BlockSpec((None, H_M, dh, dh), lambda i, j: (i, 0, 0, 0)),
                 pl.BlockSpec((None, H_M, dh), lambda i, j: (i, 0, 0)),
                 pl.BlockSpec((None, 1, LANES), lambda i, j: (i, 0, 0)))
    return pl.pallas_call(
        functools.partial(_mlstm_prompt_kernel, dh=dh, c=CHUNK),
        grid=(b, ns), in_specs=in_specs, out_specs=out_specs, out_shape=out_shape,
        scratch_shapes=[pltpu.VMEM((H_M, dh, dh), F32)],
        compiler_params=_params(("parallel", "arbitrary")), name="mlstm_prompt",
    )(q_t, k, v_t, gate, grow, gcol, g_bcast)


def _diff_prompt_kernel(lq1_ref, lk1_ref, lq2_ref, lk2_ref, q_ref, k_ref, vt_ref, gate_ref, g_ref,
                        o_ref, m_sc, l_sc, acc_sc, *, lam_init, dqk):
    tq = q_ref.shape[0]
    dv = 2 * dqk
    qi = pl.program_id(1)
    lam = _lambda(lq1_ref, lk1_ref, lq2_ref, lk2_ref, lam_init)
    lane = lax.broadcasted_iota(jnp.int32, (tq, dv), 1)
    key_i = lax.broadcasted_iota(jnp.int32, (tq, 2 * tq), 0)
    qry_i = lax.broadcasted_iota(jnp.int32, (tq, 2 * tq), 1)
    visible = key_i <= jnp.where(qry_i >= tq, qry_i - tq, qry_i)
    ones = jnp.ones((2 * SUBLANES, tq), BF16)
    q_both = []
    for h in range(H_D):
        q = q_ref[:, h * dv:(h + 1) * dv]
        zero = jnp.zeros_like(q)
        q_both.append(jnp.concatenate([jnp.where(lane < dqk, q, zero),
                                       jnp.where(lane >= dqk, q, zero)], axis=0))
    m_sc[...] = jnp.full(m_sc.shape, -jnp.inf, F32)
    l_sc[...] = jnp.zeros_like(l_sc)
    acc_sc[...] = jnp.zeros_like(acc_sc)

    def key_span(first_tile, n_tiles, on_diagonal):
        start = pl.multiple_of(first_tile * tq, tq)
        scores = []
        for h in range(H_D):
            hs = slice(h * dv, (h + 1) * dv)
            k = k_ref[pl.ds(start, n_tiles * tq), hs]
            scores.append(lax.dot_general(k, q_both[h], NT_DIMS,
                                          preferred_element_type=F32))
        for h in range(H_D):
            hs = slice(h * dv, (h + 1) * dv)
            st = scores[h]
            if on_diagonal:
                below = (n_tiles - 1) * tq
                diag = jnp.where(visible, st[below:, :], -jnp.inf)
                st = diag if n_tiles == 1 else jnp.concatenate([st[0:below, :], diag], axis=0)
            m_old = m_sc[h]
            m_new = jnp.maximum(m_old, jnp.max(st, axis=0, keepdims=True))
            alpha = jnp.exp(m_old - m_new)
            pt = jnp.exp(st - m_new).astype(BF16)
            pv = jnp.dot(jnp.concatenate([vt_ref[first_tile, hs, :], ones], axis=0), pt[0:tq],
                         preferred_element_type=F32)
            for t in range(1, n_tiles):
                pv += jnp.dot(jnp.concatenate([vt_ref[first_tile + t, hs, :], ones], axis=0),
                              pt[t * tq:(t + 1) * tq], preferred_element_type=F32)
            l_sc[h] = alpha * l_sc[h] + pv[dv:dv + 1, :]
            acc_sc[h] = alpha * acc_sc[h] + pv[0:dv, :]
            m_sc[h] = m_new

    def tile_pair(i, carry):
        key_span(2 * i, 2, False)
        return carry

    lax.fori_loop(0, qi // 2, tile_pair, 0)

    @pl.when(qi % 2 == 1)
    def _():
        key_span(qi - 1, 2, True)

    @pl.when(qi % 2 == 0)
    def _():
        key_span(qi, 1, True)

    for h in range(H_D):
        hs = slice(h * dv, (h + 1) * dv)
        on = acc_sc[h] / l_sc[h]
        o = (on[:, 0:tq] - lam * on[:, tq:2 * tq]).T
        o = o * lax.rsqrt(jnp.mean(o * o, axis=1, keepdims=True) + EPS) * g_ref[...]
        o_ref[:, hs] = (o * (1.0 - lam_init) * gate_ref[:, hs]).astype(o_ref.dtype)


def _diff_prompt(lams, dq, dk, dv_t, gate, g_norm, b, l, lam_init, gate_col0):
    d_d = dq.shape[1]
    dv_d = d_d // H_D
    tq = dv_t.shape[2]
    nq = l // tq
    lam_spec = pl.BlockSpec(lams[0].shape, lambda i, j: (0, 0))
    tok = pl.BlockSpec((tq, d_d), lambda i, j: (i * nq + j, 0))
    in_specs = [lam_spec] * 4 + [
        tok,
        pl.BlockSpec((l, d_d), lambda i, j: (i, 0)),
        pl.BlockSpec((nq, d_d, tq), lambda i, j: (i, 0, 0)),
        pl.BlockSpec((tq, d_d), lambda i, j: (i * nq + j, gate_col0 // d_d)),
        pl.BlockSpec((1, dv_d), lambda i, j: (0, 0))]
    return pl.pallas_call(
        functools.partial(_diff_prompt_kernel, lam_init=lam_init, dqk=dv_d // 2),
        grid=(b, nq), in_specs=in_specs, out_specs=tok,
        out_shape=jax.ShapeDtypeStruct((b * l, d_d), BF16),
        scratch_shapes=[pltpu.VMEM((H_D, 1, 2 * tq), F32), pltpu.VMEM((H_D, 1, 2 * tq), F32),
                        pltpu.VMEM((H_D, dv_d, 2 * tq), F32)],
        compiler_params=_params(("parallel", "parallel")), name="diff_attn_prompt",
    )(*lams, dq, dk, dv_t, gate, g_norm)


def _mem_kv_kernel(x_ref, w_ref, kf_ref, kb_ref, vf_ref, vb_ref):
    tm, d_c = kb_ref.shape
    dh = d_c // H_C
    x = x_ref[...]
    mk = jnp.dot(x, w_ref[:, 0:d_c], preferred_element_type=F32)
    mv = jnp.dot(x, w_ref[:, d_c:2 * d_c], preferred_element_type=F32)
    kb_ref[...] = mk.astype(kb_ref.dtype)
    vb_ref[...] = mv.astype(vb_ref.dtype)
    for h in range(H_C):
        sl = slice(h * dh, (h + 1) * dh)
        kf_ref[pl.ds(h, tm, stride=H_C), :] = mk[:, sl]
        vf_ref[pl.ds(h, tm, stride=H_C), :] = mv[:, sl]


def _mem_kv(mem2d, w, tm):
    m, d_model = mem2d.shape
    d_c = w.shape[1] // 2
    dh = d_c // H_C
    row = pl.BlockSpec((tm, d_c), lambda i: (i, 0))
    head_rows = pl.BlockSpec((tm * H_C, dh), lambda i: (i, 0))
    return pl.pallas_call(
        _mem_kv_kernel, grid=(m // tm,),
        in_specs=[pl.BlockSpec((tm, d_model), lambda i: (i, 0)),
                  pl.BlockSpec(w.shape, lambda i: (0, 0))],
        out_specs=(head_rows, row, head_rows, row),
        out_shape=(jax.ShapeDtypeStruct((m * H_C, dh), F32), jax.ShapeDtypeStruct((m, d_c), BF16),
                   jax.ShapeDtypeStruct((m * H_C, dh), F32), jax.ShapeDtypeStruct((m, d_c), BF16)),
        compiler_params=_params(("parallel",)), name="memory_kv",
    )(mem2d, w)


def _mem_prompt_kernel(q_ref, k_ref, v_ref, gate_ref, o_ref, *, dh):
    slices = [slice(h * dh, (h + 1) * dh) for h in range(H_C)]
    scores = [lax.dot_general(q_ref[:, sl], k_ref[:, sl], NT_DIMS, preferred_element_type=F32)
              for sl in slices]
    probs = []
    for s in scores:
        s = s * (dh ** -0.5)
        p = jnp.exp(s - jnp.max(s, axis=1, keepdims=True))
        probs.append((p / jnp.sum(p, axis=1, keepdims=True)).astype(BF16))
    for p, sl in zip(probs, slices):
        o = jnp.dot(p, v_ref[:, sl], preferred_element_type=F32)
        o_ref[:, sl] = (o * gate_ref[:, sl]).astype(o_ref.dtype)


def _mem_prompt(cq, mk, mv, gate, b, l, n_mem, tl, gate_col0):
    d_c = cq.shape[1]
    nl = l // tl
    tok = pl.BlockSpec((tl, d_c), lambda i, j: (i * nl + j, 0))
    mem = pl.BlockSpec((n_mem, d_c), lambda i, j: (i, 0))
    return pl.pallas_call(
        functools.partial(_mem_prompt_kernel, dh=d_c // H_C),
        grid=(b, nl),
        in_specs=[tok, mem, mem,
                  pl.BlockSpec((tl, d_c), lambda i, j: (i * nl + j, gate_col0 // d_c))],
        out_specs=tok, out_shape=jax.ShapeDtypeStruct((b * l, d_c), BF16),
        compiler_params=_params(("parallel", "parallel")), name="memory_attn_prompt",
    )(cq, mk, mv, gate)


def _merge_kernel(x_ref, hm_ref, hd_ref, hc_ref, w_ref, g_ref, b_ref, y_ref, *, alpha):
    tm = x_ref.shape[0]
    n_strips = max(1, tm // 256)
    rows = tm // n_strips
    for r in range(n_strips):
        rs = slice(r * rows, (r + 1) * rows)
        h = jnp.concatenate([hm_ref[rs, :], hd_ref[rs, :], hc_ref[rs, :]], axis=1)
        y = alpha * x_ref[rs, :] + jnp.dot(h, w_ref[...], preferred_element_type=F32)
        mu = jnp.mean(y, axis=1, keepdims=True)
        var = jnp.mean(jnp.square(y - mu), axis=1, keepdims=True)
        y_ref[rs, :] = (y - mu) * lax.rsqrt(var + EPS) * g_ref[...] + b_ref[...]


def _merge(x2d, h_m, h_d, h_c, w_out, ln_g, ln_b, tm, alpha):
    m, d_model = x2d.shape

    def row(n):
        return pl.BlockSpec((tm, n), lambda i: (i, 0))

    def whole(a):
        return pl.BlockSpec(a.shape, lambda i: (0, 0))

    return pl.pallas_call(
        functools.partial(_merge_kernel, alpha=alpha), grid=(m // tm,),
        in_specs=[row(d_model), row(h_m.shape[1]), row(h_d.shape[1]), row(h_c.shape[1]),
                  whole(w_out), whole(ln_g), whole(ln_b)],
        out_specs=row(d_model), out_shape=jax.ShapeDtypeStruct((m, d_model), F32),
        compiler_params=_params(("parallel",)), name="output_projection",
    )(x2d, h_m, h_d, h_c, w_out, ln_g, ln_b)


def _mlstm_step_kernel(q_ref, k_ref, v_ref, gate_ref, ig_ref, fg_ref, m_ref, n_ref, c_ref,
                       g_ref, h_ref, cn_ref, nn_ref, mn_ref, *, dh):
    def one_sequence(b, carry):
        for h in range(H_M):
            sl = slice(h * dh, (h + 1) * dh)
            q = q_ref[b, :, sl]
            k = k_ref[b, :, sl]
            v = v_ref[b, :, sl]
            ig = ig_ref[b, h]
            lf = _log_sigmoid(fg_ref[b, h])
            m_prev = m_ref[b, h]
            n_old = n_ref[b, h]
            c_old = c_ref[b, h]

            inter = lf + m_prev
            mt = jnp.maximum(ig, inter)
            w_in = jnp.exp(ig - mt)
            w_st = jnp.exp(inter - mt)
            s = jnp.sum(q * k, axis=1, keepdims=True) * w_in
            qc2 = jnp.dot(_split_rows(q, ("hi", "lo")), c_old.astype(BF16),
                          preferred_element_type=F32)
            qc = qc2[0:1, :] + qc2[1:2, :]
            outer = lax.dot_general(_split_rows(k, ("hi", "hi", "lo")),
                                    _split_rows(v, ("hi", "lo", "hi")), TN_DIMS,
                                    preferred_element_type=F32)
            num = s * v + w_st * qc
            den = s + w_st * jnp.sum(q * n_old, axis=1, keepdims=True)
            hc = num / jnp.maximum(jnp.abs(den), jnp.exp(-mt))

            cn_ref[b, h] = w_st * c_old + w_in * outer
            nn_ref[b, h] = w_st * n_old + w_in * k
            mn_ref[b, h] = mt

            mu = jnp.mean(hc, axis=1, keepdims=True)
            var = jnp.mean(jnp.square(hc - mu), axis=1, keepdims=True)
            hn = (hc - mu) * lax.rsqrt(var + EPS) * g_ref[:, sl]
            h_ref[b, :, sl] = (hn * gate_ref[b, :, sl]).astype(h_ref.dtype)
        return carry

    lax.fori_loop(0, q_ref.shape[0], one_sequence, 0)


def _mlstm_step(q, k, v, gate, ig, fg, m0, n0, c0, g_norm, tb):
    bd, _, d_m = q.shape
    dh = d_m // H_M

    def blk(shape):
        nd = len(shape)
        return pl.BlockSpec((tb,) + tuple(shape), lambda i: (i,) + (0,) * nd)

    tok = blk((1, d_m))
    one = blk((H_M, 1, 1))
    in_specs = [tok, tok, tok, tok, one, one, one, blk((H_M, 1, dh)), blk((H_M, dh, dh)),
                pl.BlockSpec((1, d_m), lambda i: (0, 0))]
    out_shape = (jax.ShapeDtypeStruct((bd, 1, d_m), BF16),
                 jax.ShapeDtypeStruct((bd, H_M, dh, dh), F32),
                 jax.ShapeDtypeStruct((bd, H_M, 1, dh), F32),
                 jax.ShapeDtypeStruct((bd, H_M, 1, 1), F32))
    out_specs = (tok, blk((H_M, dh, dh)), blk((H_M, 1, dh)), one)
    return pl.pallas_call(
        functools.partial(_mlstm_step_kernel, dh=dh), grid=(bd // tb,),
        in_specs=in_specs, out_specs=out_specs, out_shape=out_shape,
        compiler_params=_params(("parallel",)), name="mlstm_step",
    )(q, k, v, gate, ig, fg, m0, n0, c0, g_norm)


def _head_rows(row, n_heads, reps):
    dh = row.shape[1] // n_heads
    heads = [row[:, h * dh:(h + 1) * dh] for h in range(n_heads)]
    return jnp.concatenate(heads * reps, axis=0)


def _own_head(n_rows, n_cols, n_heads):
    r = lax.broadcasted_iota(jnp.int32, (n_rows, n_cols), 0)
    c = lax.broadcasted_iota(jnp.int32, (n_rows, n_cols), 1)
    return (c % n_heads) == (r % n_heads)


def _diff_decode_kernel(pt_ref, lq1_ref, lk1_ref, lq2_ref, lk2_ref, q_ref, kn_ref, vn_ref,
                        gate_ref, g_ref, *rest, n_pages, lam_init):
    del pt_ref
    tb = q_ref.shape[0]
    k_refs = rest[:tb * n_pages]
    v_refs = rest[tb * n_pages:2 * tb * n_pages]
    o_ref = rest[2 * tb * n_pages]
    n_rows, dv_d = k_refs[0].shape
    dqk = dv_d // 2
    lam = _lambda(lq1_ref, lk1_ref, lq2_ref, lk2_ref, lam_init)
    r_idx = lax.broadcasted_iota(jnp.int32, (2 * H_D, dv_d), 0)
    l_idx = lax.broadcasted_iota(jnp.int32, (2 * H_D, dv_d), 1)
    own_map = (l_idx // dqk) == (r_idx // H_D)
    own = _own_head(2 * H_D, n_rows, H_D)

    q8s, scores = [], []
    for b in range(tb):
        q8 = jnp.where(own_map, _head_rows(q_ref[b].astype(F32), H_D, 2), 0.0)
        q8b = q8.astype(BF16)
        q8s.append(q8)
        scores.append([jnp.where(own, lax.dot_general(q8b, k_refs[b * n_pages + j][...].astype(BF16),
                                                      NT_DIMS, preferred_element_type=F32), -jnp.inf)
                       for j in range(n_pages)])

    weights = []
    for b in range(tb):
        kn4 = kn_ref[b]
        s_new = jnp.sum(q8s[b] * jnp.concatenate([kn4, kn4], axis=0), axis=1, keepdims=True)
        mx = s_new
        for s in scores[b]:
            mx = jnp.maximum(mx, jnp.max(s, axis=1, keepdims=True))
        probs = [jnp.exp(s - mx) for s in scores[b]]
        p_new = jnp.exp(s_new - mx)
        total = p_new
        for p in probs:
            total = total + jnp.sum(p, axis=1, keepdims=True)
        inv = 1.0 / total

        def combine(p, inv=inv):
            pn = p * inv
            return pn - lam * pltpu.roll(pn, SUBLANES - H_D, 0)

        a_new = combine(jnp.broadcast_to(p_new, (SUBLANES, LANES)))[:, 0:1]
        weights.append((a_new, [combine(p).astype(BF16) for p in probs]))

    for b in range(tb):
        a_new, a_pages = weights[b]
        vn4 = vn_ref[b]
        acc = a_new * jnp.concatenate([vn4, vn4], axis=0)
        for j in range(n_pages):
            acc = acc + jnp.dot(a_pages[j], v_refs[b * n_pages + j][...].astype(BF16),
                                preferred_element_type=F32)
        for h in range(H_D):
            sl = slice(h * dv_d, (h + 1) * dv_d)
            o = acc[h:h + 1, :]
            o = o * lax.rsqrt(jnp.mean(o * o, axis=1, keepdims=True) + EPS) * g_ref[...]
            o_ref[b, :, sl] = (o * (1.0 - lam_init) * gate_ref[b, :, sl]).astype(o_ref.dtype)


def _diff_decode(page_table, lams, dq, dk_new, dv_new, gate, g_norm, cache_k, cache_v,
                 lam_init, gate_col0, tb):
    bd, _, d_d = dq.shape
    n_pages = page_table.shape[1]
    _, n_rows, dv_d = cache_k.shape
    lam_spec = pl.BlockSpec(lams[0].shape, lambda i, pt: (0, 0))
    tok = pl.BlockSpec((tb, 1, d_d), lambda i, pt: (i, 0, 0))
    page_specs = [pl.BlockSpec((None, n_rows, dv_d),
                               lambda i, pt, b=b, j=j: (pt[i * tb + b, j], 0, 0))
                  for b in range(tb) for j in range(n_pages)]
    new_tok = pl.BlockSpec((tb, H_D, dv_d), lambda i, pt: (i, 0, 0))
    in_specs = ([lam_spec] * 4 + [tok, new_tok, new_tok,
                                  pl.BlockSpec((tb, 1, d_d), lambda i, pt: (i, 0, gate_col0 // d_d)),
                                  pl.BlockSpec((1, dv_d), lambda i, pt: (0, 0))]
                + page_specs + page_specs)
    grid_spec = pltpu.PrefetchScalarGridSpec(
        num_scalar_prefetch=1, grid=(bd // tb,), in_specs=in_specs, out_specs=tok)
    n_ops = tb * n_pages
    return pl.pallas_call(
        functools.partial(_diff_decode_kernel, n_pages=n_pages, lam_init=lam_init),
        grid_spec=grid_spec, out_shape=jax.ShapeDtypeStruct((bd, 1, d_d), BF16),
        compiler_params=_params(("parallel",)), name="diff_attn_decode",
    )(page_table, *lams, dq, dk_new, dv_new, gate, g_norm,
      *([cache_k] * n_ops), *([cache_v] * n_ops))


def _mem_decode_kernel(q_ref, k_ref, v_ref, gate_ref, o_ref):
    tb, n_rows, dh = k_ref.shape
    own = _own_head(SUBLANES, n_rows, H_C)
    scores = []
    for b in range(tb):
        q8 = _head_rows(q_ref[b].astype(F32), H_C, SUBLANES // H_C).astype(BF16)
        scores.append(lax.dot_general(q8, k_ref[b].astype(BF16), NT_DIMS,
                                      preferred_element_type=F32))
    probs = []
    for s in scores:
        s = jnp.where(own, s * (dh ** -0.5), -jnp.inf)
        p = jnp.exp(s - jnp.max(s, axis=1, keepdims=True))
        probs.append((p / jnp.sum(p, axis=1, keepdims=True)).astype(BF16))
    for b in range(tb):
        acc = jnp.dot(probs[b], v_ref[b].astype(BF16), preferred_element_type=F32)
        for h in range(H_C):
            sl = slice(h * dh, (h + 1) * dh)
            o_ref[b, :, sl] = (acc[h:h + 1, :] * gate_ref[b, :, sl]).astype(o_ref.dtype)


def _mem_decode(cq, mem_k, mem_v, gate, gate_col0, tb):
    bd, _, d_c = cq.shape
    _, n_rows, dh = mem_k.shape
    tok = pl.BlockSpec((tb, 1, d_c), lambda i: (i, 0, 0))
    mem = pl.BlockSpec((tb, n_rows, dh), lambda i: (i, 0, 0))
    return pl.pallas_call(
        _mem_decode_kernel, grid=(bd // tb,),
        in_specs=[tok, mem, mem, pl.BlockSpec((tb, 1, d_c), lambda i: (i, 0, gate_col0 // d_c))],
        out_specs=tok, out_shape=jax.ShapeDtypeStruct((bd, 1, d_c), BF16),
        compiler_params=_params(("parallel",)), name="memory_attn_decode",
    )(cq, mem_k, mem_v, gate)


def _rope_tables(pos, n_rows, dqk):
    half = dqk // 2
    lane = jnp.arange(LANES)
    inv_freq = ROPE_THETA ** (-(lane % half).astype(F32) / half)
    ang = pos.astype(F32)[:, None] * inv_freq[None, :]
    sin_t = jnp.where((lane % dqk) < half, -jnp.sin(ang), jnp.sin(ang))
    return (jnp.broadcast_to(jnp.cos(ang), (n_rows, LANES)), jnp.broadcast_to(sin_t, (n_rows, LANES)))


def _split_weights(w_in_l, b_if_l, d_m, d_d, d_mix):
    g0 = 4 * d_m
    d0 = g0 + N_GATE
    wqt = w_in_l[:, 0:d_m].T.astype(BF16)
    wvt = w_in_l[:, 2 * d_m:3 * d_m].T.astype(BF16)
    wm = jnp.concatenate([w_in_l[:, d_m:2 * d_m], w_in_l[:, 3 * d_m:g0]], axis=1).astype(BF16)
    w_gates = w_in_l[:, g0:d0]
    wg = jnp.pad(w_gates, ((0, 0), (0, LANES - N_GATE))).astype(BF16)
    wgt = w_gates.T.astype(BF16)
    wd = w_in_l[:, d0:d0 + 4 * d_d].astype(BF16)
    wgate = w_in_l[:, d0 + 4 * d_d:].astype(BF16)
    bcol = jnp.pad(b_if_l, (0, LANES - N_GATE)).reshape(1, LANES).astype(F32)
    brow = jnp.broadcast_to(b_if_l.astype(F32)[:, None], (N_GATE, LANES))
    return wqt, wvt, wm, wg, wgt, wd, wgate, bcol, brow


def kernel(x_prompt, x_sample, state_mlstm_c, state_mlstm_n, state_mlstm_m, cache_diff_k, cache_diff_v,
           cache_mem_k, cache_mem_v, page_table, mem_prompt, w_in, b_if, w_mlstm_norm,
           lambda_q1, lambda_k1, lambda_q2, lambda_k2, w_diff_norm, w_mem_kv, w_out, ln_g, ln_b):
    depth = w_in.shape[0]
    assert depth == 1, "single-layer trunk"
    bp, lp, d_model = x_prompt.shape
    bd, ls, _ = x_sample.shape
    assert ls == 1
    d_mix = w_out.shape[1]
    d_m = d_mix // 2
    d_d = d_mix // 4
    d_c = d_mix // 4
    dh_m = d_m // H_M
    dv_d = d_d // H_D
    dqk = dv_d // 2
    assert dv_d == LANES and d_c // H_C == LANES, "one attention head per lane tile"
    n_mem = mem_prompt.shape[1]
    n_pool, page = cache_diff_k.shape[1], cache_diff_k.shape[2]
    past_len = page_table.shape[1] * page
    alpha = (2 * depth) ** 0.25
    layer = 0
    lam_init = 0.8 - 0.6 * math.exp(-0.3 * layer)

    wts = _split_weights(w_in[layer], b_if[layer], d_m, d_d, d_mix)
    lams = tuple(a[layer].reshape(1, dqk).astype(F32)
                 for a in (lambda_q1, lambda_k1, lambda_q2, lambda_k2))
    g_m = w_mlstm_norm[layer].reshape(1, d_m).astype(F32)
    g_d = w_diff_norm[layer].reshape(1, dv_d).astype(F32)
    w_out_b = w_out[layer].astype(BF16)
    g_ln = ln_g[layer].reshape(1, d_model).astype(F32)
    b_ln = ln_b[layer].reshape(1, d_model).astype(F32)
    gate_d0 = d_m
    gate_c0 = d_m + d_d

    mp = bp * lp
    xp2d = x_prompt.reshape(mp, d_model)
    cos_p, sin_p = _rope_tables(jnp.arange(lp), lp, dqk)
    (q_m, k_m, v_m, gcol, grow, dq, dk_f, dk_b, dv_f, dv_t, cq, gate) = _input_projection(
        xp2d, wts, cos_p, sin_p, tm=256, qkv_dtype=BF16, transposed_qv=True)
    h_m, c_p, n_p, m_p = _mlstm_prompt(q_m, k_m, v_m, gate, grow, gcol, g_m, bp, lp)
    h_d = _diff_prompt(lams, dq, dk_b, dv_t, gate, g_d, bp, lp, lam_init, gate_d0)
    mk_f, mk_b, mv_f, mv_b = _mem_kv(mem_prompt.reshape(bp * n_mem, d_model).astype(BF16),
                                     w_mem_kv[layer].astype(BF16), tm=512)
    h_c = _mem_prompt(cq, mk_b, mv_b, gate, bp, lp, n_mem, 512, gate_c0)
    y_p = _merge(xp2d, h_m, h_d, h_c, w_out_b, g_ln, b_ln, 1024, alpha)

    xs2d = x_sample.reshape(bd, d_model)
    cos_s, sin_s = _rope_tables(jnp.full((1,), past_len), bd, dqk)
    (qs, ks, vs, gcol_s, _, dq_s, dk_s, _, dv_s, _, cq_s, gate_s) = _input_projection(
        xs2d, wts, cos_s, sin_s, tm=bd, qkv_dtype=F32, transposed_qv=False)

    def tok3(a):
        return a.reshape(bd, 1, a.shape[1])

    ig_s = gcol_s[:, 0:H_M].reshape(bd, H_M, 1, 1)
    fg_s = gcol_s[:, H_M:N_GATE].reshape(bd, H_M, 1, 1)
    hs_m, c_s, n_s, m_s = _mlstm_step(
        tok3(qs), tok3(ks), tok3(vs), tok3(gate_s), ig_s, fg_s,
        state_mlstm_m[layer].astype(F32).reshape(bd, H_M, 1, 1),
        state_mlstm_n[layer].astype(F32).reshape(bd, H_M, 1, dh_m),
        state_mlstm_c[layer].astype(F32), g_m, tb=8)
    hs_d = _diff_decode(page_table, lams, tok3(dq_s), dk_s.reshape(bd, H_D, dv_d),
                        dv_s.reshape(bd, H_D, dv_d), tok3(gate_s), g_d,
                        cache_diff_k[layer].reshape(n_pool, page * H_D, dv_d),
                        cache_diff_v[layer].reshape(n_pool, page * H_D, dv_d), lam_init, gate_d0,
                        tb=2)
    hs_c = _mem_decode(tok3(cq_s), cache_mem_k[layer].reshape(bd, n_mem * H_C, d_c // H_C),
                       cache_mem_v[layer].reshape(bd, n_mem * H_C, d_c // H_C), tok3(gate_s),
                       gate_c0, tb=8)
    y_s = _merge(xs2d, hs_m.reshape(bd, d_m), hs_d.reshape(bd, d_d), hs_c.reshape(bd, d_c),
                 w_out_b, g_ln, b_ln, bd, alpha)

    return (y_p.reshape(bp, lp, d_model), y_s.reshape(bd, ls, d_model),
            c_p[None], n_p[None], m_p[:, 0, 0:H_M][None],
            dk_f.reshape(1, bp, lp, H_D, dv_d), dv_f.reshape(1, bp, lp, H_D, dv_d),
            mk_f.reshape(1, bp, n_mem, H_C, d_c // H_C), mv_f.reshape(1, bp, n_mem, H_C, d_c // H_C),
            c_s[None], n_s.reshape(1, bd, H_M, dh_m), m_s.reshape(1, bd, H_M),
            dk_s.reshape(1, bd, ls, H_D, dv_d), dv_s.reshape(1, bd, ls, H_D, dv_d))
```

```python
import functools
import math
from typing import NamedTuple

import jax
import jax.numpy as jnp
from jax import lax
from jax.experimental import pallas as pl
from jax.experimental.pallas import tpu as pltpu

F32 = jnp.float32
BF16 = jnp.bfloat16

H_M = 4
H_D = 4
H_C = 4
N_GATE = 2 * H_M
CHUNK = 128
ROPE_THETA = 10000.0
EPS = 1e-5
LANES = 128
SUBLANES = 8
VMEM_LIMIT = 56 * 1024 * 1024

NT_DIMS = (((1,), (1,)), ((), ()))
TN_DIMS = (((0,), (0,)), ((), ()))


def _params(semantics):
    return pltpu.CompilerParams(dimension_semantics=semantics, vmem_limit_bytes=VMEM_LIMIT)


def _log_sigmoid(x):
    return jnp.minimum(x, 0.0) - jnp.log1p(jnp.exp(-jnp.abs(x)))


def _silu(x):
    return x * jax.nn.sigmoid(x)


def _split_rows(row, order):
    n_rows = 2 * SUBLANES
    r_idx = lax.broadcasted_iota(jnp.int32, (n_rows, row.shape[1]), 0)
    hi = row.astype(BF16).astype(F32)
    parts = {"hi": hi, "lo": row - hi}
    out = jnp.zeros((n_rows, row.shape[1]), F32)
    for i, name in enumerate(order):
        out = jnp.where(r_idx == i, parts[name], out)
    return out.astype(BF16)


def _lambda(lq1_ref, lk1_ref, lq2_ref, lk2_ref, lam_init):
    a = jnp.sum(lq1_ref[...] * lk1_ref[...], axis=1, keepdims=True)
    b = jnp.sum(lq2_ref[...] * lk2_ref[...], axis=1, keepdims=True)
    return jnp.exp(a) - jnp.exp(b) + lam_init


def _inproj_kernel(x_ref, wqt_ref, wvt_ref, wm_ref, wg_ref, wgt_ref, wd_ref, wgate_ref,
                   bcol_ref, brow_ref,
                   cos_ref, sin_ref,
                   q_ref, k_ref, v_ref, gcol_ref, grow_ref,
                   dq_ref, dkf_ref, dkb_ref, dvf_ref, dvt_ref, cq_ref, gate_ref,
                   *, d_m, d_d, dh_m, dqk, transposed_qv):
    x = x_ref[...].astype(BF16)

    def mm(w):
        return jnp.dot(x, w, preferred_element_type=F32)

    def mm_t(a, b):
        return lax.dot_general(a, b, NT_DIMS, preferred_element_type=F32)

    if transposed_qv:
        q_ref[0] = mm_t(wqt_ref[...], x).astype(q_ref.dtype)
        v_ref[0] = mm_t(wvt_ref[...], x).astype(v_ref.dtype)
    else:
        q_ref[...] = mm_t(x, wqt_ref[...]).astype(q_ref.dtype)
        v_ref[...] = mm_t(x, wvt_ref[...]).astype(v_ref.dtype)
    k_ref[...] = (mm(wm_ref[:, 0:d_m]) * (dh_m ** -0.5)).astype(k_ref.dtype)
    gate_ref[:, 0:d_m] = (jax.nn.sigmoid(mm(wm_ref[:, d_m:2 * d_m]))
                          * _silu(mm(wgate_ref[:, 0:d_m]))).astype(gate_ref.dtype)
    gate_ref[:, d_m:] = _silu(mm(wgate_ref[:, d_m:])).astype(gate_ref.dtype)
    gcol_ref[...] = mm(wg_ref[...]) + bcol_ref[...]
    grow_ref[...] = (lax.dot_general(wgt_ref[...], x, NT_DIMS, preferred_element_type=F32)
                     + brow_ref[:, 0:1])

    cos = cos_ref[...]
    sin = sin_ref[...]
    lane = lax.broadcasted_iota(jnp.int32, cos.shape, 1)
    first_half = (lane % dqk) < (dqk // 2)

    def rope(t):
        swapped = jnp.where(first_half, pltpu.roll(t, LANES - dqk // 2, 1),
                            pltpu.roll(t, dqk // 2, 1))
        return t * cos + swapped * sin

    tm = x.shape[0]
    dq = mm(wd_ref[:, 0:d_d])
    dk = mm(wd_ref[:, d_d:2 * d_d])
    dv = mm(wd_ref[:, 2 * d_d:3 * d_d])
    dvt_ref[0] = dv.T.astype(dvt_ref.dtype)
    for h in range(H_D):
        sl = slice(h * LANES, (h + 1) * LANES)
        dq_ref[:, sl] = (rope(dq[:, sl]) * (dqk ** -0.5)).astype(dq_ref.dtype)
        dk_h = rope(dk[:, sl])
        dkb_ref[:, sl] = dk_h.astype(dkb_ref.dtype)
        dkf_ref[pl.ds(h, tm, stride=H_D), :] = dk_h
        dvf_ref[pl.ds(h, tm, stride=H_D), :] = dv[:, sl]
    cq_ref[...] = mm(wd_ref[:, 3 * d_d:4 * d_d]).astype(cq_ref.dtype)


def _input_projection(x2d, wts, cos_tab, sin_tab, tm, qkv_dtype, transposed_qv):
    m, d_model = x2d.shape
    wqt, wvt, wm, wg, wgt, wd, wgate, bcol, brow = wts
    d_m = wqt.shape[0]
    d_d = wd.shape[1] // 4
    d_mix = wgate.shape[1]
    dh_m = d_m // H_M
    dqk = d_d // H_D // 2
    grid = (m // tm,)

    def row_spec(n):
        return pl.BlockSpec((tm, n), lambda i: (i, 0))

    def whole(a):
        return pl.BlockSpec(a.shape, lambda i: (0,) * a.ndim, pipeline_mode=pl.Buffered(1))

    if transposed_qv:
        qv_shape = jax.ShapeDtypeStruct((m // tm, d_m, tm), qkv_dtype)
        qv_spec = pl.BlockSpec((1, d_m, tm), lambda i: (i, 0, 0))
    else:
        qv_shape = jax.ShapeDtypeStruct((m, d_m), qkv_dtype)
        qv_spec = row_spec(d_m)
    out_shape = (
        qv_shape,
        jax.ShapeDtypeStruct((m, d_m), qkv_dtype),
        qv_shape,
        jax.ShapeDtypeStruct((m, LANES), F32),
        jax.ShapeDtypeStruct((N_GATE, m), F32),
        jax.ShapeDtypeStruct((m, d_d), BF16),
        jax.ShapeDtypeStruct((m * H_D, LANES), F32),
        jax.ShapeDtypeStruct((m, d_d), BF16),
        jax.ShapeDtypeStruct((m * H_D, LANES), F32),
        jax.ShapeDtypeStruct((m // tm, d_d, tm), BF16),
        jax.ShapeDtypeStruct((m, d_d), BF16),
        jax.ShapeDtypeStruct((m, d_mix), BF16),
    )
    head_rows = pl.BlockSpec((tm * H_D, LANES), lambda i: (i, 0))
    out_specs = (
        qv_spec, row_spec(d_m), qv_spec, row_spec(LANES),
        pl.BlockSpec((N_GATE, tm), lambda i: (0, i)),
        row_spec(d_d), head_rows, row_spec(d_d), head_rows,
        pl.BlockSpec((1, d_d, tm), lambda i: (i, 0, 0)), row_spec(d_d),
        row_spec(d_mix),
    )
    n_tab = cos_tab.shape[0] // tm
    tab_spec = pl.BlockSpec((tm, LANES), lambda i: (i % n_tab, 0))
    in_specs = [row_spec(d_model), whole(wqt), whole(wvt), whole(wm), whole(wg), whole(wgt),
                whole(wd), whole(wgate), whole(bcol), whole(brow), tab_spec, tab_spec]
    return pl.pallas_call(
        functools.partial(_inproj_kernel, d_m=d_m, d_d=d_d, dh_m=dh_m, dqk=dqk,
                          transposed_qv=transposed_qv),
        grid=grid, in_specs=in_specs, out_specs=out_specs, out_shape=out_shape,
        compiler_params=_params(("parallel",)), name="input_projection",
    )(x2d, wqt, wvt, wm, wg, wgt, wd, wgate, bcol, brow, cos_tab, sin_tab)


def _mlstm_prompt_kernel(qt_ref, k_ref, vt_ref, gate_ref, grow_ref, gcol_ref, gb_ref,
                         h_ref, c_ref, n_ref, m_ref, ct_sc, *, dh, c):
    j = pl.program_id(1)

    @pl.when(j == 0)
    def _():
        ct_sc[...] = jnp.zeros_like(ct_sc)
        n_ref[...] = jnp.zeros_like(n_ref)
        m_ref[...] = jnp.zeros_like(m_ref)

    s_idx = lax.broadcasted_iota(jnp.int32, (c, c), 0)
    t_idx = lax.broadcasted_iota(jnp.int32, (c, c), 1)
    causal = s_idx <= t_idx
    lower = t_idx <= s_idx

    for cc in range(k_ref.shape[0] // c):
        _mlstm_chunk(slice(cc * c, (cc + 1) * c), causal, lower, qt_ref, k_ref, vt_ref, gate_ref,
                     grow_ref, gcol_ref, gb_ref, h_ref, n_ref, m_ref, ct_sc, dh=dh, c=c)

    @pl.when(j == pl.num_programs(1) - 1)
    def _():
        for h in range(H_M):
            c_ref[h] = ct_sc[h].T


def _mlstm_chunk(ts, causal, lower, qt_ref, k_ref, vt_ref, gate_ref, grow_ref, gcol_ref, gb_ref,
                 h_ref, n_ref, m_ref, ct_sc, *, dh, c):
    heads = []
    for h in range(H_M):
        sl = slice(h * dh, (h + 1) * dh)
        qt = qt_ref[sl, ts]
        k = k_ref[ts, sl]
        ig_row = grow_ref[h:h + 1, ts]
        lf_row = _log_sigmoid(grow_ref[H_M + h:H_M + h + 1, ts])
        ig_col = gcol_ref[ts, h:h + 1]
        lf_col = _log_sigmoid(gcol_ref[ts, H_M + h:H_M + h + 1])
        bcum_col = jnp.sum(jnp.where(lower, lf_row, 0.0), axis=1, keepdims=True)
        bcum_row = jnp.sum(jnp.where(causal, lf_col, 0.0), axis=0, keepdims=True)
        m_prev = m_ref[:, h:h + 1]
        dlog = jnp.where(causal, bcum_row - bcum_col + ig_col, -jnp.inf)
        inter = bcum_row + m_prev
        mt = jnp.maximum(jnp.max(dlog, axis=0, keepdims=True), inter)
        w_inter = jnp.exp(inter - mt)
        st = jnp.dot(k, qt, preferred_element_type=F32) * jnp.exp(dlog - mt)
        n_old = n_ref[h:h + 1, :]
        cq = jnp.dot(ct_sc[h].astype(BF16), qt, preferred_element_type=F32)
        nq2 = jnp.dot(_split_rows(n_old, ("hi", "lo")), qt, preferred_element_type=F32)
        m_new = mt[:, c - 1:c]
        b_last = bcum_row[:, c - 1:c]
        w_end = jnp.exp(b_last - bcum_row + ig_row - m_new)
        decay = jnp.exp(b_last + m_prev - m_new)
        heads.append(dict(sl=sl, mt=mt, w_inter=w_inter, st=st, n_old=n_old,
                          cq=cq, nq=nq2[0:1, :] + nq2[1:2, :], m_new=m_new, w_end=w_end,
                          decay=decay))

    for h, d in enumerate(heads):
        vt = vt_ref[d["sl"], ts]
        num = jnp.dot(vt, d["st"].astype(BF16), preferred_element_type=F32) + d["w_inter"] * d["cq"]
        den = jnp.sum(d["st"], axis=0, keepdims=True) + d["w_inter"] * d["nq"]
        d["hc"] = num * (1.0 / jnp.maximum(jnp.abs(den), jnp.exp(-d["mt"])))
        vtw = (vt.astype(F32) * d["w_end"]).astype(BF16)
        k = k_ref[ts, d["sl"]]
        ct_sc[h] = d["decay"] * ct_sc[h] + jnp.dot(vtw, k, preferred_element_type=F32)
        kw2 = jnp.dot(_split_rows(d["w_end"], ("hi", "lo")), k, preferred_element_type=F32)
        n_ref[h:h + 1, :] = d["decay"] * d["n_old"] + kw2[0:1, :] + kw2[1:2, :]
        m_ref[:, h:h + 1] = d["m_new"]

        sl = d["sl"]
        hc = d["hc"]
        mu = jnp.mean(hc, axis=0, keepdims=True)
        var = jnp.mean(jnp.square(hc - mu), axis=0, keepdims=True)
        hn = ((hc - mu) * lax.rsqrt(var + EPS) * gb_ref[sl, :]).T
        h_ref[ts, sl] = (hn * gate_ref[ts, sl]).astype(h_ref.dtype)


def _mlstm_prompt(q_t, k, v_t, gate, grow, gcol, g_norm, b, l):
    d_m = k.shape[1]
    dh = d_m // H_M
    nc = l // CHUNK
    tm = q_t.shape[2]
    per = tm // CHUNK
    ns = nc // per
    tok = pl.BlockSpec((tm, d_m), lambda i, j: (i * ns + j, 0))
    tok_t = pl.BlockSpec((None, d_m, tm), lambda i, j: (i * ns + j, 0, 0))
    g_bcast = jnp.broadcast_to(g_norm.reshape(d_m, 1), (d_m, CHUNK))
    in_specs = [tok_t, tok, tok_t, tok,
                pl.BlockSpec((N_GATE, tm), lambda i, j: (0, i * ns + j)),
                pl.BlockSpec((tm, LANES), lambda i, j: (i * ns + j, 0)),
                pl.BlockSpec((d_m, CHUNK), lambda i, j: (0, 0))]
    out_shape = (jax.ShapeDtypeStruct((b * l, d_m), BF16),
                 jax.ShapeDtypeStruct((b, H_M, dh, dh), F32),
                 jax.ShapeDtypeStruct((b, H_M, dh), F32),
                 jax.ShapeDtypeStruct((b, 1, LANES), F32))
    out_specs = (tok,
                 pl.BlockSpec((None, H_M, dh, dh), lambda i, j: (i, 0, 0, 0)),
                 pl.BlockSpec((None, H_M, dh), lambda i, j: (i, 0, 0)),
                 pl.BlockSpec((None, 1, LANES), lambda i, j: (i, 0, 0)))
    return pl.pallas_call(
        functools.partial(_mlstm_prompt_kernel, dh=dh, c=CHUNK),
        grid=(b, ns), in_specs=in_specs, out_specs=out_specs, out_shape=out_shape,
        scratch_shapes=[pltpu.VMEM((H_M, dh, dh), F32)],
        compiler_params=_params(("parallel", "arbitrary")), name="mlstm_prompt",
    )(q_t, k, v_t, gate, grow, gcol, g_bcast)


def _diff_prompt_kernel(lq1_ref, lk1_ref, lq2_ref, lk2_ref, q_ref, k_ref, vt_ref, gate_ref, g_ref,
                        o_ref, m_sc, l_sc, acc_sc, *, lam_init, dqk):
    tq = q_ref.shape[0]
    dv = 2 * dqk
    qi = pl.program_id(1)
    lam = _lambda(lq1_ref, lk1_ref, lq2_ref, lk2_ref, lam_init)
    lane = lax.broadcasted_iota(jnp.int32, (tq, dv), 1)
    key_i = lax.broadcasted_iota(jnp.int32, (tq, 2 * tq), 0)
    qry_i = lax.broadcasted_iota(jnp.int32, (tq, 2 * tq), 1)
    visible = key_i <= jnp.where(qry_i >= tq, qry_i - tq, qry_i)
    ones = jnp.ones((2 * SUBLANES, tq), BF16)
    q_both = []
    for h in range(H_D):
        q = q_ref[:, h * dv:(h + 1) * dv]
        zero = jnp.zeros_like(q)
        q_both.append(jnp.concatenate([jnp.where(lane < dqk, q, zero),
                                       jnp.where(lane >= dqk, q, zero)], axis=0))
    m_sc[...] = jnp.full(m_sc.shape, -jnp.inf, F32)
    l_sc[...] = jnp.zeros_like(l_sc)
    acc_sc[...] = jnp.zeros_like(acc_sc)

    def key_span(first_tile, n_tiles, on_diagonal):
        start = pl.multiple_of(first_tile * tq, tq)
        scores = []
        for h in range(H_D):
            hs = slice(h * dv, (h + 1) * dv)
            k = k_ref[pl.ds(start, n_tiles * tq), hs]
            scores.append(lax.dot_general(k, q_both[h], NT_DIMS,
                                          preferred_element_type=F32))
        for h in range(H_D):
            hs = slice(h * dv, (h + 1) * dv)
            st = scores[h]
            if on_diagonal:
                below = (n_tiles - 1) * tq
                diag = jnp.where(visible, st[below:, :], -jnp.inf)
                st = diag if n_tiles == 1 else jnp.concatenate([st[0:below, :], diag], axis=0)
            m_old = m_sc[h]
            m_new = jnp.maximum(m_old, jnp.max(st, axis=0, keepdims=True))
            alpha = jnp.exp(m_old - m_new)
            pt = jnp.exp(st - m_new).astype(BF16)
            pv = jnp.dot(jnp.concatenate([vt_ref[first_tile, hs, :], ones], axis=0), pt[0:tq],
                         preferred_element_type=F32)
            for t in range(1, n_tiles):
                pv += jnp.dot(jnp.concatenate([vt_ref[first_tile + t, hs, :], ones], axis=0),
                              pt[t * tq:(t + 1) * tq], preferred_element_type=F32)
            l_sc[h] = alpha * l_sc[h] + pv[dv:dv + 1, :]
            acc_sc[h] = alpha * acc_sc[h] + pv[0:dv, :]
            m_sc[h] = m_new

    def tile_pair(i, carry):
        key_span(2 * i, 2, False)
        return carry

    lax.fori_loop(0, qi // 2, tile_pair, 0)

    @pl.when(qi % 2 == 1)
    def _():
        key_span(qi - 1, 2, True)

    @pl.when(qi % 2 == 0)
    def _():
        key_span(qi, 1, True)

    for h in range(H_D):
        hs = slice(h * dv, (h + 1) * dv)
        on = acc_sc[h] / l_sc[h]
        o = (on[:, 0:tq] - lam * on[:, tq:2 * tq]).T
        o = o * lax.rsqrt(jnp.mean(o * o, axis=1, keepdims=True) + EPS) * g_ref[...]
        o_ref[:, hs] = (o * (1.0 - lam_init) * gate_ref[:, hs]).astype(o_ref.dtype)


def _diff_prompt(lams, dq, dk, dv_t, gate, g_norm, b, l, lam_init, gate_col0):
    d_d = dq.shape[1]
    dv_d = d_d // H_D
    tq = dv_t.shape[2]
    nq = l // tq
    lam_spec = pl.BlockSpec(lams[0].shape, lambda i, j: (0, 0))
    tok = pl.BlockSpec((tq, d_d), lambda i, j: (i * nq + j, 0))
    in_specs = [lam_spec] * 4 + [
        tok,
        pl.BlockSpec((l, d_d), lambda i, j: (i, 0)),
        pl.BlockSpec((nq, d_d, tq), lambda i, j: (i, 0, 0)),
        pl.BlockSpec((tq, d_d), lambda i, j: (i * nq + j, gate_col0 // d_d)),
        pl.BlockSpec((1, dv_d), lambda i, j: (0, 0))]
    return pl.pallas_call(
        functools.partial(_diff_prompt_kernel, lam_init=lam_init, dqk=dv_d // 2),
        grid=(b, nq), in_specs=in_specs, out_specs=tok,
        out_shape=jax.ShapeDtypeStruct((b * l, d_d), BF16),
        scratch_shapes=[pltpu.VMEM((H_D, 1, 2 * tq), F32), pltpu.VMEM((H_D, 1, 2 * tq), F32),
                        pltpu.VMEM((H_D, dv_d, 2 * tq), F32)],
        compiler_params=_params(("parallel", "parallel")), name="diff_attn_prompt",
    )(*lams, dq, dk, dv_t, gate, g_norm)


def _mem_kv_kernel(x_ref, w_ref, kf_ref, kb_ref, vf_ref, vb_ref):
    tm, d_c = kb_ref.shape
    dh = d_c // H_C
    x = x_ref[...]
    mk = jnp.dot(x, w_ref[:, 0:d_c], preferred_element_type=F32)
    mv = jnp.dot(x, w_ref[:, d_c:2 * d_c], preferred_element_type=F32)
    kb_ref[...] = mk.astype(kb_ref.dtype)
    vb_ref[...] = mv.astype(vb_ref.dtype)
    for h in range(H_C):
        sl = slice(h * dh, (h + 1) * dh)
        kf_ref[pl.ds(h, tm, stride=H_C), :] = mk[:, sl]
        vf_ref[pl.ds(h, tm, stride=H_C), :] = mv[:, sl]


def _mem_kv(mem2d, w, tm):
    m, d_model = mem2d.shape
    d_c = w.shape[1] // 2
    dh = d_c // H_C
    row = pl.BlockSpec((tm, d_c), lambda i: (i, 0))
    head_rows = pl.BlockSpec((tm * H_C, dh), lambda i: (i, 0))
    return pl.pallas_call(
        _mem_kv_kernel, grid=(m // tm,),
        in_specs=[pl.BlockSpec((tm, d_model), lambda i: (i, 0)),
                  pl.BlockSpec(w.shape, lambda i: (0, 0))],
        out_specs=(head_rows, row, head_rows, row),
        out_shape=(jax.ShapeDtypeStruct((m * H_C, dh), F32), jax.ShapeDtypeStruct((m, d_c), BF16),
                   jax.ShapeDtypeStruct((m * H_C, dh), F32), jax.ShapeDtypeStruct((m, d_c), BF16)),
        compiler_params=_params(("parallel",)), name="memory_kv",
    )(mem2d, w)


def _mem_prompt_kernel(q_ref, k_ref, v_ref, gate_ref, o_ref, *, dh):
    slices = [slice(h * dh, (h + 1) * dh) for h in range(H_C)]
    scores = [lax.dot_general(q_ref[:, sl], k_ref[:, sl], NT_DIMS, preferred_element_type=F32)
              for sl in slices]
    probs = []
    for s in scores:
        s = s * (dh ** -0.5)
        p = jnp.exp(s - jnp.max(s, axis=1, keepdims=True))
        probs.append((p / jnp.sum(p, axis=1, keepdims=True)).astype(BF16))
    for p, sl in zip(probs, slices):
        o = jnp.dot(p, v_ref[:, sl], preferred_element_type=F32)
        o_ref[:, sl] = (o * gate_ref[:, sl]).astype(o_ref.dtype)


def _mem_prompt(cq, mk, mv, gate, b, l, n_mem, tl, gate_col0):
    d_c = cq.shape[1]
    nl = l // tl
    tok = pl.BlockSpec((tl, d_c), lambda i, j: (i * nl + j, 0))
    mem = pl.BlockSpec((n_mem, d_c), lambda i, j: (i, 0))
    return pl.pallas_call(
        functools.partial(_mem_prompt_kernel, dh=d_c // H_C),
        grid=(b, nl),
        in_specs=[tok, mem, mem,
                  pl.BlockSpec((tl, d_c), lambda i, j: (i * nl + j, gate_col0 // d_c))],
        out_specs=tok, out_shape=jax.ShapeDtypeStruct((b * l, d_c), BF16),
        compiler_params=_params(("parallel", "parallel")), name="memory_attn_prompt",
    )(cq, mk, mv, gate)


def _merge_kernel(x_ref, hm_ref, hd_ref, hc_ref, w_ref, g_ref, b_ref, y_ref, *, alpha):
    tm = x_ref.shape[0]
    n_strips = max(1, tm // 256)
    rows = tm // n_strips
    for r in range(n_strips):
        rs = slice(r * rows, (r + 1) * rows)
        h = jnp.concatenate([hm_ref[rs, :], hd_ref[rs, :], hc_ref[rs, :]], axis=1)
        y = alpha * x_ref[rs, :] + jnp.dot(h, w_ref[...], preferred_element_type=F32)
        mu = jnp.mean(y, axis=1, keepdims=True)
        var = jnp.mean(jnp.square(y - mu), axis=1, keepdims=True)
        y_ref[rs, :] = (y - mu) * lax.rsqrt(var + EPS) * g_ref[...] + b_ref[...]


def _merge(x2d, h_m, h_d, h_c, w_out, ln_g, ln_b, tm, alpha):
    m, d_model = x2d.shape

    def row(n):
        return pl.BlockSpec((tm, n), lambda i: (i, 0))

    def whole(a):
        return pl.BlockSpec(a.shape, lambda i: (0, 0))

    return pl.pallas_call(
        functools.partial(_merge_kernel, alpha=alpha), grid=(m // tm,),
        in_specs=[row(d_model), row(h_m.shape[1]), row(h_d.shape[1]), row(h_c.shape[1]),
                  whole(w_out), whole(ln_g), whole(ln_b)],
        out_specs=row(d_model), out_shape=jax.ShapeDtypeStruct((m, d_model), F32),
        compiler_params=_params(("parallel",)), name="output_projection",
    )(x2d, h_m, h_d, h_c, w_out, ln_g, ln_b)


def _mlstm_step_kernel(q_ref, k_ref, v_ref, gate_ref, ig_ref, fg_ref, m_ref, n_ref, c_ref,
                       g_ref, h_ref, cn_ref, nn_ref, mn_ref, *, dh):
    def one_sequence(b, carry):
        for h in range(H_M):
            sl = slice(h * dh, (h + 1) * dh)
            q = q_ref[b, :, sl]
            k = k_ref[b, :, sl]
            v = v_ref[b, :, sl]
            ig = ig_ref[b, h]
            lf = _log_sigmoid(fg_ref[b, h])
            m_prev = m_ref[b, h]
            n_old = n_ref[b, h]
            c_old = c_ref[b, h]

            inter = lf + m_prev
            mt = jnp.maximum(ig, inter)
            w_in = jnp.exp(ig - mt)
            w_st = jnp.exp(inter - mt)
            s = jnp.sum(q * k, axis=1, keepdims=True) * w_in
            qc2 = jnp.dot(_split_rows(q, ("hi", "lo")), c_old.astype(BF16),
                          preferred_element_type=F32)
            qc = qc2[0:1, :] + qc2[1:2, :]
            outer = lax.dot_general(_split_rows(k, ("hi", "hi", "lo")),
                                    _split_rows(v, ("hi", "lo", "hi")), TN_DIMS,
                                    preferred_element_type=F32)
            num = s * v + w_st * qc
            den = s + w_st * jnp.sum(q * n_old, axis=1, keepdims=True)
            hc = num / jnp.maximum(jnp.abs(den), jnp.exp(-mt))

            cn_ref[b, h] = w_st * c_old + w_in * outer
            nn_ref[b, h] = w_st * n_old + w_in * k
            mn_ref[b, h] = mt

            mu = jnp.mean(hc, axis=1, keepdims=True)
            var = jnp.mean(jnp.square(hc - mu), axis=1, keepdims=True)
            hn = (hc - mu) * lax.rsqrt(var + EPS) * g_ref[:, sl]
            h_ref[b, :, sl] = (hn * gate_ref[b, :, sl]).astype(h_ref.dtype)
        return carry

    lax.fori_loop(0, q_ref.shape[0], one_sequence, 0)


def _mlstm_step(q, k, v, gate, ig, fg, m0, n0, c0, g_norm, tb):
    bd, _, d_m = q.shape
    dh = d_m // H_M

    def blk(shape):
        nd = len(shape)
        return pl.BlockSpec((tb,) + tuple(shape), lambda i: (i,) + (0,) * nd)

    tok = blk((1, d_m))
    one = blk((H_M, 1, 1))
    in_specs = [tok, tok, tok, tok, one, one, one, blk((H_M, 1, dh)), blk((H_M, dh, dh)),
                pl.BlockSpec((1, d_m), lambda i: (0, 0))]
    out_shape = (jax.ShapeDtypeStruct((bd, 1, d_m), BF16),
                 jax.ShapeDtypeStruct((bd, H_M, dh, dh), F32),
                 jax.ShapeDtypeStruct((bd, H_M, 1, dh), F32),
                 jax.ShapeDtypeStruct((bd, H_M, 1, 1), F32))
    out_specs = (tok, blk((H_M, dh, dh)), blk((H_M, 1, dh)), one)
    return pl.pallas_call(
        functools.partial(_mlstm_step_kernel, dh=dh), grid=(bd // tb,),
        in_specs=in_specs, out_specs=out_specs, out_shape=out_shape,
        compiler_params=_params(("parallel",)), name="mlstm_step",
    )(q, k, v, gate, ig, fg, m0, n0, c0, g_norm)


def _head_rows(row, n_heads, reps):
    dh = row.shape[1] // n_heads
    heads = [row[:, h * dh:(h + 1) * dh] for h in range(n_heads)]
    return jnp.concatenate(heads * reps, axis=0)


def _own_head(n_rows, n_cols, n_heads):
    r = lax.broadcasted_iota(jnp.int32, (n_rows, n_cols), 0)
    c = lax.broadcasted_iota(jnp.int32, (n_rows, n_cols), 1)
    return (c % n_heads) == (r % n_heads)


def _diff_decode_kernel(pt_ref, lq1_ref, lk1_ref, lq2_ref, lk2_ref, q_ref, kn_ref, vn_ref,
                        gate_ref, g_ref, *rest, n_pages, lam_init):
    del pt_ref
    tb = q_ref.shape[0]
    k_refs = rest[:tb * n_pages]
    v_refs = rest[tb * n_pages:2 * tb * n_pages]
    o_ref = rest[2 * tb * n_pages]
    n_rows, dv_d = k_refs[0].shape
    dqk = dv_d // 2
    lam = _lambda(lq1_ref, lk1_ref, lq2_ref, lk2_ref, lam_init)
    r_idx = lax.broadcasted_iota(jnp.int32, (2 * H_D, dv_d), 0)
    l_idx = lax.broadcasted_iota(jnp.int32, (2 * H_D, dv_d), 1)
    own_map = (l_idx // dqk) == (r_idx // H_D)
    own = _own_head(2 * H_D, n_rows, H_D)

    q8s, scores = [], []
    for b in range(tb):
        q8 = jnp.where(own_map, _head_rows(q_ref[b].astype(F32), H_D, 2), 0.0)
        q8b = q8.astype(BF16)
        q8s.append(q8)
        scores.append([jnp.where(own, lax.dot_general(q8b, k_refs[b * n_pages + j][...].astype(BF16),
                                                      NT_DIMS, preferred_element_type=F32), -jnp.inf)
                       for j in range(n_pages)])

    weights = []
    for b in range(tb):
        kn4 = kn_ref[b]
        s_new = jnp.sum(q8s[b] * jnp.concatenate([kn4, kn4], axis=0), axis=1, keepdims=True)
        mx = s_new
        for s in scores[b]:
            mx = jnp.maximum(mx, jnp.max(s, axis=1, keepdims=True))
        probs = [jnp.exp(s - mx) for s in scores[b]]
        p_new = jnp.exp(s_new - mx)
        total = p_new
        for p in probs:
            total = total + jnp.sum(p, axis=1, keepdims=True)
        inv = 1.0 / total

        def combine(p, inv=inv):
            pn = p * inv
            return pn - lam * pltpu.roll(pn, SUBLANES - H_D, 0)

        a_new = combine(jnp.broadcast_to(p_new, (SUBLANES, LANES)))[:, 0:1]
        weights.append((a_new, [combine(p).astype(BF16) for p in probs]))

    for b in range(tb):
        a_new, a_pages = weights[b]
        vn4 = vn_ref[b]
        acc = a_new * jnp.concatenate([vn4, vn4], axis=0)
        for j in range(n_pages):
            acc = acc + jnp.dot(a_pages[j], v_refs[b * n_pages + j][...].astype(BF16),
                                preferred_element_type=F32)
        for h in range(H_D):
            sl = slice(h * dv_d, (h + 1) * dv_d)
            o = acc[h:h + 1, :]
            o = o * lax.rsqrt(jnp.mean(o * o, axis=1, keepdims=True) + EPS) * g_ref[...]
            o_ref[b, :, sl] = (o * (1.0 - lam_init) * gate_ref[b, :, sl]).astype(o_ref.dtype)


def _diff_decode(page_table, lams, dq, dk_new, dv_new, gate, g_norm, cache_k, cache_v,
                 lam_init, gate_col0, tb):
    bd, _, d_d = dq.shape
    n_pages = page_table.shape[1]
    _, n_rows, dv_d = cache_k.shape
    lam_spec = pl.BlockSpec(lams[0].shape, lambda i, pt: (0, 0))
    tok = pl.BlockSpec((tb, 1, d_d), lambda i, pt: (i, 0, 0))
    page_specs = [pl.BlockSpec((None, n_rows, dv_d),
                               lambda i, pt, b=b, j=j: (pt[i * tb + b, j], 0, 0))
                  for b in range(tb) for j in range(n_pages)]
    new_tok = pl.BlockSpec((tb, H_D, dv_d), lambda i, pt: (i, 0, 0))
    in_specs = ([lam_spec] * 4 + [tok, new_tok, new_tok,
                                  pl.BlockSpec((tb, 1, d_d), lambda i, pt: (i, 0, gate_col0 // d_d)),
                                  pl.BlockSpec((1, dv_d), lambda i, pt: (0, 0))]
                + page_specs + page_specs)
    grid_spec = pltpu.PrefetchScalarGridSpec(
        num_scalar_prefetch=1, grid=(bd // tb,), in_specs=in_specs, out_specs=tok)
    n_ops = tb * n_pages
    return pl.pallas_call(
        functools.partial(_diff_decode_kernel, n_pages=n_pages, lam_init=lam_init),
        grid_spec=grid_spec, out_shape=jax.ShapeDtypeStruct((bd, 1, d_d), BF16),
        compiler_params=_params(("parallel",)), name="diff_attn_decode",
    )(page_table, *lams, dq, dk_new, dv_new, gate, g_norm,
      *([cache_k] * n_ops), *([cache_v] * n_ops))


def _mem_decode_kernel(q_ref, k_ref, v_ref, gate_ref, o_ref):
    tb, n_rows, dh = k_ref.shape
    own = _own_head(SUBLANES, n_rows, H_C)
    scores = []
    for b in range(tb):
        q8 = _head_rows(q_ref[b].astype(F32), H_C, SUBLANES // H_C).astype(BF16)
        scores.append(lax.dot_general(q8, k_ref[b].astype(BF16), NT_DIMS,
                                      preferred_element_type=F32))
    probs = []
    for s in scores:
        s = jnp.where(own, s * (dh ** -0.5), -jnp.inf)
        p = jnp.exp(s - jnp.max(s, axis=1, keepdims=True))
        probs.append((p / jnp.sum(p, axis=1, keepdims=True)).astype(BF16))
    for b in range(tb):
        acc = jnp.dot(probs[b], v_ref[b].astype(BF16), preferred_element_type=F32)
        for h in range(H_C):
            sl = slice(h * dh, (h + 1) * dh)
            o_ref[b, :, sl] = (acc[h:h + 1, :] * gate_ref[b, :, sl]).astype(o_ref.dtype)


def _mem_decode(cq, mem_k, mem_v, gate, gate_col0, tb):
    bd, _, d_c = cq.shape
    _, n_rows, dh = mem_k.shape
    tok = pl.BlockSpec((tb, 1, d_c), lambda i: (i, 0, 0))
    mem = pl.BlockSpec((tb, n_rows, dh), lambda i: (i, 0, 0))
    return pl.pallas_call(
        _mem_decode_kernel, grid=(bd // tb,),
        in_specs=[tok, mem, mem, pl.BlockSpec((tb, 1, d_c), lambda i: (i, 0, gate_col0 // d_c))],
        out_specs=tok, out_shape=jax.ShapeDtypeStruct((bd, 1, d_c), BF16),
        compiler_params=_params(("parallel",)), name="memory_attn_decode",
    )(cq, mem_k, mem_v, gate)


def _rope_tables(pos, n_rows, dqk):
    half = dqk // 2
    lane = jnp.arange(LANES)
    inv_freq = ROPE_THETA ** (-(lane % half).astype(F32) / half)
    ang = pos.astype(F32)[:, None] * inv_freq[None, :]
    sin_t = jnp.where((lane % dqk) < half, -jnp.sin(ang), jnp.sin(ang))
    return (jnp.broadcast_to(jnp.cos(ang), (n_rows, LANES)), jnp.broadcast_to(sin_t, (n_rows, LANES)))


class _Tiles(NamedTuple):
    token: int = 256
    merge_rows: int = 1024
    mem_kv_rows: int = 512
    mem_q_rows: int = 1024
    step_seqs: int = 8
    decode_seqs: int = 2
    mem_decode_seqs: int = 8


def _split_weights(w_in_l, b_if_l, d_m, d_d):
    g0 = 4 * d_m
    d0 = g0 + N_GATE
    wqt = w_in_l[:, 0:d_m].T.astype(BF16)
    wvt = w_in_l[:, 2 * d_m:3 * d_m].T.astype(BF16)
    wm = jnp.concatenate([w_in_l[:, d_m:2 * d_m], w_in_l[:, 3 * d_m:g0]], axis=1).astype(BF16)
    w_gates = w_in_l[:, g0:d0]
    wg = jnp.pad(w_gates, ((0, 0), (0, LANES - N_GATE))).astype(BF16)
    wgt = w_gates.T.astype(BF16)
    wd = w_in_l[:, d0:d0 + 4 * d_d].astype(BF16)
    wgate = w_in_l[:, d0 + 4 * d_d:].astype(BF16)
    bcol = jnp.pad(b_if_l, (0, LANES - N_GATE)).reshape(1, LANES).astype(F32)
    brow = jnp.broadcast_to(b_if_l.astype(F32)[:, None], (N_GATE, LANES))
    return wqt, wvt, wm, wg, wgt, wd, wgate, bcol, brow


def kernel(x_prompt, x_sample, state_mlstm_c, state_mlstm_n, state_mlstm_m, cache_diff_k, cache_diff_v,
           cache_mem_k, cache_mem_v, page_table, mem_prompt, w_in, b_if, w_mlstm_norm,
           lambda_q1, lambda_k1, lambda_q2, lambda_k2, w_diff_norm, w_mem_kv, w_out, ln_g, ln_b):
    depth = w_in.shape[0]
    assert depth == 1, "single-layer trunk"
    bp, lp, d_model = x_prompt.shape
    bd, ls, _ = x_sample.shape
    assert ls == 1
    d_mix = w_out.shape[1]
    d_m = d_mix // 2
    d_d = d_mix // 4
    d_c = d_mix // 4
    dh_m = d_m // H_M
    dv_d = d_d // H_D
    dqk = dv_d // 2
    assert dv_d == LANES and d_c // H_C == LANES, "one attention head per lane tile"
    n_mem = mem_prompt.shape[1]
    n_pool, page = cache_diff_k.shape[1], cache_diff_k.shape[2]
    past_len = page_table.shape[1] * page
    alpha = (2 * depth) ** 0.25
    layer = 0
    lam_init = 0.8 - 0.6 * math.exp(-0.3 * layer)

    tiles = _Tiles()
    wts = _split_weights(w_in[layer], b_if[layer], d_m, d_d)
    lams = tuple(a[layer].reshape(1, dqk).astype(F32)
                 for a in (lambda_q1, lambda_k1, lambda_q2, lambda_k2))
    g_m = w_mlstm_norm[layer].reshape(1, d_m).astype(F32)
    g_d = w_diff_norm[layer].reshape(1, dv_d).astype(F32)
    w_out_b = w_out[layer].astype(BF16)
    g_ln = ln_g[layer].reshape(1, d_model).astype(F32)
    b_ln = ln_b[layer].reshape(1, d_model).astype(F32)
    gate_d0 = d_m
    gate_c0 = d_m + d_d

    mp = bp * lp
    xp2d = x_prompt.reshape(mp, d_model)
    cos_p, sin_p = _rope_tables(jnp.arange(lp), lp, dqk)
    (q_m, k_m, v_m, gcol, grow, dq, dk_f, dk_b, dv_f, dv_t, cq, gate) = _input_projection(
        xp2d, wts, cos_p, sin_p, tm=tiles.token, qkv_dtype=BF16, transposed_qv=True)
    h_m, c_p, n_p, m_p = _mlstm_prompt(q_m, k_m, v_m, gate, grow, gcol, g_m, bp, lp)
    h_d = _diff_prompt(lams, dq, dk_b, dv_t, gate, g_d, bp, lp, lam_init, gate_d0)
    mk_f, mk_b, mv_f, mv_b = _mem_kv(mem_prompt.reshape(bp * n_mem, d_model).astype(BF16),
                                     w_mem_kv[layer].astype(BF16), tm=tiles.mem_kv_rows)
    h_c = _mem_prompt(cq, mk_b, mv_b, gate, bp, lp, n_mem, tiles.mem_q_rows, gate_c0)
    y_p = _merge(xp2d, h_m, h_d, h_c, w_out_b, g_ln, b_ln, tiles.merge_rows, alpha)

    xs2d = x_sample.reshape(bd, d_model)
    cos_s, sin_s = _rope_tables(jnp.full((1,), past_len), bd, dqk)
    (qs, ks, vs, gcol_s, _, dq_s, dk_s, _, dv_s, _, cq_s, gate_s) = _input_projection(
        xs2d, wts, cos_s, sin_s, tm=bd, qkv_dtype=F32, transposed_qv=False)

    def tok3(a):
        return a.reshape(bd, 1, a.shape[1])

    ig_s = gcol_s[:, 0:H_M].reshape(bd, H_M, 1, 1)
    fg_s = gcol_s[:, H_M:N_GATE].reshape(bd, H_M, 1, 1)
    hs_m, c_s, n_s, m_s = _mlstm_step(
        tok3(qs), tok3(ks), tok3(vs), tok3(gate_s), ig_s, fg_s,
        state_mlstm_m[layer].astype(F32).reshape(bd, H_M, 1, 1),
        state_mlstm_n[layer].astype(F32).reshape(bd, H_M, 1, dh_m),
        state_mlstm_c[layer].astype(F32), g_m, tb=tiles.step_seqs)
    hs_d = _diff_decode(page_table, lams, tok3(dq_s), dk_s.reshape(bd, H_D, dv_d),
                        dv_s.reshape(bd, H_D, dv_d), tok3(gate_s), g_d,
                        cache_diff_k[layer].reshape(n_pool, page * H_D, dv_d),
                        cache_diff_v[layer].reshape(n_pool, page * H_D, dv_d), lam_init, gate_d0,
                        tb=tiles.decode_seqs)
    hs_c = _mem_decode(tok3(cq_s), cache_mem_k[layer].reshape(bd, n_mem * H_C, d_c // H_C),
                       cache_mem_v[layer].reshape(bd, n_mem * H_C, d_c // H_C), tok3(gate_s),
                       gate_c0, tb=tiles.mem_decode_seqs)
    y_s = _merge(xs2d, hs_m.reshape(bd, d_m), hs_d.reshape(bd, d_d), hs_c.reshape(bd, d_c),
                 w_out_b, g_ln, b_ln, bd, alpha)

    return (y_p.reshape(bp, lp, d_model), y_s.reshape(bd, ls, d_model),
            c_p[None], n_p[None], m_p[:, 0, 0:H_M][None],
            dk_f.reshape(1, bp, lp, H_D, dv_d), dv_f.reshape(1, bp, lp, H_D, dv_d),
            mk_f.reshape(1, bp, n_mem, H_C, d_c // H_C), mv_f.reshape(1, bp, n_mem, H_C, d_c // H_C),
            c_s[None], n_s.reshape(1, bd, H_M, dh_m), m_s.reshape(1, bd, H_M),
            dk_s.reshape(1, bd, ls, H_D, dv_d), dv_s.reshape(1, bd, ls, H_D, dv_d))
```

```python
import functools
import math
from typing import NamedTuple

import jax
import jax.numpy as jnp
from jax import lax
from jax.experimental import pallas as pl
from jax.experimental.pallas import tpu as pltpu

F32 = jnp.float32
BF16 = jnp.bfloat16

H_M = 4
H_D = 4
H_C = 4
N_GATE = 2 * H_M
CHUNK = 128
ROPE_THETA = 10000.0
EPS = 1e-5
LANES = 128
SUBLANES = 8
VMEM_LIMIT = 56 * 1024 * 1024

NT_DIMS = (((1,), (1,)), ((), ()))
TN_DIMS = (((0,), (0,)), ((), ()))


def _params(semantics):
    return pltpu.CompilerParams(dimension_semantics=semantics, vmem_limit_bytes=VMEM_LIMIT)


def _log_sigmoid(x):
    return jnp.minimum(x, 0.0) - jnp.log1p(jnp.exp(-jnp.abs(x)))


def _silu(x):
    return x * jax.nn.sigmoid(x)


def _split_rows(row, order):
    n_rows = 2 * SUBLANES
    r_idx = lax.broadcasted_iota(jnp.int32, (n_rows, row.shape[1]), 0)
    hi = row.astype(BF16).astype(F32)
    parts = {"hi": hi, "lo": row - hi}
    out = jnp.zeros((n_rows, row.shape[1]), F32)
    for i, name in enumerate(order):
        out = jnp.where(r_idx == i, parts[name], out)
    return out.astype(BF16)


def _lambda(lq1_ref, lk1_ref, lq2_ref, lk2_ref, lam_init):
    a = jnp.sum(lq1_ref[...] * lk1_ref[...], axis=1, keepdims=True)
    b = jnp.sum(lq2_ref[...] * lk2_ref[...], axis=1, keepdims=True)
    return jnp.exp(a) - jnp.exp(b) + lam_init


def _inproj_kernel(x_ref, wqt_ref, wvt_ref, wm_ref, wg_ref, wgt_ref, wd_ref, wgate_ref,
                   bcol_ref, brow_ref,
                   cos_ref, sin_ref,
                   q_ref, k_ref, v_ref, gcol_ref, grow_ref,
                   dq_ref, dkf_ref, dkb_ref, dvf_ref, dvt_ref, cq_ref, gate_ref,
                   *, d_m, d_d, dh_m, dqk, transposed_qv):
    x = x_ref[...].astype(BF16)

    def mm(w):
        return jnp.dot(x, w, preferred_element_type=F32)

    def mm_t(a, b):
        return lax.dot_general(a, b, NT_DIMS, preferred_element_type=F32)

    if transposed_qv:
        q_ref[0] = mm_t(wqt_ref[...], x).astype(q_ref.dtype)
        v_ref[0] = mm_t(wvt_ref[...], x).astype(v_ref.dtype)
    else:
        q_ref[...] = mm_t(x, wqt_ref[...]).astype(q_ref.dtype)
        v_ref[...] = mm_t(x, wvt_ref[...]).astype(v_ref.dtype)
    k_ref[...] = (mm(wm_ref[:, 0:d_m]) * (dh_m ** -0.5)).astype(k_ref.dtype)
    gate_ref[:, 0:d_m] = (jax.nn.sigmoid(mm(wm_ref[:, d_m:2 * d_m]))
                          * _silu(mm(wgate_ref[:, 0:d_m]))).astype(gate_ref.dtype)
    gate_ref[:, d_m:] = _silu(mm(wgate_ref[:, d_m:])).astype(gate_ref.dtype)
    gcol_ref[...] = mm(wg_ref[...]) + bcol_ref[...]
    grow_ref[...] = (lax.dot_general(wgt_ref[...], x, NT_DIMS, preferred_element_type=F32)
                     + brow_ref[:, 0:1])

    cos = cos_ref[...]
    sin = sin_ref[...]
    lane = lax.broadcasted_iota(jnp.int32, cos.shape, 1)
    first_half = (lane % dqk) < (dqk // 2)

    def rope(t):
        swapped = jnp.where(first_half, pltpu.roll(t, LANES - dqk // 2, 1),
                            pltpu.roll(t, dqk // 2, 1))
        return t * cos + swapped * sin

    tm = x.shape[0]
    dq = mm(wd_ref[:, 0:d_d])
    dk = mm(wd_ref[:, d_d:2 * d_d])
    dv = mm(wd_ref[:, 2 * d_d:3 * d_d])
    dvt_ref[0] = dv.T.astype(dvt_ref.dtype)
    for h in range(H_D):
        sl = slice(h * LANES, (h + 1) * LANES)
        dq_ref[:, sl] = (rope(dq[:, sl]) * (dqk ** -0.5)).astype(dq_ref.dtype)
        dk_h = rope(dk[:, sl])
        dkb_ref[:, sl] = dk_h.astype(dkb_ref.dtype)
        dkf_ref[pl.ds(h, tm, stride=H_D), :] = dk_h
        dvf_ref[pl.ds(h, tm, stride=H_D), :] = dv[:, sl]
    cq_ref[...] = mm(wd_ref[:, 3 * d_d:4 * d_d]).astype(cq_ref.dtype)


def _input_projection(x2d, wts, cos_tab, sin_tab, tm, qkv_dtype, transposed_qv):
    m, d_model = x2d.shape
    wqt, wvt, wm, wg, wgt, wd, wgate, bcol, brow = wts
    d_m = wqt.shape[0]
    d_d = wd.shape[1] // 4
    d_mix = wgate.shape[1]
    dh_m = d_m // H_M
    dqk = d_d // H_D // 2
    grid = (m // tm,)

    def row_spec(n):
        return pl.BlockSpec((tm, n), lambda i: (i, 0))

    def whole(a):
        return pl.BlockSpec(a.shape, lambda i: (0,) * a.ndim, pipeline_mode=pl.Buffered(1))

    if transposed_qv:
        qv_shape = jax.ShapeDtypeStruct((m // tm, d_m, tm), qkv_dtype)
        qv_spec = pl.BlockSpec((1, d_m, tm), lambda i: (i, 0, 0))
    else:
        qv_shape = jax.ShapeDtypeStruct((m, d_m), qkv_dtype)
        qv_spec = row_spec(d_m)
    out_shape = (
        qv_shape,
        jax.ShapeDtypeStruct((m, d_m), qkv_dtype),
        qv_shape,
        jax.ShapeDtypeStruct((m, LANES), F32),
        jax.ShapeDtypeStruct((N_GATE, m), F32),
        jax.ShapeDtypeStruct((m, d_d), BF16),
        jax.ShapeDtypeStruct((m * H_D, LANES), F32),
        jax.ShapeDtypeStruct((m, d_d), BF16),
        jax.ShapeDtypeStruct((m * H_D, LANES), F32),
        jax.ShapeDtypeStruct((m // tm, d_d, tm), BF16),
        jax.ShapeDtypeStruct((m, d_d), BF16),
        jax.ShapeDtypeStruct((m, d_mix), BF16),
    )
    head_rows = pl.BlockSpec((tm * H_D, LANES), lambda i: (i, 0))
    out_specs = (
        qv_spec, row_spec(d_m), qv_spec, row_spec(LANES),
        pl.BlockSpec((N_GATE, tm), lambda i: (0, i)),
        row_spec(d_d), head_rows, row_spec(d_d), head_rows,
        pl.BlockSpec((1, d_d, tm), lambda i: (i, 0, 0)), row_spec(d_d),
        row_spec(d_mix),
    )
    n_tab = cos_tab.shape[0] // tm
    tab_spec = pl.BlockSpec((tm, LANES), lambda i: (i % n_tab, 0))
    in_specs = [row_spec(d_model), whole(wqt), whole(wvt), whole(wm), whole(wg), whole(wgt),
                whole(wd), whole(wgate), whole(bcol), whole(brow), tab_spec, tab_spec]
    return pl.pallas_call(
        functools.partial(_inproj_kernel, d_m=d_m, d_d=d_d, dh_m=dh_m, dqk=dqk,
                          transposed_qv=transposed_qv),
        grid=grid, in_specs=in_specs, out_specs=out_specs, out_shape=out_shape,
        compiler_params=_params(("parallel",)), name="input_projection",
    )(x2d, wqt, wvt, wm, wg, wgt, wd, wgate, bcol, brow, cos_tab, sin_tab)


def _mlstm_prompt_kernel(qt_ref, k_ref, vt_ref, gate_ref, grow_ref, gcol_ref, gb_ref,
                         h_ref, c_ref, n_ref, m_ref, ct_sc, *, dh, c):
    j = pl.program_id(1)

    @pl.when(j == 0)
    def _():
        ct_sc[...] = jnp.zeros_like(ct_sc)
        n_ref[...] = jnp.zeros_like(n_ref)
        m_ref[...] = jnp.zeros_like(m_ref)

    s_idx = lax.broadcasted_iota(jnp.int32, (c, c), 0)
    t_idx = lax.broadcasted_iota(jnp.int32, (c, c), 1)
    causal = s_idx <= t_idx
    lower = t_idx <= s_idx

    for cc in range(k_ref.shape[0] // c):
        _mlstm_chunk(slice(cc * c, (cc + 1) * c), causal, lower, qt_ref, k_ref, vt_ref, gate_ref,
                     grow_ref, gcol_ref, gb_ref, h_ref, n_ref, m_ref, ct_sc, dh=dh, c=c)

    @pl.when(j == pl.num_programs(1) - 1)
    def _():
        for h in range(H_M):
            c_ref[h] = ct_sc[h].T


def _mlstm_chunk(ts, causal, lower, qt_ref, k_ref, vt_ref, gate_ref, grow_ref, gcol_ref, gb_ref,
                 h_ref, n_ref, m_ref, ct_sc, *, dh, c):
    heads = []
    for h in range(H_M):
        sl = slice(h * dh, (h + 1) * dh)
        qt = qt_ref[sl, ts]
        k = k_ref[ts, sl]
        ig_row = grow_ref[h:h + 1, ts]
        lf_row = _log_sigmoid(grow_ref[H_M + h:H_M + h + 1, ts])
        ig_col = gcol_ref[ts, h:h + 1]
        lf_col = _log_sigmoid(gcol_ref[ts, H_M + h:H_M + h + 1])
        bcum_col = jnp.sum(jnp.where(lower, lf_row, 0.0), axis=1, keepdims=True)
        bcum_row = jnp.sum(jnp.where(causal, lf_col, 0.0), axis=0, keepdims=True)
        m_prev = m_ref[:, h:h + 1]
        dlog = jnp.where(causal, bcum_row - bcum_col + ig_col, -jnp.inf)
        inter = bcum_row + m_prev
        mt = jnp.maximum(jnp.max(dlog, axis=0, keepdims=True), inter)
        w_inter = jnp.exp(inter - mt)
        st = jnp.dot(k, qt, preferred_element_type=F32) * jnp.exp(dlog - mt)
        n_old = n_ref[h:h + 1, :]
        cq = jnp.dot(ct_sc[h].astype(BF16), qt, preferred_element_type=F32)
        nq2 = jnp.dot(_split_rows(n_old, ("hi", "lo")), qt, preferred_element_type=F32)
        m_new = mt[:, c - 1:c]
        b_last = bcum_row[:, c - 1:c]
        w_end = jnp.exp(b_last - bcum_row + ig_row - m_new)
        decay = jnp.exp(b_last + m_prev - m_new)
        heads.append(dict(sl=sl, mt=mt, w_inter=w_inter, st=st, n_old=n_old,
                          cq=cq, nq=nq2[0:1, :] + nq2[1:2, :], m_new=m_new, w_end=w_end,
                          decay=decay))

    for h, d in enumerate(heads):
        vt = vt_ref[d["sl"], ts]
        num = jnp.dot(vt, d["st"].astype(BF16), preferred_element_type=F32) + d["w_inter"] * d["cq"]
        den = jnp.sum(d["st"], axis=0, keepdims=True) + d["w_inter"] * d["nq"]
        d["hc"] = num * (1.0 / jnp.maximum(jnp.abs(den), jnp.exp(-d["mt"])))
        vtw = (vt.astype(F32) * d["w_end"]).astype(BF16)
        k = k_ref[ts, d["sl"]]
        ct_sc[h] = d["decay"] * ct_sc[h] + jnp.dot(vtw, k, preferred_element_type=F32)
        kw2 = jnp.dot(_split_rows(d["w_end"], ("hi", "lo")), k, preferred_element_type=F32)
        n_ref[h:h + 1, :] = d["decay"] * d["n_old"] + kw2[0:1, :] + kw2[1:2, :]
        m_ref[:, h:h + 1] = d["m_new"]

        sl = d["sl"]
        hc = d["hc"]
        mu = jnp.mean(hc, axis=0, keepdims=True)
        var = jnp.mean(jnp.square(hc - mu), axis=0, keepdims=True)
        hn = ((hc - mu) * lax.rsqrt(var + EPS) * gb_ref[sl, :]).T
        h_ref[ts, sl] = (hn * gate_ref[ts, sl]).astype(h_ref.dtype)


def _mlstm_prompt(q_t, k, v_t, gate, grow, gcol, g_norm, b, l):
    d_m = k.shape[1]
    dh = d_m // H_M
    nc = l // CHUNK
    tm = q_t.shape[2]
    per = tm // CHUNK
    ns = nc // per
    tok = pl.BlockSpec((tm, d_m), lambda i, j: (i * ns + j, 0))
    tok_t = pl.BlockSpec((None, d_m, tm), lambda i, j: (i * ns + j, 0, 0))
    g_bcast = jnp.broadcast_to(g_norm.reshape(d_m, 1), (d_m, CHUNK))
    in_specs = [tok_t, tok, tok_t, tok,
                pl.BlockSpec((N_GATE, tm), lambda i, j: (0, i * ns + j)),
                pl.BlockSpec((tm, LANES), lambda i, j: (i * ns + j, 0)),
                pl.BlockSpec((d_m, CHUNK), lambda i, j: (0, 0))]
    out_shape = (jax.ShapeDtypeStruct((b * l, d_m), BF16),
                 jax.ShapeDtypeStruct((b, H_M, dh, dh), F32),
                 jax.ShapeDtypeStruct((b, H_M, dh), F32),
                 jax.ShapeDtypeStruct((b, 1, LANES), F32))
    out_specs = (tok,
                 pl.BlockSpec((None, H_M, dh, dh), lambda i, j: (i, 0, 0, 0)),
                 pl.BlockSpec((None, H_M, dh), lambda i, j: (i, 0, 0)),
                 pl.BlockSpec((None, 1, LANES), lambda i, j: (i, 0, 0)))
    return pl.pallas_call(
        functools.partial(_mlstm_prompt_kernel, dh=dh, c=CHUNK),
        grid=(b, ns), in_specs=in_specs, out_specs=out_specs, out_shape=out_shape,
        scratch_shapes=[pltpu.VMEM((H_M, dh, dh), F32)],
        compiler_params=_params(("parallel", "arbitrary")), name="mlstm_prompt",
    )(q_t, k, v_t, gate, grow, gcol, g_bcast)


def _diff_prompt_kernel(lq1_ref, lk1_ref, lq2_ref, lk2_ref, q_ref, k_ref, vt_ref, gate_ref, g_ref,
                        o_ref, m_sc, l_sc, acc_sc, *, lam_init, dqk):
    tq = q_ref.shape[0]
    dv = 2 * dqk
    qi = pl.program_id(1)
    lam = _lambda(lq1_ref, lk1_ref, lq2_ref, lk2_ref, lam_init)
    lane = lax.broadcasted_iota(jnp.int32, (tq, dv), 1)
    key_i = lax.broadcasted_iota(jnp.int32, (tq, 2 * tq), 0)
    qry_i = lax.broadcasted_iota(jnp.int32, (tq, 2 * tq), 1)
    visible = key_i <= jnp.where(qry_i >= tq, qry_i - tq, qry_i)
    ones = jnp.ones((2 * SUBLANES, tq), BF16)
    q_both = []
    for h in range(H_D):
        q = q_ref[:, h * dv:(h + 1) * dv]
        zero = jnp.zeros_like(q)
        q_both.append(jnp.concatenate([jnp.where(lane < dqk, q, zero),
                                       jnp.where(lane >= dqk, q, zero)], axis=0))
    m_sc[...] = jnp.full(m_sc.shape, -jnp.inf, F32)
    l_sc[...] = jnp.zeros_like(l_sc)
    acc_sc[...] = jnp.zeros_like(acc_sc)

    def key_span(first_tile, n_tiles, on_diagonal):
        start = pl.multiple_of(first_tile * tq, tq)
        scores = []
        for h in range(H_D):
            hs = slice(h * dv, (h + 1) * dv)
            k = k_ref[pl.ds(start, n_tiles * tq), hs]
            scores.append(lax.dot_general(k, q_both[h], NT_DIMS,
                                          preferred_element_type=F32))
        for h in range(H_D):
            hs = slice(h * dv, (h + 1) * dv)
            st = scores[h]
            if on_diagonal:
                below = (n_tiles - 1) * tq
                diag = jnp.where(visible, st[below:, :], -jnp.inf)
                st = diag if n_tiles == 1 else jnp.concatenate([st[0:below, :], diag], axis=0)
            m_old = m_sc[h]
            m_new = jnp.maximum(m_old, jnp.max(st, axis=0, keepdims=True))
            alpha = jnp.exp(m_old - m_new)
            pt = jnp.exp(st - m_new).astype(BF16)
            pv = jnp.dot(jnp.concatenate([vt_ref[first_tile, hs, :], ones], axis=0), pt[0:tq],
                         preferred_element_type=F32)
            for t in range(1, n_tiles):
                pv += jnp.dot(jnp.concatenate([vt_ref[first_tile + t, hs, :], ones], axis=0),
                              pt[t * tq:(t + 1) * tq], preferred_element_type=F32)
            l_sc[h] = alpha * l_sc[h] + pv[dv:dv + 1, :]
            acc_sc[h] = alpha * acc_sc[h] + pv[0:dv, :]
            m_sc[h] = m_new

    def tile_pair(i, carry):
        key_span(2 * i, 2, False)
        return carry

    lax.fori_loop(0, qi // 2, tile_pair, 0)

    @pl.when(qi % 2 == 1)
    def _():
        key_span(qi - 1, 2, True)

    @pl.when(qi % 2 == 0)
    def _():
        key_span(qi, 1, True)

    for h in range(H_D):
        hs = slice(h * dv, (h + 1) * dv)
        on = acc_sc[h] / l_sc[h]
        o = (on[:, 0:tq] - lam * on[:, tq:2 * tq]).T
        o = o * lax.rsqrt(jnp.mean(o * o, axis=1, keepdims=True) + EPS) * g_ref[...]
        o_ref[:, hs] = (o * (1.0 - lam_init) * gate_ref[:, hs]).astype(o_ref.dtype)


def _diff_prompt(lams, dq, dk, dv_t, gate, g_norm, b, l, lam_init, gate_col0):
    d_d = dq.shape[1]
    dv_d = d_d // H_D
    tq = dv_t.shape[2]
    nq = l // tq
    lam_spec = pl.BlockSpec(lams[0].shape, lambda i, j: (0, 0))
    tok = pl.BlockSpec((tq, d_d), lambda i, j: (i * nq + j, 0))
    in_specs = [lam_spec] * 4 + [
        tok,
        pl.BlockSpec((l, d_d), lambda i, j: (i, 0)),
        pl.BlockSpec((nq, d_d, tq), lambda i, j: (i, 0, 0)),
        pl.BlockSpec((tq, d_d), lambda i, j: (i * nq + j, gate_col0 // d_d)),
        pl.BlockSpec((1, dv_d), lambda i, j: (0, 0))]
    return pl.pallas_call(
        functools.partial(_diff_prompt_kernel, lam_init=lam_init, dqk=dv_d // 2),
        grid=(b, nq), in_specs=in_specs, out_specs=tok,
        out_shape=jax.ShapeDtypeStruct((b * l, d_d), BF16),
        scratch_shapes=[pltpu.VMEM((H_D, 1, 2 * tq), F32), pltpu.VMEM((H_D, 1, 2 * tq), F32),
                        pltpu.VMEM((H_D, dv_d, 2 * tq), F32)],
        compiler_params=_params(("parallel", "parallel")), name="diff_attn_prompt",
    )(*lams, dq, dk, dv_t, gate, g_norm)


def _mem_kv_kernel(x_ref, w_ref, kf_ref, kb_ref, vf_ref, vb_ref):
    tm, d_c = kb_ref.shape
    dh = d_c // H_C
    x = x_ref[...]
    mk = jnp.dot(x, w_ref[:, 0:d_c], preferred_element_type=F32)
    mv = jnp.dot(x, w_ref[:, d_c:2 * d_c], preferred_element_type=F32)
    kb_ref[...] = mk.astype(kb_ref.dtype)
    vb_ref[...] = mv.astype(vb_ref.dtype)
    for h in range(H_C):
        sl = slice(h * dh, (h + 1) * dh)
        kf_ref[pl.ds(h, tm, stride=H_C), :] = mk[:, sl]
        vf_ref[pl.ds(h, tm, stride=H_C), :] = mv[:, sl]


def _mem_kv(mem2d, w, tm):
    m, d_model = mem2d.shape
    d_c = w.shape[1] // 2
    dh = d_c // H_C
    row = pl.BlockSpec((tm, d_c), lambda i: (i, 0))
    head_rows = pl.BlockSpec((tm * H_C, dh), lambda i: (i, 0))
    return pl.pallas_call(
        _mem_kv_kernel, grid=(m // tm,),
        in_specs=[pl.BlockSpec((tm, d_model), lambda i: (i, 0)),
                  pl.BlockSpec(w.shape, lambda i: (0, 0))],
        out_specs=(head_rows, row, head_rows, row),
        out_shape=(jax.ShapeDtypeStruct((m * H_C, dh), F32), jax.ShapeDtypeStruct((m, d_c), BF16),
                   jax.ShapeDtypeStruct((m * H_C, dh), F32), jax.ShapeDtypeStruct((m, d_c), BF16)),
        compiler_params=_params(("parallel",)), name="memory_kv",
    )(mem2d, w)


def _mem_prompt_kernel(q_ref, k_ref, v_ref, gate_ref, o_ref, *, dh):
    slices = [slice(h * dh, (h + 1) * dh) for h in range(H_C)]
    scores = [lax.dot_general(q_ref[:, sl], k_ref[:, sl], NT_DIMS, preferred_element_type=F32)
              for sl in slices]
    probs = []
    for s in scores:
        s = s * (dh ** -0.5)
        p = jnp.exp(s - jnp.max(s, axis=1, keepdims=True))
        probs.append((p / jnp.sum(p, axis=1, keepdims=True)).astype(BF16))
    for p, sl in zip(probs, slices):
        o = jnp.dot(p, v_ref[:, sl], preferred_element_type=F32)
        o_ref[:, sl] = (o * gate_ref[:, sl]).astype(o_ref.dtype)


def _mem_prompt(cq, mk, mv, gate, b, l, n_mem, tl, gate_col0):
    d_c = cq.shape[1]
    nl = l // tl
    tok = pl.BlockSpec((tl, d_c), lambda i, j: (i * nl + j, 0))
    mem = pl.BlockSpec((n_mem, d_c), lambda i, j: (i, 0))
    return pl.pallas_call(
        functools.partial(_mem_prompt_kernel, dh=d_c // H_C),
        grid=(b, nl),
        in_specs=[tok, mem, mem,
                  pl.BlockSpec((tl, d_c), lambda i, j: (i * nl + j, gate_col0 // d_c))],
        out_specs=tok, out_shape=jax.ShapeDtypeStruct((b * l, d_c), BF16),
        compiler_params=_params(("parallel", "parallel")), name="memory_attn_prompt",
    )(cq, mk, mv, gate)


def _merge_kernel(x_ref, hm_ref, hd_ref, hc_ref, w_ref, g_ref, b_ref, y_ref, *, alpha):
    tm = x_ref.shape[0]
    n_strips = max(1, tm // 256)
    rows = tm // n_strips
    for r in range(n_strips):
        rs = slice(r * rows, (r + 1) * rows)
        h = jnp.concatenate([hm_ref[rs, :], hd_ref[rs, :], hc_ref[rs, :]], axis=1)
        y = alpha * x_ref[rs, :] + jnp.dot(h, w_ref[...], preferred_element_type=F32)
        mu = jnp.mean(y, axis=1, keepdims=True)
        var = jnp.mean(jnp.square(y - mu), axis=1, keepdims=True)
        y_ref[rs, :] = (y - mu) * lax.rsqrt(var + EPS) * g_ref[...] + b_ref[...]


def _merge(x2d, h_m, h_d, h_c, w_out, ln_g, ln_b, tm, alpha):
    m, d_model = x2d.shape

    def row(n):
        return pl.BlockSpec((tm, n), lambda i: (i, 0))

    def whole(a):
        return pl.BlockSpec(a.shape, lambda i: (0, 0))

    return pl.pallas_call(
        functools.partial(_merge_kernel, alpha=alpha), grid=(m // tm,),
        in_specs=[row(d_model), row(h_m.shape[1]), row(h_d.shape[1]), row(h_c.shape[1]),
                  whole(w_out), whole(ln_g), whole(ln_b)],
        out_specs=row(d_model), out_shape=jax.ShapeDtypeStruct((m, d_model), F32),
        compiler_params=_params(("parallel",)), name="output_projection",
    )(x2d, h_m, h_d, h_c, w_out, ln_g, ln_b)


def _mlstm_step_kernel(q_ref, k_ref, v_ref, gate_ref, ig_ref, fg_ref, m_ref, n_ref, c_ref,
                       g_ref, h_ref, cn_ref, nn_ref, mn_ref, *, dh):
    def one_sequence(b, carry):
        for h in range(H_M):
            sl = slice(h * dh, (h + 1) * dh)
            q = q_ref[b, :, sl]
            k = k_ref[b, :, sl]
            v = v_ref[b, :, sl]
            ig = ig_ref[b, h]
            lf = _log_sigmoid(fg_ref[b, h])
            m_prev = m_ref[b, h]
            n_old = n_ref[b, h]
            c_old = c_ref[b, h]

            inter = lf + m_prev
            mt = jnp.maximum(ig, inter)
            w_in = jnp.exp(ig - mt)
            w_st = jnp.exp(inter - mt)
            s = jnp.sum(q * k, axis=1, keepdims=True) * w_in
            qc2 = jnp.dot(_split_rows(q, ("hi", "lo")), c_old.astype(BF16),
                          preferred_element_type=F32)
            qc = qc2[0:1, :] + qc2[1:2, :]
            outer = lax.dot_general(_split_rows(k, ("hi", "hi", "lo")),
                                    _split_rows(v, ("hi", "lo", "hi")), TN_DIMS,
                                    preferred_element_type=F32)
            num = s * v + w_st * qc
            den = s + w_st * jnp.sum(q * n_old, axis=1, keepdims=True)
            hc = num / jnp.maximum(jnp.abs(den), jnp.exp(-mt))

            cn_ref[b, h] = w_st * c_old + w_in * outer
            nn_ref[b, h] = w_st * n_old + w_in * k
            mn_ref[b, h] = mt

            mu = jnp.mean(hc, axis=1, keepdims=True)
            var = jnp.mean(jnp.square(hc - mu), axis=1, keepdims=True)
            hn = (hc - mu) * lax.rsqrt(var + EPS) * g_ref[:, sl]
            h_ref[b, :, sl] = (hn * gate_ref[b, :, sl]).astype(h_ref.dtype)
        return carry

    lax.fori_loop(0, q_ref.shape[0], one_sequence, 0)


def _mlstm_step(q, k, v, gate, ig, fg, m0, n0, c0, g_norm, tb):
    bd, _, d_m = q.shape
    dh = d_m // H_M

    def blk(shape):
        nd = len(shape)
        return pl.BlockSpec((tb,) + tuple(shape), lambda i: (i,) + (0,) * nd)

    tok = blk((1, d_m))
    one = blk((H_M, 1, 1))
    in_specs = [tok, tok, tok, tok, one, one, one, blk((H_M, 1, dh)), blk((H_M, dh, dh)),
                pl.BlockSpec((1, d_m), lambda i: (0, 0))]
    out_shape = (jax.ShapeDtypeStruct((bd, 1, d_m), BF16),
                 jax.ShapeDtypeStruct((bd, H_M, dh, dh), F32),
                 jax.ShapeDtypeStruct((bd, H_M, 1, dh), F32),
                 jax.ShapeDtypeStruct((bd, H_M, 1, 1), F32))
    out_specs = (tok, blk((H_M, dh, dh)), blk((H_M, 1, dh)), one)
    return pl.pallas_call(
        functools.partial(_mlstm_step_kernel, dh=dh), grid=(bd // tb,),
        in_specs=in_specs, out_specs=out_specs, out_shape=out_shape,
        compiler_params=_params(("parallel",)), name="mlstm_step",
    )(q, k, v, gate, ig, fg, m0, n0, c0, g_norm)


def _head_rows(row, n_heads, reps):
    dh = row.shape[1] // n_heads
    heads = [row[:, h * dh:(h + 1) * dh] for h in range(n_heads)]
    return jnp.concatenate(heads * reps, axis=0)


def _own_head(n_rows, n_cols, n_heads):
    r = lax.broadcasted_iota(jnp.int32, (n_rows, n_cols), 0)
    c = lax.broadcasted_iota(jnp.int32, (n_rows, n_cols), 1)
    return (c % n_heads) == (r % n_heads)


def _diff_decode_kernel(pt_ref, lq1_ref, lk1_ref, lq2_ref, lk2_ref, q_ref, kn_ref, vn_ref,
                        gate_ref, g_ref, *rest, n_pages, lam_init):
    del pt_ref
    tb = q_ref.shape[0]
    k_refs = rest[:tb * n_pages]
    v_refs = rest[tb * n_pages:2 * tb * n_pages]
    o_ref = rest[2 * tb * n_pages]
    n_rows, dv_d = k_refs[0].shape
    dqk = dv_d // 2
    lam = _lambda(lq1_ref, lk1_ref, lq2_ref, lk2_ref, lam_init)
    r_idx = lax.broadcasted_iota(jnp.int32, (2 * H_D, dv_d), 0)
    l_idx = lax.broadcasted_iota(jnp.int32, (2 * H_D, dv_d), 1)
    own_map = (l_idx // dqk) == (r_idx // H_D)
    own = _own_head(2 * H_D, n_rows, H_D)

    q8s, scores = [], []
    for b in range(tb):
        q8 = jnp.where(own_map, _head_rows(q_ref[b].astype(F32), H_D, 2), 0.0)
        q8b = q8.astype(BF16)
        q8s.append(q8)
        scores.append([jnp.where(own, lax.dot_general(q8b, k_refs[b * n_pages + j][...].astype(BF16),
                                                      NT_DIMS, preferred_element_type=F32), -jnp.inf)
                       for j in range(n_pages)])

    weights = []
    for b in range(tb):
        kn4 = kn_ref[b]
        s_new = jnp.sum(q8s[b] * jnp.concatenate([kn4, kn4], axis=0), axis=1, keepdims=True)
        mx = s_new
        for s in scores[b]:
            mx = jnp.maximum(mx, jnp.max(s, axis=1, keepdims=True))
        probs = [jnp.exp(s - mx) for s in scores[b]]
        p_new = jnp.exp(s_new - mx)
        total = p_new
        for p in probs:
            total = total + jnp.sum(p, axis=1, keepdims=True)
        inv = 1.0 / total

        def combine(p, inv=inv):
            pn = p * inv
            return pn - lam * pltpu.roll(pn, SUBLANES - H_D, 0)

        a_new = combine(jnp.broadcast_to(p_new, (SUBLANES, LANES)))[:, 0:1]
        weights.append((a_new, [combine(p).astype(BF16) for p in probs]))

    for b in range(tb):
        a_new, a_pages = weights[b]
        vn4 = vn_ref[b]
        acc = a_new * jnp.concatenate([vn4, vn4], axis=0)
        for j in range(n_pages):
            acc = acc + jnp.dot(a_pages[j], v_refs[b * n_pages + j][...].astype(BF16),
                                preferred_element_type=F32)
        for h in range(H_D):
            sl = slice(h * dv_d, (h + 1) * dv_d)
            o = acc[h:h + 1, :]
            o = o * lax.rsqrt(jnp.mean(o * o, axis=1, keepdims=True) + EPS) * g_ref[...]
            o_ref[b, :, sl] = (o * (1.0 - lam_init) * gate_ref[b, :, sl]).astype(o_ref.dtype)


def _diff_decode(page_table, lams, dq, dk_new, dv_new, gate, g_norm, cache_k, cache_v,
                 lam_init, gate_col0, tb):
    bd, _, d_d = dq.shape
    n_pages = page_table.shape[1]
    _, n_rows, dv_d = cache_k.shape
    lam_spec = pl.BlockSpec(lams[0].shape, lambda i, pt: (0, 0))
    tok = pl.BlockSpec((tb, 1, d_d), lambda i, pt: (i, 0, 0))
    page_specs = [pl.BlockSpec((None, n_rows, dv_d),
                               lambda i, pt, b=b, j=j: (pt[i * tb + b, j], 0, 0))
                  for b in range(tb) for j in range(n_pages)]
    new_tok = pl.BlockSpec((tb, H_D, dv_d), lambda i, pt: (i, 0, 0))
    in_specs = ([lam_spec] * 4 + [tok, new_tok, new_tok,
                                  pl.BlockSpec((tb, 1, d_d), lambda i, pt: (i, 0, gate_col0 // d_d)),
                                  pl.BlockSpec((1, dv_d), lambda i, pt: (0, 0))]
                + page_specs + page_specs)
    grid_spec = pltpu.PrefetchScalarGridSpec(
        num_scalar_prefetch=1, grid=(bd // tb,), in_specs=in_specs, out_specs=tok)
    n_ops = tb * n_pages
    return pl.pallas_call(
        functools.partial(_diff_decode_kernel, n_pages=n_pages, lam_init=lam_init),
        grid_spec=grid_spec, out_shape=jax.ShapeDtypeStruct((bd, 1, d_d), BF16),
        compiler_params=_params(("parallel",)), name="diff_attn_decode",
    )(page_table, *lams, dq, dk_new, dv_new, gate, g_norm,
      *([cache_k] * n_ops), *([cache_v] * n_ops))


def _mem_decode_kernel(q_ref, k_ref, v_ref, gate_ref, o_ref):
    tb, n_rows, dh = k_ref.shape
    own = _own_head(SUBLANES, n_rows, H_C)
    scores = []
    for b in range(tb):
        q8 = _head_rows(q_ref[b].astype(F32), H_C, SUBLANES // H_C).astype(BF16)
        scores.append(lax.dot_general(q8, k_ref[b].astype(BF16), NT_DIMS,
                                      preferred_element_type=F32))
    probs = []
    for s in scores:
        s = jnp.where(own, s * (dh ** -0.5), -jnp.inf)
        p = jnp.exp(s - jnp.max(s, axis=1, keepdims=True))
        probs.append((p / jnp.sum(p, axis=1, keepdims=True)).astype(BF16))
    for b in range(tb):
        acc = jnp.dot(probs[b], v_ref[b].astype(BF16), preferred_element_type=F32)
        for h in range(H_C):
            sl = slice(h * dh, (h + 1) * dh)
            o_ref[b, :, sl] = (acc[h:h + 1, :] * gate_ref[b, :, sl]).astype(o_ref.dtype)


def _mem_decode(cq, mem_k, mem_v, gate, gate_col0, tb):
    bd, _, d_c = cq.shape
    _, n_rows, dh = mem_k.shape
    tok = pl.BlockSpec((tb, 1, d_c), lambda i: (i, 0, 0))
    mem = pl.BlockSpec((tb, n_rows, dh), lambda i: (i, 0, 0))
    return pl.pallas_call(
        _mem_decode_kernel, grid=(bd // tb,),
        in_specs=[tok, mem, mem, pl.BlockSpec((tb, 1, d_c), lambda i: (i, 0, gate_col0 // d_c))],
        out_specs=tok, out_shape=jax.ShapeDtypeStruct((bd, 1, d_c), BF16),
        compiler_params=_params(("parallel",)), name="memory_attn_decode",
    )(cq, mem_k, mem_v, gate)


def _rope_tables(pos, n_rows, dqk):
    half = dqk // 2
    lane = jnp.arange(LANES)
    inv_freq = ROPE_THETA ** (-(lane % half).astype(F32) / half)
    ang = pos.astype(F32)[:, None] * inv_freq[None, :]
    sin_t = jnp.where((lane % dqk) < half, -jnp.sin(ang), jnp.sin(ang))
    return (jnp.broadcast_to(jnp.cos(ang), (n_rows, LANES)), jnp.broadcast_to(sin_t, (n_rows, LANES)))


class _Tiles(NamedTuple):
    token: int = 256
    merge_rows: int = 1024
    mem_kv_rows: int = 512
    mem_q_rows: int = 1024
    step_seqs: int = 8
    decode_seqs: int = 2
    mem_decode_seqs: int = 16


def _split_weights(w_in_l, b_if_l, d_m, d_d):
    g0 = 4 * d_m
    d0 = g0 + N_GATE
    wqt = w_in_l[:, 0:d_m].T.astype(BF16)
    wvt = w_in_l[:, 2 * d_m:3 * d_m].T.astype(BF16)
    wm = jnp.concatenate([w_in_l[:, d_m:2 * d_m], w_in_l[:, 3 * d_m:g0]], axis=1).astype(BF16)
    w_gates = w_in_l[:, g0:d0]
    wg = jnp.pad(w_gates, ((0, 0), (0, LANES - N_GATE))).astype(BF16)
    wgt = w_gates.T.astype(BF16)
    wd = w_in_l[:, d0:d0 + 4 * d_d].astype(BF16)
    wgate = w_in_l[:, d0 + 4 * d_d:].astype(BF16)
    bcol = jnp.pad(b_if_l, (0, LANES - N_GATE)).reshape(1, LANES).astype(F32)
    brow = jnp.broadcast_to(b_if_l.astype(F32)[:, None], (N_GATE, LANES))
    return wqt, wvt, wm, wg, wgt, wd, wgate, bcol, brow


def kernel(x_prompt, x_sample, state_mlstm_c, state_mlstm_n, state_mlstm_m, cache_diff_k, cache_diff_v,
           cache_mem_k, cache_mem_v, page_table, mem_prompt, w_in, b_if, w_mlstm_norm,
           lambda_q1, lambda_k1, lambda_q2, lambda_k2, w_diff_norm, w_mem_kv, w_out, ln_g, ln_b):
    depth = w_in.shape[0]
    assert depth == 1, "single-layer trunk"
    bp, lp, d_model = x_prompt.shape
    bd, ls, _ = x_sample.shape
    assert ls == 1
    d_mix = w_out.shape[1]
    d_m = d_mix // 2
    d_d = d_mix // 4
    d_c = d_mix // 4
    dh_m = d_m // H_M
    dv_d = d_d // H_D
    dqk = dv_d // 2
    assert dv_d == LANES and d_c // H_C == LANES, "one attention head per lane tile"
    n_mem = mem_prompt.shape[1]
    n_pool, page = cache_diff_k.shape[1], cache_diff_k.shape[2]
    past_len = page_table.shape[1] * page
    alpha = (2 * depth) ** 0.25
    layer = 0
    lam_init = 0.8 - 0.6 * math.exp(-0.3 * layer)

    tiles = _Tiles()
    assert lp % tiles.token == 0 and lp % tiles.mem_q_rows == 0 and tiles.token % CHUNK == 0
    assert (bp * lp) % tiles.merge_rows == 0 and (bp * n_mem) % tiles.mem_kv_rows == 0
    assert bd % tiles.step_seqs == 0 and bd % tiles.decode_seqs == 0
    assert bd % tiles.mem_decode_seqs == 0
    wts = _split_weights(w_in[layer], b_if[layer], d_m, d_d)
    lams = tuple(a[layer].reshape(1, dqk).astype(F32)
                 for a in (lambda_q1, lambda_k1, lambda_q2, lambda_k2))
    g_m = w_mlstm_norm[layer].reshape(1, d_m).astype(F32)
    g_d = w_diff_norm[layer].reshape(1, dv_d).astype(F32)
    w_out_b = w_out[layer].astype(BF16)
    g_ln = ln_g[layer].reshape(1, d_model).astype(F32)
    b_ln = ln_b[layer].reshape(1, d_model).astype(F32)
    gate_d0 = d_m
    gate_c0 = d_m + d_d

    mp = bp * lp
    xp2d = x_prompt.reshape(mp, d_model)
    cos_p, sin_p = _rope_tables(jnp.arange(lp), lp, dqk)
    (q_m, k_m, v_m, gcol, grow, dq, dk_f, dk_b, dv_f, dv_t, cq, gate) = _input_projection(
        xp2d, wts, cos_p, sin_p, tm=tiles.token, qkv_dtype=BF16, transposed_qv=True)
    h_m, c_p, n_p, m_p = _mlstm_prompt(q_m, k_m, v_m, gate, grow, gcol, g_m, bp, lp)
    h_d = _diff_prompt(lams, dq, dk_b, dv_t, gate, g_d, bp, lp, lam_init, gate_d0)
    mk_f, mk_b, mv_f, mv_b = _mem_kv(mem_prompt.reshape(bp * n_mem, d_model).astype(BF16),
                                     w_mem_kv[layer].astype(BF16), tm=tiles.mem_kv_rows)
    h_c = _mem_prompt(cq, mk_b, mv_b, gate, bp, lp, n_mem, tiles.mem_q_rows, gate_c0)
    y_p = _merge(xp2d, h_m, h_d, h_c, w_out_b, g_ln, b_ln, tiles.merge_rows, alpha)

    xs2d = x_sample.reshape(bd, d_model)
    cos_s, sin_s = _rope_tables(jnp.full((1,), past_len), bd, dqk)
    (qs, ks, vs, gcol_s, _, dq_s, dk_s, _, dv_s, _, cq_s, gate_s) = _input_projection(
        xs2d, wts, cos_s, sin_s, tm=bd, qkv_dtype=F32, transposed_qv=False)

    def tok3(a):
        return a.reshape(bd, 1, a.shape[1])

    ig_s = gcol_s[:, 0:H_M].reshape(bd, H_M, 1, 1)
    fg_s = gcol_s[:, H_M:N_GATE].reshape(bd, H_M, 1, 1)
    hs_m, c_s, n_s, m_s = _mlstm_step(
        tok3(qs), tok3(ks), tok3(vs), tok3(gate_s), ig_s, fg_s,
        state_mlstm_m[layer].astype(F32).reshape(bd, H_M, 1, 1),
        state_mlstm_n[layer].astype(F32).reshape(bd, H_M, 1, dh_m),
        state_mlstm_c[layer].astype(F32), g_m, tb=tiles.step_seqs)
    hs_d = _diff_decode(page_table, lams, tok3(dq_s), dk_s.reshape(bd, H_D, dv_d),
                        dv_s.reshape(bd, H_D, dv_d), tok3(gate_s), g_d,
                        cache_diff_k[layer].reshape(n_pool, page * H_D, dv_d),
                        cache_diff_v[layer].reshape(n_pool, page * H_D, dv_d), lam_init, gate_d0,
                        tb=tiles.decode_seqs)
    hs_c = _mem_decode(tok3(cq_s), cache_mem_k[layer].reshape(bd, n_mem * H_C, d_c // H_C),
                       cache_mem_v[layer].reshape(bd, n_mem * H_C, d_c // H_C), tok3(gate_s),
                       gate_c0, tb=tiles.mem_decode_seqs)
    y_s = _merge(xs2d, hs_m.reshape(bd, d_m), hs_d.reshape(bd, d_d), hs_c.reshape(bd, d_c),
                 w_out_b, g_ln, b_ln, bd, alpha)

    return (y_p.reshape(bp, lp, d_model), y_s.reshape(bd, ls, d_model),
            c_p[None], n_p[None], m_p[:, 0, 0:H_M][None],
            dk_f.reshape(1, bp, lp, H_D, dv_d), dv_f.reshape(1, bp, lp, H_D, dv_d),
            mk_f.reshape(1, bp, n_mem, H_C, d_c // H_C), mv_f.reshape(1, bp, n_mem, H_C, d_c // H_C),
            c_s[None], n_s.reshape(1, bd, H_M, dh_m), m_s.reshape(1, bd, H_M),
            dk_s.reshape(1, bd, ls, H_D, dv_d), dv_s.reshape(1, bd, ls, H_D, dv_d))
```
